```python
import math
import jax, jax.numpy as jnp
from jax import lax
import numpy as np

D_MODEL = 1024
BATCH = 32
SEQ = 256
DEPTH = 2
DEC_BATCH = 4
DEC_SEQ = 2048
PAST_LEN = 512

GRID_W = 64
EPS = 1e-6
ATT_HEADS = 8
ATT_KV_HEADS = 2
HEAD_DIM = 64
ATT_GROUP = ATT_HEADS // ATT_KV_HEADS
ATT_W = ATT_HEADS * HEAD_DIM
KV_W = ATT_KV_HEADS * HEAD_DIM
ATT_COLS = ATT_W + 2 * KV_W
WINDOW = 128
BLK = 128
ROPE_BASE = 10000.0
RW_HEADS = 4
RW_HD = 64
RW_W = RW_HEADS * RW_HD
RW_W_RANK = 32
RW_A_RANK = 32
RW_G_RANK = 64
RW_COLS = 3 * RW_W + RW_W_RANK + RW_A_RANK + RW_G_RANK
RW_DECAY_SCALE = 0.606531
RW_GN_EPS = 64e-5
SSD_HEADS = 4
SSD_HD = 64
SSD_W = SSD_HEADS * SSD_HD
SSD_GROUPS = 2
SSD_STATE = 64
SSD_CONV_K = 3
SSD_CONV_DIM = SSD_W + 2 * SSD_GROUPS * SSD_STATE
SSD_CHUNK = 128
SSD_COLS = SSD_W + SSD_CONV_DIM + SSD_HEADS
IN_COLS = ATT_COLS + RW_COLS + SSD_COLS
MIX_W = ATT_W + RW_W + SSD_W
N_EXPERTS = 16
N_GROUPS = 4
EXP_PER_GROUP = N_EXPERTS // N_GROUPS
TOP_K = 2
D_FF = 256
MOE_BLK = 128

kernel_name = 'hybrid_diffusion_prefix_trunk_step'


def split_cols(u, sizes):
    return jnp.split(u, [int(s) for s in np.cumsum(sizes)[:-1]], axis=-1)


def rmsnorm(x, w):
    xf = x.astype(jnp.float32)
    y = xf * lax.rsqrt(jnp.mean(xf * xf, -1, keepdims=True) + EPS)
    return (y * w.astype(jnp.float32)).astype(x.dtype)


def modulation(cond, w_mod, b_mod):
    m = (jax.nn.silu(cond) @ w_mod + b_mod)[..., None, :]
    return jnp.split(m, 6, axis=-1)


def rope_axis(x, cos, sin):
    x1, x2 = jnp.split(x, 2, -1)
    return jnp.concatenate([x1 * cos - x2 * sin, x2 * cos + x1 * sin], -1)


def apply_axial_rope(x):
    L = x.shape[1]
    t = jnp.arange(L)
    pos = jnp.stack([t // GRID_W, t % GRID_W], 0).astype(jnp.float32)
    n_freq = HEAD_DIM // 4
    inv = ROPE_BASE ** (-jnp.arange(n_freq, dtype=jnp.float32) / n_freq)
    ang = (pos[:, :, None] * inv).reshape((2, L) + (1,) * (x.ndim - 3) + (n_freq,))
    cos, sin = jnp.cos(ang).astype(x.dtype), jnp.sin(ang).astype(x.dtype)
    xr, xc = jnp.split(x, 2, -1)
    return jnp.concatenate([rope_axis(xr, cos[0], sin[0]), rope_axis(xc, cos[1], sin[1])], -1)


def attend(q, k, v, sink, mask):
    s = jnp.einsum('bqhgd,bjhd->bhgqj', q, k).astype(jnp.float32) * HEAD_DIM ** -0.5
    if mask is not None:
        s = jnp.where(mask, s, -1e30)
    sk = jnp.broadcast_to(sink.astype(jnp.float32)[None, :, :, None, None], s.shape[:-1] + (1,))
    p = jax.nn.softmax(jnp.concatenate([s, sk], -1), axis=-1)[..., :-1]
    return jnp.einsum('bhgqj,bjhd->bqhgd', p.astype(v.dtype), v)


def attn_context(q, k, v, sink):
    b, Lc = q.shape[:2]
    qb = jnp.moveaxis(q.reshape((b, Lc // BLK, BLK) + q.shape[2:]), 1, 0)
    out = lax.map(lambda qi: attend(qi, k, v, sink, None), qb)
    return jnp.moveaxis(out, 0, 1).reshape(q.shape)


def attn_latent(q, k, v, k_ctx, v_ctx, sink):
    b, L = q.shape[:2]
    nb = L // BLK
    pad = ((0, 0), (BLK, BLK), (0, 0), (0, 0))
    band = lambda t: jnp.concatenate([t[:, :-2], t[:, 1:-1], t[:, 2:]], axis=2)
    kb = jnp.moveaxis(band(jnp.pad(k, pad).reshape(b, nb + 2, BLK, ATT_KV_HEADS, HEAD_DIM)), 1, 0)
    vb = jnp.moveaxis(band(jnp.pad(v, pad).reshape(b, nb + 2, BLK, ATT_KV_HEADS, HEAD_DIM)), 1, 0)
    qb = jnp.moveaxis(q.reshape(b, nb, BLK, ATT_KV_HEADS, ATT_GROUP, HEAD_DIM), 1, 0)
    qpos = jnp.arange(nb)[:, None] * BLK + jnp.arange(BLK)[None]
    kpos = (jnp.arange(nb)[:, None] - 1) * BLK + jnp.arange(3 * BLK)[None]
    band_mask = ((jnp.abs(qpos[:, :, None] - kpos[:, None, :]) <= WINDOW)
                 & ((kpos >= 0) & (kpos < L))[:, None, :])
    mask = jnp.concatenate([band_mask, jnp.ones((nb, BLK, k_ctx.shape[1]), bool)], -1)

    def block(args):
        qi, ki, vi, mi = args
        return attend(qi, jnp.concatenate([ki, k_ctx], 1), jnp.concatenate([vi, v_ctx], 1), sink, mi)

    out = lax.map(block, (qb, kb, vb, mask))
    return jnp.moveaxis(out, 0, 1).reshape(q.shape)


def shift_mix(u, mu):
    prev = jnp.pad(u[:, :-1], ((0, 0), (1, 0), (0, 0)))
    nxt = jnp.pad(u[:, 1:], ((0, 0), (0, 1), (0, 0)))
    return u + mu[0] * (prev - u) + mu[1] * (nxt - u)


def rwkv_scan(r, w, k, v, kk, bb, s0, reverse):
    def step(S, xs):
        rt, wt, kt, vt, kkt, bt = xs
        sa = jnp.einsum('bhvk,bhk->bhv', S, kkt)
        S = S * wt[:, :, None, :] - sa[..., None] * bt[:, :, None, :] + vt[..., None] * kt[:, :, None, :]
        return S, jnp.einsum('bhvk,bhk->bhv', S, rt)
    xs = tuple(jnp.moveaxis(t, 1, 0) for t in (r, w, k, v, kk, bb))
    S, y = lax.scan(step, s0, xs, reverse=reverse)
    return jnp.moveaxis(y, 0, 1), S


def rwkv_mixer(u, s0, mu, w0, w_up, a0, a_up, g_up, k_k, k_a, r_k, ln_w, ln_b):
    b, L, _ = u.shape
    u = shift_mix(u, mu).astype(jnp.float32)
    r, k, v, wd, ad, gd = split_cols(u, [RW_W, RW_W, RW_W, RW_W_RANK, RW_A_RANK, RW_G_RANK])
    heads = lambda t: t.reshape(b, L, RW_HEADS, RW_HD)
    g = jax.nn.sigmoid(gd) @ g_up
    kk = heads(k * k_k)
    kk = kk * lax.rsqrt(jnp.sum(kk * kk, -1, keepdims=True) + 1e-12)
    s0 = s0.astype(jnp.float32)
    y = jnp.zeros((b, L, RW_HEADS, RW_HD), jnp.float32)
    states = []
    for d in range(2):
        w = jnp.exp(-RW_DECAY_SCALE * jax.nn.sigmoid(w0[d] + jnp.tanh(wd) @ w_up[d]))
        a = jax.nn.sigmoid(a0[d] + ad @ a_up[d])
        kd = k * (1 + (a - 1) * k_a)
        yd, sd = rwkv_scan(heads(r), heads(w), heads(kd), heads(v), kk, kk * heads(a), s0[:, d], d == 1)
        y = y + yd
        states.append(sd)
    mean = jnp.mean(y, -1, keepdims=True)
    var = jnp.mean(jnp.square(y - mean), -1, keepdims=True)
    y = ((y - mean) * lax.rsqrt(var + RW_GN_EPS)).reshape(b, L, RW_W) * ln_w + ln_b
    bonus = jnp.sum(heads(r) * heads(k) * r_k, -1, keepdims=True) * heads(v)
    y = (y + bonus.reshape(b, L, RW_W)) * g
    return y, jnp.stack(states, 1)


def centred_dwconv(x, w, bias):
    L = x.shape[1]
    p = SSD_CONV_K // 2
    xp = jnp.pad(x, ((0, 0), (p, p), (0, 0)))
    return sum(xp[:, i:i + L] * w[i] for i in range(SSD_CONV_K)) + bias


def ssd_chunked(x, dt, A, B, C, h0):
    b, L, H, P = x.shape
    nc = L // SSD_CHUNK
    ch = lambda t: t.reshape((b, nc, SSD_CHUNK) + t.shape[2:])
    xdt, Bc, Cc = ch(x * dt[..., None]), ch(B), ch(C)
    dA_cs = jnp.cumsum(ch(dt * A), axis=2)
    diff = dA_cs[:, :, :, None, :] - dA_cs[:, :, None, :, :]
    causal = jnp.tril(jnp.ones((SSD_CHUNK, SSD_CHUNK), bool))[None, None, :, :, None]
    Lmat = jnp.exp(jnp.where(causal, diff, -jnp.inf))
    y_diag = jnp.einsum('bcihn,bcjhn,bcijh,bcjhp->bcihp', Cc, Bc, Lmat, xdt)
    decay_in = jnp.exp(dA_cs[:, :, -1:] - dA_cs)
    chunk_states = jnp.einsum('bcjhn,bcjh,bcjhp->bchpn', Bc, decay_in, xdt)
    chunk_decay = jnp.exp(dA_cs[:, :, -1])

    def step(h, xs):
        st, dec = xs
        return h * dec[:, :, None, None] + st, h

    h_final, h_enter = lax.scan(step, h0, (jnp.moveaxis(chunk_states, 1, 0), jnp.moveaxis(chunk_decay, 1, 0)))
    h_enter = jnp.moveaxis(h_enter, 0, 1)
    y_off = jnp.einsum('bcihn,bchpn,bcih->bcihp', Cc, h_enter, jnp.exp(dA_cs))
    return (y_diag + y_off).reshape(b, L, H, P), h_final


def ssd_mixer(u, h0, conv_w, conv_b, dt_bias, a_log, d_skip, norm_w):
    b, L, _ = u.shape
    u = u.astype(jnp.float32)
    z, xbc, dt_raw = split_cols(u, [SSD_W, SSD_CONV_DIM, SSD_HEADS])
    xbc = jax.nn.silu(centred_dwconv(xbc, conv_w, conv_b))
    x, B, C = split_cols(xbc, [SSD_W, SSD_GROUPS * SSD_STATE, SSD_GROUPS * SSD_STATE])
    x = x.reshape(b, L, SSD_HEADS, SSD_HD)
    rep = SSD_HEADS // SSD_GROUPS
    B = jnp.repeat(B.reshape(b, L, SSD_GROUPS, SSD_STATE), rep, axis=2)
    C = jnp.repeat(C.reshape(b, L, SSD_GROUPS, SSD_STATE), rep, axis=2)
    h0 = h0.astype(jnp.float32)
    y = d_skip[:, None] * x
    states = []
    for d in range(2):
        flip = (lambda t: jnp.flip(t, 1)) if d == 1 else (lambda t: t)
        dt = jax.nn.softplus(dt_raw + dt_bias[d])
        A = -jnp.exp(a_log[d])
        yd, hd = ssd_chunked(flip(x), flip(dt), A, flip(B), flip(C), h0[:, d])
        y = y + flip(yd)
        states.append(hd)
    y = rmsnorm(y.reshape(b, L, SSD_W) * jax.nn.silu(z), norm_w)
    return y, jnp.stack(states, 1)


def moe(h, router_w, router_bias, w_gate, w_up, w_down):
    b, L, D = h.shape
    t = h.reshape(-1, MOE_BLK, D)

    def block(xb):
        scores = jax.nn.sigmoid((xb @ router_w).astype(jnp.float32))
        sel = scores + router_bias
        grp = lax.top_k(sel.reshape(-1, N_GROUPS, EXP_PER_GROUP), TOP_K)[0].sum(-1)
        g_idx = jnp.argmax(grp, -1)
        in_grp = (jnp.arange(N_EXPERTS) // EXP_PER_GROUP)[None] == g_idx[:, None]
        _, e_idx = lax.top_k(jnp.where(in_grp, sel, -jnp.inf), TOP_K)
        wts = jnp.take_along_axis(scores, e_idx, -1)
        wts = wts / jnp.sum(wts, -1, keepdims=True)
        combine = jnp.sum(jax.nn.one_hot(e_idx, N_EXPERTS, dtype=jnp.float32) * wts[..., None], 1)
        hid = jax.nn.silu(jnp.einsum('td,edf->tef', xb, w_gate)) * jnp.einsum('td,edf->tef', xb, w_up)
        return jnp.einsum('tef,te,efd->td', hid, combine.astype(hid.dtype), w_down)

    return lax.map(block, t).reshape(b, L, D)


def trunk(x, cond, is_ctx, cache_k, cache_v, state_rwkv, state_ssd, P):
    b, L, _ = x.shape
    new_k, new_v, new_rw, new_ssd = [], [], [], []
    for l in range(DEPTH):
        sh1, sc1, g1, sh2, sc2, g2 = modulation(cond, P['w_mod'][l], P['b_mod'][l])
        h = rmsnorm(x, P['norm1_w'][l]) * (1 + sc1) + sh1
        u_att, u_rw, u_ssd = split_cols(h @ P['w_in'][l], [ATT_COLS, RW_COLS, SSD_COLS])
        q, k, v = split_cols(u_att, [ATT_W, KV_W, KV_W])
        q = rmsnorm(q.reshape(b, L, ATT_KV_HEADS, ATT_GROUP, HEAD_DIM), P['q_norm_w'][l])
        k = rmsnorm(k.reshape(b, L, ATT_KV_HEADS, HEAD_DIM), P['k_norm_w'][l])
        v = v.reshape(b, L, ATT_KV_HEADS, HEAD_DIM)
        sink = P['attn_sink'][l].reshape(ATT_KV_HEADS, ATT_GROUP)
        if is_ctx:
            o_att = attn_context(q, k, v, sink)
            rw0 = jnp.zeros((b, 2, RW_HEADS, RW_HD, RW_HD), jnp.float32)
            ssd0 = jnp.zeros((b, 2, SSD_HEADS, SSD_HD, SSD_STATE), jnp.float32)
        else:
            q, k = apply_axial_rope(q), apply_axial_rope(k)
            o_att = attn_latent(q, k, v, cache_k[:, l], cache_v[:, l], sink)
            rw0, ssd0 = state_rwkv[:, l], state_ssd[:, l]
        o_rw, s_rw = rwkv_mixer(u_rw, rw0, P['rw_mu'][l], P['rw_w0'][l], P['rw_w_up'][l], P['rw_a0'][l],
                                P['rw_a_up'][l], P['rw_g_up'][l], P['rw_k_k'][l], P['rw_k_a'][l],
                                P['rw_r_k'][l], P['rw_ln_w'][l], P['rw_ln_b'][l])
        o_ssd, s_ssd = ssd_mixer(u_ssd, ssd0, P['ssd_conv_w'][l], P['ssd_conv_b'][l], P['ssd_dt_bias'][l],
                                 P['ssd_a_log'][l], P['ssd_d'][l], P['ssd_norm_w'][l])
        if is_ctx:
            new_k.append(k)
            new_v.append(v)
            new_rw.append(s_rw.astype(x.dtype))
            new_ssd.append(s_ssd.astype(x.dtype))
        o = jnp.concatenate([o_att.reshape(b, L, ATT_W), o_rw.astype(x.dtype), o_ssd.astype(x.dtype)], -1)
        x = x + g1 * (o @ P['w_out'][l])
        h = rmsnorm(x, P['norm2_w'][l]) * (1 + sc2) + sh2
        x = x + g2 * moe(h, P['router_w'], P['router_bias'], P['exp_gate'][l], P['exp_up'][l], P['exp_down'][l])
    if is_ctx:
        return x, (jnp.stack(new_k, 1), jnp.stack(new_v, 1), jnp.stack(new_rw, 1), jnp.stack(new_ssd, 1))
    return x, None


def setup_inputs(seed: int = 0) -> dict:
    key = jax.random.key(seed)
    keys = jax.random.split(key, 64)
    ks = (keys[i] for i in range(64))
    nrm = lambda shape, scale: jax.random.normal(next(ks), shape, jnp.float32) * scale
    uni = lambda shape, lo, hi: jax.random.uniform(next(ks), shape, jnp.float32, minval=lo, maxval=hi)
    L = DEPTH
    D = D_MODEL
    dt0 = jnp.exp(uni((L, 2, SSD_HEADS), math.log(1e-3), math.log(1e-1)))
    return {
        'x_prompt': nrm((BATCH, SEQ, D), 1.0),
        'x_sample': nrm((DEC_BATCH, DEC_SEQ, D), 1.0),
        'cache_k': nrm((DEC_BATCH, DEPTH, PAST_LEN, ATT_KV_HEADS, HEAD_DIM), 1.0),
        'cache_v': nrm((DEC_BATCH, DEPTH, PAST_LEN, ATT_KV_HEADS, HEAD_DIM), 1.0),
        'state_rwkv': nrm((DEC_BATCH, DEPTH, 2, RW_HEADS, RW_HD, RW_HD), 0.5),
        'state_ssd': nrm((DEC_BATCH, DEPTH, 2, SSD_HEADS, SSD_HD, SSD_STATE), 0.3),
        'c': nrm((DEC_BATCH, D), 1.0),
        'c_ctx': nrm((D,), 1.0),
        'w_mod': nrm((L, D, 6 * D), 0.5 * D ** -0.5),
        'b_mod': nrm((L, 6 * D), 0.02),
        'norm1_w': 1.0 + nrm((L, D), 0.05),
        'norm2_w': 1.0 + nrm((L, D), 0.05),
        'w_in': nrm((L, D, IN_COLS), D ** -0.5),
        'w_out': nrm((L, MIX_W, D), MIX_W ** -0.5),
        'q_norm_w': 1.0 + nrm((L, HEAD_DIM), 0.05),
        'k_norm_w': 1.0 + nrm((L, HEAD_DIM), 0.05),
        'attn_sink': nrm((L, ATT_HEADS), 0.5),
        'rw_mu': uni((L, 2, RW_COLS), 0.0, 0.5),
        'rw_w0': nrm((L, 2, RW_W), 0.5),
        'rw_w_up': nrm((L, 2, RW_W_RANK, RW_W), 0.1),
        'rw_a0': nrm((L, 2, RW_W), 0.1),
        'rw_a_up': nrm((L, 2, RW_A_RANK, RW_W), 0.1),
        'rw_g_up': nrm((L, RW_G_RANK, RW_W), RW_G_RANK ** -0.5),
        'rw_k_k': 0.85 + nrm((L, RW_W), 0.05),
        'rw_k_a': 1.0 + nrm((L, RW_W), 0.05),
        'rw_r_k': nrm((L, RW_HEADS, RW_HD), 0.1),
        'rw_ln_w': 1.0 + nrm((L, RW_W), 0.05),
        'rw_ln_b': nrm((L, RW_W), 0.02),
        'ssd_conv_w': nrm((L, SSD_CONV_K, SSD_CONV_DIM), SSD_CONV_K ** -0.5),
        'ssd_conv_b': nrm((L, SSD_CONV_DIM), 0.02),
        'ssd_dt_bias': dt0 + jnp.log(-jnp.expm1(-dt0)),
        'ssd_a_log': jnp.log(uni((L, 2, SSD_HEADS), 1.0, 16.0)),
        'ssd_d': 1.0 + nrm((L, SSD_HEADS), 0.1),
        'ssd_norm_w': 1.0 + nrm((L, SSD_W), 0.05),
        'router_w': nrm((D, N_EXPERTS), D ** -0.5),
        'router_bias': nrm((N_EXPERTS,), 0.01),
        'exp_gate': nrm((L, N_EXPERTS, D, D_FF), D ** -0.5),
        'exp_up': nrm((L, N_EXPERTS, D, D_FF), D ** -0.5),
        'exp_down': nrm((L, N_EXPERTS, D_FF, D), D_FF ** -0.5),
    }


def reference(x_prompt, x_sample, cache_k, cache_v, state_rwkv, state_ssd, c, c_ctx,
              w_mod, b_mod, norm1_w, norm2_w, w_in, w_out, q_norm_w, k_norm_w, attn_sink,
              rw_mu, rw_w0, rw_w_up, rw_a0, rw_a_up, rw_g_up, rw_k_k, rw_k_a, rw_r_k, rw_ln_w, rw_ln_b,
              ssd_conv_w, ssd_conv_b, ssd_dt_bias, ssd_a_log, ssd_d, ssd_norm_w,
              router_w, router_bias, exp_gate, exp_up, exp_down):
    P = {
        'w_mod': w_mod, 'b_mod': b_mod, 'norm1_w': norm1_w, 'norm2_w': norm2_w,
        'w_in': w_in, 'w_out': w_out, 'q_norm_w': q_norm_w, 'k_norm_w': k_norm_w, 'attn_sink': attn_sink,
        'rw_mu': rw_mu, 'rw_w0': rw_w0, 'rw_w_up': rw_w_up, 'rw_a0': rw_a0, 'rw_a_up': rw_a_up,
        'rw_g_up': rw_g_up, 'rw_k_k': rw_k_k, 'rw_k_a': rw_k_a, 'rw_r_k': rw_r_k,
        'rw_ln_w': rw_ln_w, 'rw_ln_b': rw_ln_b,
        'ssd_conv_w': ssd_conv_w, 'ssd_conv_b': ssd_conv_b, 'ssd_dt_bias': ssd_dt_bias,
        'ssd_a_log': ssd_a_log, 'ssd_d': ssd_d, 'ssd_norm_w': ssd_norm_w,
        'router_w': router_w, 'router_bias': router_bias,
        'exp_gate': exp_gate, 'exp_up': exp_up, 'exp_down': exp_down,
    }
    y_prompt, ctx_state = trunk(x_prompt, c_ctx, True, None, None, None, None, P)
    new_cache_k, new_cache_v, new_state_rwkv, new_state_ssd = ctx_state
    y_sample, _ = trunk(x_sample, c, False, cache_k, cache_v, state_rwkv, state_ssd, P)
    return (y_prompt, y_sample, new_cache_k, new_cache_v, new_state_rwkv, new_state_ssd)
```

```python
import functools
import math

import jax
import jax.numpy as jnp
import numpy as np
from jax import lax
from jax.experimental import pallas as pl
from jax.experimental.pallas import tpu as pltpu

F32 = jnp.float32
BF16 = jnp.bfloat16
HIGHEST = lax.Precision.HIGHEST

D_MODEL = 1024
DEPTH = 2
GRID_W = 64
EPS = 1e-6
ATT_HEADS = 8
ATT_KV_HEADS = 2
HEAD_DIM = 64
ATT_W = 512
KV_W = 128
ATT_COLS = 768
WINDOW = 128
BLK = 128
ROPE_BASE = 10000.0
RW_HEADS = 4
RW_HD = 64
RW_W = 256
RW_COLS = 896
RW_DECAY_SCALE = 0.606531
RW_GN_EPS = 64e-5
SSD_W = 256
SSD_CHUNK = 128
SSD_COLS = 772
SSD_PAD = 896
IN_PAD = ATT_COLS + RW_COLS + SSD_PAD
N_EXPERTS = 16
EXP_PER_GROUP = 4
D_FF = 256

LANES = 128
SUBLANES = 8
TOKEN_TILE = 256
MOE_TILE = 1024
SCAN_CHUNK = 16
SCAN_BATCH = 4
VMEM_LIMIT = 48 * 1024 * 1024


def _params(sem):
    return pltpu.CompilerParams(dimension_semantics=sem, vmem_limit_bytes=VMEM_LIMIT)


def _bdot(a, b):
    return jnp.dot(a.astype(BF16), b.astype(BF16), preferred_element_type=F32)


def _bdot_nt(a, b):
    return lax.dot_general(a.astype(BF16), b.astype(BF16), (((1,), (1,)), ((), ())),
                           preferred_element_type=F32)


def _xdot(a, b):
    return jnp.dot(a, b, precision=HIGHEST, preferred_element_type=F32)


def _iota(shape, axis):
    return lax.broadcasted_iota(jnp.int32, shape, axis)


def _group_ones(n, log2_blk):
    return ((_iota((n, n), 0) >> log2_blk) == (_iota((n, n), 1) >> log2_blk)).astype(F32)


def _sigmoid(x):
    return 1.0 / (1.0 + jnp.exp(-x))


def _silu(x):
    return x * _sigmoid(x)


def _rms(x, w):
    return x * lax.rsqrt(jnp.mean(x * x, -1, keepdims=True) + EPS) * w


def _mod_kernel(c_ref, w_ref, b_ref, o_ref):
    o_ref[...] = _bdot(_silu(c_ref[...]), w_ref[...]) + b_ref[...]


def _modulation(cond8, w_mod, b_mod):
    out = pl.pallas_call(
        _mod_kernel,
        grid=(DEPTH, 6),
        in_specs=[
            pl.BlockSpec((SUBLANES, D_MODEL), lambda l, j: (0, 0)),
            pl.BlockSpec((None, D_MODEL, D_MODEL), lambda l, j: (l, 0, j)),
            pl.BlockSpec((None, 1, D_MODEL), lambda l, j: (l, 0, j)),
        ],
        out_specs=pl.BlockSpec((None, None, SUBLANES, D_MODEL), lambda l, j: (l, j, 0, 0)),
        out_shape=jax.ShapeDtypeStruct((DEPTH, 6, SUBLANES, D_MODEL), F32),
        compiler_params=_params(("arbitrary", "arbitrary")),
        name="modulation",
    )(cond8, w_mod, b_mod.reshape(DEPTH, 1, 6 * D_MODEL))
    return out.reshape(DEPTH, 6, SUBLANES, 1, D_MODEL)


def _mod_spec(row_fn):
    return pl.BlockSpec((6, None, 1, D_MODEL), lambda *idx: (0, row_fn(idx[0]), 0, 0))


def _swap16(x):
    w = x.shape[1]
    first = (_iota(x.shape, 1) & 31) < 16
    return jnp.where(first, pltpu.roll(x, w - 16, 1), pltpu.roll(x, 16, 1))


def _inproj_kernel(use_rope, x_ref, mod_ref, n1_ref, w_ref, qw_ref, kw_ref, *rest):
    if use_rope:
        cos_ref, sin_ref, q_ref, k_ref, v_ref, urw_ref, ussd_ref = rest
    else:
        q_ref, k_ref, v_ref, urw_ref, ussd_ref = rest
    h = _rms(x_ref[...], n1_ref[...]) * (1.0 + mod_ref[1]) + mod_ref[0]
    u = _bdot(h, w_ref[...])
    q = u[:, :ATT_W]
    k = u[:, ATT_W:ATT_W + KV_W]
    ones_q = _group_ones(ATT_W, 6)
    ones_k = _group_ones(KV_W, 6)
    q = q * lax.rsqrt(_xdot(q * q, ones_q) * (1.0 / HEAD_DIM) + EPS) * qw_ref[...]
    k = k * lax.rsqrt(_xdot(k * k, ones_k) * (1.0 / HEAD_DIM) + EPS) * kw_ref[...]
    if use_rope:
        cos = cos_ref[...]
        sin = sin_ref[...]
        k = k * cos + _swap16(k) * sin
        cos4 = jnp.concatenate([cos] * 4, axis=1)
        sin4 = jnp.concatenate([sin] * 4, axis=1)
        q = q * cos4 + _swap16(q) * sin4
    q_ref[...] = q
    k_ref[...] = k
    v_ref[...] = u[:, ATT_W + KV_W:ATT_COLS]
    urw_ref[...] = u[:, ATT_COLS:ATT_COLS + RW_COLS]
    ussd_ref[...] = u[:, ATT_COLS + RW_COLS:]


def _inproj(x, mod_l, row_fn, n1, w_in_bf, qw, kw, rope, seq_len):
    n_tok = x.shape[0]
    tiles_per_seq = seq_len // TOKEN_TILE
    in_specs = [
        pl.BlockSpec((TOKEN_TILE, D_MODEL), lambda i: (i, 0)),
        _mod_spec(row_fn),
        pl.BlockSpec((1, D_MODEL), lambda i: (0, 0)),
        pl.BlockSpec((D_MODEL, IN_PAD), lambda i: (0, 0)),
        pl.BlockSpec((1, ATT_W), lambda i: (0, 0)),
        pl.BlockSpec((1, KV_W), lambda i: (0, 0)),
    ]
    args = [x, mod_l, n1, w_in_bf, qw, kw]
    if rope is not None:
        in_specs += [pl.BlockSpec((TOKEN_TILE, LANES), lambda i: (i % tiles_per_seq, 0))] * 2
        args += list(rope)
    widths = (ATT_W, KV_W, KV_W, RW_COLS, SSD_PAD)
    return pl.pallas_call(
        functools.partial(_inproj_kernel, rope is not None),
        grid=(n_tok // TOKEN_TILE,),
        in_specs=in_specs,
        out_specs=[pl.BlockSpec((TOKEN_TILE, w), lambda i: (i, 0)) for w in widths],
        out_shape=[jax.ShapeDtypeStruct((n_tok, w), F32) for w in widths],
        compiler_params=_params(("parallel",)),
        name="inproj",
    )(*args)


def _softmax_pv(s, sink, v_bf):
    m = jnp.maximum(jnp.max(s, -1, keepdims=True), sink)
    p = jnp.exp(s - m)
    den = jnp.sum(p, -1, keepdims=True) + jnp.exp(sink - m)
    return jnp.dot(p.astype(BF16), v_bf, preferred_element_type=F32) / den


def _attend_heads(sink_ref, q, k, v, mask):
    outs = []
    for hk in range(ATT_KV_HEADS):
        kh = k[:, hk * HEAD_DIM:(hk + 1) * HEAD_DIM].astype(BF16)
        vh = v[:, hk * HEAD_DIM:(hk + 1) * HEAD_DIM].astype(BF16)
        for g in range(ATT_HEADS // ATT_KV_HEADS):
            hd = hk * (ATT_HEADS // ATT_KV_HEADS) + g
            qh = q[:, hd * HEAD_DIM:(hd + 1) * HEAD_DIM]
            s = _bdot_nt(qh, kh) * (HEAD_DIM ** -0.5)
            if mask is not None:
                s = jnp.where(mask, s, -1e30)
            outs.append(_softmax_pv(s, sink_ref[hd], vh))
    return jnp.concatenate(outs, axis=1)


def _attn_ctx_kernel(sink_ref, q_ref, k_ref, v_ref, o_ref):
    o_ref[...] = _attend_heads(sink_ref, q_ref[...], k_ref[...], v_ref[...], None)


def _attn_ctx(sink, q, k, v, nb, seq_len):
    return pl.pallas_call(
        _attn_ctx_kernel,
        grid=(nb,),
        in_specs=[
            pl.BlockSpec(memory_space=pltpu.SMEM),
            pl.BlockSpec((seq_len, ATT_W), lambda b: (b, 0)),
            pl.BlockSpec((seq_len, KV_W), lambda b: (b, 0)),
            pl.BlockSpec((seq_len, KV_W), lambda b: (b, 0)),
        ],
        out_specs=pl.BlockSpec((seq_len, ATT_W), lambda b: (b, 0)),
        out_shape=jax.ShapeDtypeStruct((nb * seq_len, ATT_W), F32),
        compiler_params=_params(("parallel",)),
        name="attn_ctx",
    )(sink, q, k, v)


def _attn_lat_kernel(seq_len, sink_ref, q_ref, k_ref, v_ref, kc_ref, vc_ref, o_ref):
    i = pl.program_id(1)
    band = 3 * BLK
    start = pl.multiple_of(jnp.clip((i - 1) * BLK, 0, seq_len - band), BLK)
    k = jnp.concatenate([k_ref[pl.ds(start, band), :], kc_ref[...]], axis=0)
    v = jnp.concatenate([v_ref[pl.ds(start, band), :], vc_ref[...]], axis=0)
    n_keys = k.shape[0]
    col = _iota((BLK, n_keys), 1)
    dist = jnp.abs(i * BLK + _iota((BLK, n_keys), 0) - (start + col))
    mask = jnp.where(col >= band, 0, dist) <= WINDOW
    o_ref[...] = _attend_heads(sink_ref, q_ref[...], k, v, mask)


def _attn_lat(sink, q, k, v, kc, vc, nb, seq_len):
    nblk = seq_len // BLK
    past = kc.shape[1]
    return pl.pallas_call(
        functools.partial(_attn_lat_kernel, seq_len),
        grid=(nb, nblk),
        in_specs=[
            pl.BlockSpec(memory_space=pltpu.SMEM),
            pl.BlockSpec((BLK, ATT_W), lambda b, i: (b * nblk + i, 0)),
            pl.BlockSpec((seq_len, KV_W), lambda b, i: (b, 0)),
            pl.BlockSpec((seq_len, KV_W), lambda b, i: (b, 0)),
            pl.BlockSpec((None, past, KV_W), lambda b, i: (b, 0, 0)),
            pl.BlockSpec((None, past, KV_W), lambda b, i: (b, 0, 0)),
        ],
        out_specs=pl.BlockSpec((BLK, ATT_W), lambda b, i: (b * nblk + i, 0)),
        out_shape=jax.ShapeDtypeStruct((nb * seq_len, ATT_W), F32),
        compiler_params=_params(("parallel", "arbitrary")),
        name="attn_lat",
    )(sink, q, k, v, kc, vc)


def _rw_prep_kernel(tiles_per_seq, u_ref, up_ref, un_ref, mu_ref, wg_ref, ww_ref, wa_ref,
                    w0_ref, a0_ref, kk_ref, ka_ref, rk_ref,
                    opf_ref, opb_ref, v_ref, g_ref, gb_ref):
    i = pl.program_id(0)
    t = i % tiles_per_seq
    has_prev = (t > 0).astype(F32)
    has_next = (t < tiles_per_seq - 1).astype(F32)
    u = u_ref[...]
    row = _iota(u.shape, 0)
    prev = jnp.where(row == 0, up_ref[SUBLANES - 1:SUBLANES, :] * has_prev, pltpu.roll(u, 1, 0))
    nxt = jnp.where(row == TOKEN_TILE - 1, un_ref[0:1, :] * has_next, pltpu.roll(u, TOKEN_TILE - 1, 0))
    u = u + mu_ref[0:1, :] * (prev - u) + mu_ref[1:2, :] * (nxt - u)
    r = u[:, :RW_W]
    k = u[:, RW_W:2 * RW_W]
    v = u[:, 2 * RW_W:3 * RW_W]
    low = u[:, 3 * RW_W:]
    ones = _group_ones(RW_W, 6)
    g = _bdot(_sigmoid(low), wg_ref[...])
    kk = k * kk_ref[...]
    kk = kk * lax.rsqrt(_xdot(kk * kk, ones) + 1e-12)
    tanh_low = jnp.tanh(low)
    for d, o_ref in enumerate((opf_ref, opb_ref)):
        w = jnp.exp(-RW_DECAY_SCALE * _sigmoid(w0_ref[d:d + 1, :] + _bdot(tanh_low, ww_ref[d])))
        a = _sigmoid(a0_ref[d:d + 1, :] + _bdot(low, wa_ref[d]))
        o_ref[:, 0 * RW_W:1 * RW_W] = r
        o_ref[:, 1 * RW_W:2 * RW_W] = kk
        o_ref[:, 2 * RW_W:3 * RW_W] = w
        o_ref[:, 3 * RW_W:4 * RW_W] = k * (1.0 + (a - 1.0) * ka_ref[...])
        o_ref[:, 4 * RW_W:5 * RW_W] = kk * a
    bonus = _xdot(r * k * rk_ref[...], ones) * v
    v_ref[...] = v
    g_ref[...] = g
    gb_ref[...] = bonus * g


def _rw_prep(u_rw, seq_len, mu, wg, ww, wa, w0, a0, k_k, k_a, r_k):
    n_tok = u_rw.shape[0]
    tiles_per_seq = seq_len // TOKEN_TILE
    sub_per_tile = TOKEN_TILE // SUBLANES
    last_sub = n_tok // SUBLANES - 1
    full = lambda a: pl.BlockSpec(a.shape, lambda i: (0,) * a.ndim)
    widths = (5 * RW_W, 5 * RW_W, RW_W, RW_W, RW_W)
    return pl.pallas_call(
        functools.partial(_rw_prep_kernel, tiles_per_seq),
        grid=(n_tok // TOKEN_TILE,),
        in_specs=[
            pl.BlockSpec((TOKEN_TILE, RW_COLS), lambda i: (i, 0)),
            pl.BlockSpec((SUBLANES, RW_COLS), lambda i: (jnp.maximum(i * sub_per_tile - 1, 0), 0)),
            pl.BlockSpec((SUBLANES, RW_COLS), lambda i: (jnp.minimum((i + 1) * sub_per_tile, last_sub), 0)),
        ] + [full(a) for a in (mu, wg, ww, wa, w0, a0, k_k, k_a, r_k)],
        out_specs=[pl.BlockSpec((TOKEN_TILE, w), lambda i: (i, 0)) for w in widths],
        out_shape=[jax.ShapeDtypeStruct((n_tok, w), F32) for w in widths],
        compiler_params=_params(("parallel",)),
        name="rw_prep",
    )(u_rw, u_rw, u_rw, mu, wg, ww, wa, w0, a0, k_k, k_a, r_k)


def _allsum_sublanes(x):
    x = x + pltpu.roll(x, 4, 0)
    x = x + pltpu.roll(x, 2, 0)
    return x + pltpu.roll(x, 1, 0)


def _rw_scan_kernel(n_chunks, opf_ref, opb_ref, vf_ref, vb_ref, s0_ref, yf_ref, yb_ref, sfin_ref, s_scr):
    c = pl.program_id(1)

    @pl.when(c == 0)
    def _():
        s_scr[...] = s0_ref[...]

    sub = _iota((SUBLANES, LANES), 0)
    nkb = RW_HD // SUBLANES

    def step(tt, carry):
        for d, (op_ref, v_ref, y_ref, tl) in enumerate(
                ((opf_ref, vf_ref, yf_ref, tt), (opb_ref, vb_ref, yb_ref, SCAN_CHUNK - 1 - tt))):
            load = lambda j: [op_ref[j, tl, pl.ds(kb * SUBLANES, SUBLANES), :] for kb in range(nkb)]
            r, kk, w, kd, bb = load(0), load(1), load(2), load(3), load(4)
            y = jnp.zeros((SUBLANES, LANES), F32)
            for vi in range(SUBLANES):
                s = [s_scr[d, kb, vi] for kb in range(nkb)]
                sa = s[0] * kk[0]
                for kb in range(1, nkb):
                    sa = sa + s[kb] * kk[kb]
                sa = _allsum_sublanes(sa)
                vb = jnp.broadcast_to(v_ref[tl, pl.ds(vi, 1), :], (SUBLANES, LANES))
                acc = None
                for kb in range(nkb):
                    sn = s[kb] * w[kb] - sa * bb[kb] + vb * kd[kb]
                    s_scr[d, kb, vi] = sn
                    acc = sn * r[kb] if acc is None else acc + sn * r[kb]
                y = jnp.where(sub == vi, _allsum_sublanes(acc), y)
            y_ref[tl] = y
        return carry

    lax.fori_loop(0, SCAN_CHUNK, step, 0)

    @pl.when(c == n_chunks - 1)
    def _():
        sfin_ref[...] = s_scr[...]


def _rw_scan(opf, opb, vv, s0):
    n_grp, _, seq_len = opf.shape[:3]
    nc = seq_len // SCAN_CHUNK
    op_shape = (None, 5, SCAN_CHUNK, RW_HD, LANES)
    v_shape = (None, SCAN_CHUNK, SUBLANES, LANES)
    s_shape = (None, 2, SUBLANES, SUBLANES, SUBLANES, LANES)
    fwd5 = lambda g, c: (g, 0, c, 0, 0)
    bwd5 = lambda g, c: (g, 0, nc - 1 - c, 0, 0)
    fwd4 = lambda g, c: (g, c, 0, 0)
    bwd4 = lambda g, c: (g, nc - 1 - c, 0, 0)
    s_map = lambda g, c: (g, 0, 0, 0, 0, 0)
    y_sds = jax.ShapeDtypeStruct((n_grp, seq_len, SUBLANES, LANES), F32)
    return pl.pallas_call(
        functools.partial(_rw_scan_kernel, nc),
        grid=(n_grp, nc),
        in_specs=[
            pl.BlockSpec(op_shape, fwd5), pl.BlockSpec(op_shape, bwd5),
            pl.BlockSpec(v_shape, fwd4), pl.BlockSpec(v_shape, bwd4),
            pl.BlockSpec(s_shape, s_map),
        ],
        out_specs=[pl.BlockSpec(v_shape, fwd4), pl.BlockSpec(v_shape, bwd4), pl.BlockSpec(s_shape, s_map)],
        out_shape=[y_sds, y_sds, jax.ShapeDtypeStruct(s0.shape, F32)],
        scratch_shapes=[pltpu.VMEM((2, SUBLANES, SUBLANES, SUBLANES, LANES), F32)],
        compiler_params=_params(("parallel", "arbitrary")),
        name="rw_scan",
    )(opf, opb, vv, vv, s0)


def _rw_mixer(u_rw, nb, seq_len, s0, P):
    opf, opb, v, g, gb = _rw_prep(u_rw, seq_len, *P)
    ng = nb // SCAN_BATCH
    vg = LANES // (SCAN_BATCH * RW_HEADS)
    vi = RW_HD // vg

    def to_scan(op):
        op = op.reshape(ng, SCAN_BATCH, seq_len, 5, RW_HEADS, RW_HD)
        op = jnp.transpose(op, (0, 3, 2, 5, 1, 4)).reshape(ng, 5, seq_len, RW_HD, 1, SCAN_BATCH * RW_HEADS)
        op = jnp.broadcast_to(op, (ng, 5, seq_len, RW_HD, vg, SCAN_BATCH * RW_HEADS))
        return op.reshape(ng, 5, seq_len, RW_HD, LANES)

    vv = v.reshape(ng, SCAN_BATCH, seq_len, RW_HEADS, vg, vi)
    vv = jnp.transpose(vv, (0, 2, 5, 4, 1, 3)).reshape(ng, seq_len, vi, LANES)
    kb = RW_HD // SUBLANES
    s0 = s0.reshape(ng, SCAN_BATCH, 2, RW_HEADS, vg, vi, kb, SUBLANES)
    s0 = jnp.transpose(s0, (0, 2, 6, 5, 7, 4, 1, 3)).reshape(ng, 2, kb, vi, SUBLANES, LANES)
    yf, yb, sfin = _rw_scan(to_scan(opf), to_scan(opb), vv, s0)

    def to_tokens(y):
        y = y.reshape(ng, seq_len, vi, vg, SCAN_BATCH, RW_HEADS)
        return jnp.transpose(y, (0, 4, 1, 5, 3, 2)).reshape(nb * seq_len, RW_W)

    sfin = sfin.reshape(ng, 2, kb, vi, SUBLANES, vg, SCAN_BATCH, RW_HEADS)
    sfin = jnp.transpose(sfin, (0, 6, 1, 7, 5, 3, 2, 4)).reshape(nb, 2, RW_HEADS, RW_HD, RW_HD)
    return to_tokens(yf), to_tokens(yb), g, gb, sfin


def _softplus(x):
    return jnp.maximum(x, 0.0) + jnp.log1p(jnp.exp(-jnp.abs(x)))


def _ssd_kernel(reverse, n_chunks, u_ref, up_ref, un_ref, h0_ref, cw_ref, cb_ref, dtb_ref, a_ref, *rest):
    if reverse:
        yf_ref, nw_ref, out_ref, hfin_ref, h_scr = rest
    else:
        dsk_ref, out_ref, hfin_ref, h_scr = rest
    s = pl.program_id(1)
    c = (n_chunks - 1 - s) if reverse else s

    @pl.when(s == 0)
    def _():
        h_scr[...] = h0_ref[...]

    q = SSD_CHUNK
    u = u_ref[...]
    xbc = u[:, SSD_W:SSD_W + 512]
    row = _iota(xbc.shape, 0)
    has_prev = (c > 0).astype(F32)
    has_next = (c < n_chunks - 1).astype(F32)
    prev = jnp.where(row == 0, up_ref[SUBLANES - 1:SUBLANES, SSD_W:SSD_W + 512] * has_prev, pltpu.roll(xbc, 1, 0))
    nxt = jnp.where(row == q - 1, un_ref[0:1, SSD_W:SSD_W + 512] * has_next, pltpu.roll(xbc, q - 1, 0))
    xc = _silu(cw_ref[0:1, :] * prev + cw_ref[1:2, :] * xbc + cw_ref[2:3, :] * nxt + cb_ref[...])
    x = xc[:, :SSD_W]
    b_all = xc[:, SSD_W:SSD_W + LANES]
    c_all = xc[:, SSD_W + LANES:]

    dt = _softplus(u[:, SSD_W + 512:] + dtb_ref[...])
    dta = dt * a_ref[...]
    ii = _iota((q, q), 0)
    jj = _iota((q, q), 1)
    tri = (jj >= ii) if reverse else (jj <= ii)
    cs = _xdot(tri.astype(F32), dta)
    cs_t = cs.T
    expand = ((_iota((LANES, SSD_W), 1) >> 6) == _iota((LANES, SSD_W), 0)).astype(F32)
    dt_e = _xdot(dt, expand)
    cs_e = _xdot(cs, expand)
    tot = cs_e[0:1, :] if reverse else cs_e[q - 1:q, :]
    xdt = x * dt_e
    dec_in = jnp.exp(tot - cs_e)
    dec_out = jnp.exp(cs_e)
    cdec = jnp.exp(tot)
    lane = _iota((q, LANES), 1)
    rowc = _iota((LANES, 1), 0)
    ys = []
    for g in range(2):
        bg = b_all[:, g * 64:(g + 1) * 64]
        cg = c_all[:, g * 64:(g + 1) * 64]
        gm = _bdot_nt(cg, bg)
        sl = slice(g * LANES, (g + 1) * LANES)
        xdt_g = xdt[:, sl]
        yd = []
        for hh in range(2):
            h = 2 * g + hh
            diff = cs[:, h:h + 1] - cs_t[h:h + 1, :]
            lm = jnp.exp(jnp.where(tri, diff, -jnp.inf))
            yd.append(_bdot(gm * lm, xdt_g))
        hg = h_scr[g]
        y = jnp.where(lane < 64, yd[0], yd[1]) + _bdot_nt(cg, hg) * dec_out[:, sl]
        st = _bdot((xdt_g * dec_in[:, sl]).T, bg)
        cd = cdec[:, sl]
        h_new = hg * jnp.where(rowc < 64, cd[:, 0:1], cd[:, 64:65]) + st
        h_scr[g] = h_new
        hfin_ref[g] = h_new
        ys.append(y)
    y = jnp.concatenate(ys, axis=1)
    if reverse:
        val = (yf_ref[...] + y) * _silu(u[:, :SSD_W])
        out_ref[...] = _rms(val, nw_ref[...])
    else:
        out_ref[...] = y + dsk_ref[...] * x


def _ssd_sweep(reverse, u_ssd, nb, seq_len, h0, cw, cb, dtb, a_neg, extra):
    nc = seq_len // SSD_CHUNK
    sub_per_chunk = SSD_CHUNK // SUBLANES
    last_sub = u_ssd.shape[0] // SUBLANES - 1
    chunk = (lambda s: nc - 1 - s) if reverse else (lambda s: s)
    rowblk = lambda b, s: b * nc + chunk(s)
    full = lambda a: pl.BlockSpec(a.shape, lambda b, s: (0,) * a.ndim)
    in_specs = [
        pl.BlockSpec((SSD_CHUNK, SSD_PAD), lambda b, s: (rowblk(b, s), 0)),
        pl.BlockSpec((SUBLANES, SSD_PAD), lambda b, s: (jnp.maximum(rowblk(b, s) * sub_per_chunk - 1, 0), 0)),
        pl.BlockSpec((SUBLANES, SSD_PAD),
                     lambda b, s: (jnp.minimum((rowblk(b, s) + 1) * sub_per_chunk, last_sub), 0)),
        pl.BlockSpec((None, 2, LANES, 64), lambda b, s: (b, 0, 0, 0)),
        full(cw), full(cb), full(dtb), full(a_neg),
    ]
    args = [u_ssd, u_ssd, u_ssd, h0, cw, cb, dtb, a_neg]
    if reverse:
        yf, nw = extra
        in_specs += [pl.BlockSpec((SSD_CHUNK, SSD_W), lambda b, s: (rowblk(b, s), 0)), full(nw)]
        args += [yf, nw]
    else:
        in_specs += [full(extra)]
        args += [extra]
    return pl.pallas_call(
        functools.partial(_ssd_kernel, reverse, nc),
        grid=(nb, nc),
        in_specs=in_specs,
        out_specs=[pl.BlockSpec((SSD_CHUNK, SSD_W), lambda b, s: (rowblk(b, s), 0)),
                   pl.BlockSpec((None, 2, LANES, 64), lambda b, s: (b, 0, 0, 0))],
        out_shape=[jax.ShapeDtypeStruct((nb * seq_len, SSD_W), F32),
                   jax.ShapeDtypeStruct((nb, 2, LANES, 64), F32)],
        scratch_shapes=[pltpu.VMEM((2, LANES, 64), F32)],
        compiler_params=_params(("parallel", "arbitrary")),
        name="ssd_bwd" if reverse else "ssd_fwd",
    )(*args)


def _ssd_mixer(u_ssd, nb, seq_len, h0, cw, cb, dtb, a_neg, dsk, nw):
    yf, hf = _ssd_sweep(False, u_ssd, nb, seq_len, h0[:, 0], cw, cb, dtb[0], a_neg[0], dsk)
    out, hb = _ssd_sweep(True, u_ssd, nb, seq_len, h0[:, 1], cw, cb, dtb[1], a_neg[1], (yf, nw))
    return out, jnp.stack([hf, hb], axis=1).reshape(nb, 2, 4, 64, 64)


def _group_peers(x, width, log2_stride):
    stride = 1 << log2_stride
    pos = (_iota(x.shape, 1) >> log2_stride) & 3
    peers = []
    for j in (1, 2, 3):
        ahead = pltpu.roll(x, width - j * stride, 1)
        behind = pltpu.roll(x, (4 - j) * stride, 1)
        peers.append((jnp.where(pos + j < 4, ahead, behind), pos + j >= 4))
    return peers


def _route(scores, bias):
    sel = scores + bias
    rank = jnp.zeros(sel.shape, jnp.int32)
    as_int = lambda cond: jnp.where(cond, 1, 0)
    for other, wrapped in _group_peers(sel, LANES, 0):
        rank = rank + jnp.where(wrapped, as_int(other >= sel), as_int(other > sel))
    top2 = rank < 2
    m = jnp.where(top2, sel, 0.0)
    grp = m
    for other, _ in _group_peers(m, LANES, 0):
        grp = grp + other
    best = jnp.ones(sel.shape, jnp.int32)
    for other, wrapped in _group_peers(grp, LANES, 2):
        loses = jnp.where(wrapped, as_int(other >= grp), as_int(other > grp))
        best = best * (1 - loses)
    chosen = jnp.where(top2, best, 0) > 0
    cw = jnp.where(chosen, scores, 0.0)
    den = cw
    for other, _ in _group_peers(cw, LANES, 0):
        den = den + other
    return jnp.where(chosen, cw / jnp.where(chosen, den, 1.0), 0.0)


def _outproj_kernel(oatt_ref, yf_ref, yb_ref, g_ref, gb_ref, ossd_ref, x_ref, mod_ref, wout_ref,
                    lnw_ref, lnb_ref, n2_ref, rw_ref, rb_ref, x1_ref, h2_ref, comb_ref):
    ones = _group_ones(RW_W, 6)
    y = yf_ref[...] + yb_ref[...]
    dev = y - _xdot(y, ones) * (1.0 / RW_HD)
    yn = dev * lax.rsqrt(_xdot(dev * dev, ones) * (1.0 / RW_HD) + RW_GN_EPS)
    o_rw = (yn * lnw_ref[...] + lnb_ref[...]) * g_ref[...] + gb_ref[...]
    o = jnp.concatenate([oatt_ref[...], o_rw, ossd_ref[...]], axis=1)
    x1 = x_ref[...] + mod_ref[2] * _bdot(o, wout_ref[...])
    h2 = _rms(x1, n2_ref[...]) * (1.0 + mod_ref[4]) + mod_ref[3]
    scores = _sigmoid(_bdot(h2, rw_ref[...]))
    x1_ref[...] = x1
    h2_ref[...] = h2.astype(BF16)
    comb_ref[...] = _route(scores, rb_ref[...])


def _outproj(o_att, yf, yb, g, gb, o_ssd, x, mod_l, row_fn, w_out_bf, lnw, lnb, n2, router_w, router_b):
    n_tok = x.shape[0]
    tile = lambda w: pl.BlockSpec((TOKEN_TILE, w), lambda i: (i, 0))
    full = lambda a: pl.BlockSpec(a.shape, lambda i: (0,) * a.ndim)
    return pl.pallas_call(
        _outproj_kernel,
        grid=(n_tok // TOKEN_TILE,),
        in_specs=[tile(ATT_W), tile(RW_W), tile(RW_W), tile(RW_W), tile(RW_W), tile(SSD_W), tile(D_MODEL),
                  _mod_spec(row_fn)] + [full(a) for a in (w_out_bf, lnw, lnb, n2, router_w, router_b)],
        out_specs=[tile(D_MODEL), tile(D_MODEL), tile(LANES)],
        out_shape=[jax.ShapeDtypeStruct((n_tok, D_MODEL), F32),
                   jax.ShapeDtypeStruct((n_tok, D_MODEL), BF16),
                   jax.ShapeDtypeStruct((n_tok, LANES), F32)],
        compiler_params=_params(("parallel",)),
        name="outproj",
    )(o_att, yf, yb, g, gb, o_ssd, x, mod_l, w_out_bf, lnw, lnb, n2, router_w, router_b)


def _moe_kernel(h2_ref, comb_ref, x1_ref, mod_ref, wg_ref, wu_ref, wd_ref, o_ref, acc_ref):
    e = pl.program_id(1)

    @pl.when(e == 0)
    def _():
        acc_ref[...] = jnp.zeros_like(acc_ref)

    h2 = h2_ref[...]
    hid = _silu(jnp.dot(h2, wg_ref[...], preferred_element_type=F32)) * jnp.dot(
        h2, wu_ref[...], preferred_element_type=F32)
    comb = comb_ref[...]
    wcol = jnp.sum(jnp.where(_iota(comb.shape, 1) == e, comb, 0.0), axis=1, keepdims=True)
    acc_ref[...] += _bdot(hid * wcol, wd_ref[...])

    @pl.when(e == N_EXPERTS - 1)
    def _():
        o_ref[...] = x1_ref[...] + mod_ref[5] * acc_ref[...]


def _moe(h2, comb, x1, mod_l, row_fn, wg_bf, wu_bf, wd_bf):
    n_tok = x1.shape[0]
    tile = lambda w: pl.BlockSpec((MOE_TILE, w), lambda i, e: (i, 0))
    return pl.pallas_call(
        _moe_kernel,
        grid=(n_tok // MOE_TILE, N_EXPERTS),
        in_specs=[tile(D_MODEL), tile(LANES), tile(D_MODEL), _mod_spec(row_fn),
                  pl.BlockSpec((None, D_MODEL, D_FF), lambda i, e: (e, 0, 0)),
                  pl.BlockSpec((None, D_MODEL, D_FF), lambda i, e: (e, 0, 0)),
                  pl.BlockSpec((None, D_FF, D_MODEL), lambda i, e: (e, 0, 0))],
        out_specs=tile(D_MODEL),
        out_shape=jax.ShapeDtypeStruct((n_tok, D_MODEL), F32),
        scratch_shapes=[pltpu.VMEM((MOE_TILE, D_MODEL), F32)],
        compiler_params=_params(("parallel", "arbitrary")),
        name="moe",
    )(h2, comb, x1, mod_l, wg_bf, wu_bf, wd_bf)


def _rope_tables(seq_len):
    t = np.arange(seq_len)
    pos = np.stack([t // GRID_W, t % GRID_W], 0).astype(np.float32)
    n_freq = HEAD_DIM // 4
    inv = jnp.asarray(ROPE_BASE, F32) ** (-jnp.arange(n_freq, dtype=F32) / n_freq)
    ang = jnp.asarray(pos)[:, :, None] * inv
    cos, sin = jnp.cos(ang), jnp.sin(ang)
    cos_h = jnp.concatenate([cos[0], cos[0], cos[1], cos[1]], -1)
    sin_h = jnp.concatenate([-sin[0], sin[0], -sin[1], sin[1]], -1)
    return jnp.tile(cos_h, (1, 2)), jnp.tile(sin_h, (1, 2))


def _pad_rows(w, start, total):
    return jnp.pad(w, ((0, 0),) * (w.ndim - 2) + ((start, total - start - w.shape[-2]), (0, 0)))


def _layer_params(l, P):
    row = lambda a: a.reshape(1, -1)
    lp = {}
    lp["n1"] = row(P["norm1_w"][l])
    lp["n2"] = row(P["norm2_w"][l])
    lp["w_in"] = jnp.pad(P["w_in"][l], ((0, 0), (0, SSD_PAD - SSD_COLS))).astype(BF16)
    lp["w_out"] = P["w_out"][l].astype(BF16)
    lp["qw"] = row(jnp.tile(P["q_norm_w"][l], ATT_HEADS))
    lp["kw"] = row(jnp.tile(P["k_norm_w"][l], ATT_KV_HEADS))
    lp["sink"] = P["attn_sink"][l]
    lp["rw"] = (
        P["rw_mu"][l],
        _pad_rows(P["rw_g_up"][l], 64, LANES),
        _pad_rows(P["rw_w_up"][l], 0, LANES),
        _pad_rows(P["rw_a_up"][l], 32, LANES),
        P["rw_w0"][l], P["rw_a0"][l],
        row(P["rw_k_k"][l]), row(P["rw_k_a"][l]), row(P["rw_r_k"][l]),
    )
    lp["lnw"] = row(P["rw_ln_w"][l])
    lp["lnb"] = row(P["rw_ln_b"][l])
    pad_heads = lambda a: jnp.pad(a, ((0, 0), (0, LANES - a.shape[1]))).reshape(2, 1, LANES)
    lp["ssd"] = (
        P["ssd_conv_w"][l], row(P["ssd_conv_b"][l]),
        pad_heads(P["ssd_dt_bias"][l]), pad_heads(-jnp.exp(P["ssd_a_log"][l])),
        row(jnp.repeat(P["ssd_d"][l], 64)), row(P["ssd_norm_w"][l]),
    )
    lp["wg"] = P["exp_gate"][l].astype(BF16)
    lp["wu"] = P["exp_up"][l].astype(BF16)
    lp["wd"] = P["exp_down"][l].astype(BF16)
    return lp


def _trunk(x, nb, seq_len, is_ctx, mod, layers, router, cache_k, cache_v, state_rwkv, state_ssd):
    tiles_per_seq = seq_len // TOKEN_TILE
    moe_per_seq = max(seq_len // MOE_TILE, 1)
    if is_ctx:
        row_tok = lambda i: 0
        row_moe = lambda i: 0
        rope = None
    else:
        row_tok = lambda i: 1 + i // tiles_per_seq
        row_moe = lambda i: 1 + i // moe_per_seq
        rope = _rope_tables(seq_len)
    new_k, new_v, new_rw, new_ssd = [], [], [], []
    for l, lp in enumerate(layers):
        q, k, v, u_rw, u_ssd = _inproj(x, mod[l], row_tok, lp["n1"], lp["w_in"], lp["qw"], lp["kw"], rope, seq_len)
        if is_ctx:
            o_att = _attn_ctx(lp["sink"], q, k, v, nb, seq_len)
            rw0 = jnp.zeros((nb, 2, RW_HEADS, RW_HD, RW_HD), F32)
            ssd0 = jnp.zeros((nb, 2, 2, LANES, 64), F32)
        else:
            kc = cache_k[:, l].reshape(nb, -1, KV_W)
            vc = cache_v[:, l].reshape(nb, -1, KV_W)
            o_att = _attn_lat(lp["sink"], q, k.reshape(nb, seq_len, KV_W).reshape(nb * seq_len, KV_W),
                              v, kc, vc, nb, seq_len)
            rw0 = state_rwkv[:, l]
            ssd0 = state_ssd[:, l].reshape(nb, 2, 2, LANES, 64)
        yf, yb, g, gb, s_rw = _rw_mixer(u_rw, nb, seq_len, rw0, lp["rw"])
        o_ssd, s_ssd = _ssd_mixer(u_ssd, nb, seq_len, ssd0, *lp["ssd"])
        if is_ctx:
            new_k.append(k.reshape(nb, seq_len, ATT_KV_HEADS, HEAD_DIM))
            new_v.append(v.reshape(nb, seq_len, ATT_KV_HEADS, HEAD_DIM))
            new_rw.append(s_rw)
            new_ssd.append(s_ssd)
        x1, h2, comb = _outproj(o_att, yf, yb, g, gb, o_ssd, x, mod[l], row_tok, lp["w_out"],
                                lp["lnw"], lp["lnb"], lp["n2"], *router)
        x = _moe(h2, comb, x1, mod[l], row_moe, lp["wg"], lp["wu"], lp["wd"])
    if is_ctx:
        return x, tuple(jnp.stack(t, 1) for t in (new_k, new_v, new_rw, new_ssd))
    return x, None


def kernel(x_prompt, x_sample, cache_k, cache_v, state_rwkv, state_ssd, c, c_ctx, w_mod, b_mod, norm1_w, norm2_w, w_in, w_out, q_norm_w, k_norm_w, attn_sink, rw_mu, rw_w0, rw_w_up, rw_a0, rw_a_up, rw_g_up, rw_k_k, rw_k_a, rw_r_k, rw_ln_w, rw_ln_b, ssd_conv_w, ssd_conv_b, ssd_dt_bias, ssd_a_log, ssd_d, ssd_norm_w, router_w, router_bias, exp_gate, exp_up, exp_down):
    P = dict(norm1_w=norm1_w, norm2_w=norm2_w, w_in=w_in, w_out=w_out, q_norm_w=q_norm_w, k_norm_w=k_norm_w,
             attn_sink=attn_sink, rw_mu=rw_mu, rw_w0=rw_w0, rw_w_up=rw_w_up, rw_a0=rw_a0, rw_a_up=rw_a_up,
             rw_g_up=rw_g_up, rw_k_k=rw_k_k, rw_k_a=rw_k_a, rw_r_k=rw_r_k, rw_ln_w=rw_ln_w, rw_ln_b=rw_ln_b,
             ssd_conv_w=ssd_conv_w, ssd_conv_b=ssd_conv_b, ssd_dt_bias=ssd_dt_bias, ssd_a_log=ssd_a_log,
             ssd_d=ssd_d, ssd_norm_w=ssd_norm_w, exp_gate=exp_gate, exp_up=exp_up, exp_down=exp_down)
    nb_ctx, seq_ctx, _ = x_prompt.shape
    nb_lat, seq_lat, _ = x_sample.shape
    assert nb_lat + 1 <= SUBLANES and seq_ctx % TOKEN_TILE == 0 and seq_lat % MOE_TILE == 0
    assert nb_ctx % SCAN_BATCH == 0 and nb_lat % SCAN_BATCH == 0
    cond8 = jnp.zeros((SUBLANES, D_MODEL), F32).at[0].set(c_ctx).at[1:1 + nb_lat].set(c)
    mod = _modulation(cond8, w_mod, b_mod)
    layers = [_layer_params(l, P) for l in range(DEPTH)]
    router = (jnp.pad(router_w, ((0, 0), (0, LANES - N_EXPERTS))).astype(BF16),
              jnp.pad(router_bias, (0, LANES - N_EXPERTS)).reshape(1, LANES))
    y_prompt, ctx_state = _trunk(x_prompt.reshape(-1, D_MODEL), nb_ctx, seq_ctx, True, mod, layers, router,
                                 None, None, None, None)
    y_sample, _ = _trunk(x_sample.reshape(-1, D_MODEL), nb_lat, seq_lat, False, mod, layers, router,
                         cache_k, cache_v, state_rwkv, state_ssd)
    return (y_prompt.reshape(x_prompt.shape), y_sample.reshape(x_sample.shape)) + ctx_state
```

```python
import functools
import math

import jax
import jax.numpy as jnp
import numpy as np
from jax import lax
from jax.experimental import pallas as pl
from jax.experimental.pallas import tpu as pltpu

F32 = jnp.float32
BF16 = jnp.bfloat16
HIGHEST = lax.Precision.HIGHEST

D_MODEL = 1024
DEPTH = 2
GRID_W = 64
EPS = 1e-6
ATT_HEADS = 8
ATT_KV_HEADS = 2
HEAD_DIM = 64
ATT_W = 512
KV_W = 128
ATT_COLS = 768
WINDOW = 128
BLK = 128
ROPE_BASE = 10000.0
RW_HEADS = 4
RW_HD = 64
RW_W = 256
RW_COLS = 896
RW_DECAY_SCALE = 0.606531
RW_GN_EPS = 64e-5
SSD_W = 256
SSD_CHUNK = 128
SSD_COLS = 772
SSD_PAD = 896
IN_PAD = ATT_COLS + RW_COLS + SSD_PAD
N_EXPERTS = 16
EXP_PER_GROUP = 4
D_FF = 256

LANES = 128
SUBLANES = 8
TOKEN_TILE = 256
MOE_TILE = 1024
SCAN_CHUNK = 16
VMEM_LIMIT = 48 * 1024 * 1024


def _params(sem):
    return pltpu.CompilerParams(dimension_semantics=sem, vmem_limit_bytes=VMEM_LIMIT)


def _bdot(a, b):
    return jnp.dot(a.astype(BF16), b.astype(BF16), preferred_element_type=F32)


def _bdot_nt(a, b):
    return lax.dot_general(a.astype(BF16), b.astype(BF16), (((1,), (1,)), ((), ())),
                           preferred_element_type=F32)


def _xdot(a, b):
    return jnp.dot(a, b, precision=HIGHEST, preferred_element_type=F32)


def _iota(shape, axis):
    return lax.broadcasted_iota(jnp.int32, shape, axis)


def _group_ones(n, log2_blk):
    return ((_iota((n, n), 0) >> log2_blk) == (_iota((n, n), 1) >> log2_blk)).astype(F32)


def _sigmoid(x):
    return 1.0 / (1.0 + jnp.exp(-x))


def _silu(x):
    return x * _sigmoid(x)


def _rms(x, w):
    return x * lax.rsqrt(jnp.mean(x * x, -1, keepdims=True) + EPS) * w


def _mod_kernel(c_ref, w_ref, b_ref, o_ref):
    o_ref[...] = _bdot(_silu(c_ref[...]), w_ref[...]) + b_ref[...]


def _modulation(cond8, w_mod, b_mod):
    out = pl.pallas_call(
        _mod_kernel,
        grid=(DEPTH, 6),
        in_specs=[
            pl.BlockSpec((SUBLANES, D_MODEL), lambda l, j: (0, 0)),
            pl.BlockSpec((None, D_MODEL, D_MODEL), lambda l, j: (l, 0, j)),
            pl.BlockSpec((None, 1, D_MODEL), lambda l, j: (l, 0, j)),
        ],
        out_specs=pl.BlockSpec((None, None, SUBLANES, D_MODEL), lambda l, j: (l, j, 0, 0)),
        out_shape=jax.ShapeDtypeStruct((DEPTH, 6, SUBLANES, D_MODEL), F32),
        compiler_params=_params(("arbitrary", "arbitrary")),
        name="modulation",
    )(cond8, w_mod, b_mod.reshape(DEPTH, 1, 6 * D_MODEL))
    return out.reshape(DEPTH, 6, SUBLANES, 1, D_MODEL)


def _mod_spec(row_fn):
    return pl.BlockSpec((6, None, 1, D_MODEL), lambda *idx: (0, row_fn(idx[0]), 0, 0))


def _swap16(x):
    w = x.shape[1]
    first = (_iota(x.shape, 1) & 31) < 16
    return jnp.where(first, pltpu.roll(x, w - 16, 1), pltpu.roll(x, 16, 1))


def _inproj_kernel(use_rope, x_ref, mod_ref, n1_ref, w_ref, qw_ref, kw_ref, *rest):
    if use_rope:
        cos_ref, sin_ref, q_ref, k_ref, v_ref, urw_ref, ussd_ref = rest
    else:
        q_ref, k_ref, v_ref, urw_ref, ussd_ref = rest
    h = _rms(x_ref[...], n1_ref[...]) * (1.0 + mod_ref[1]) + mod_ref[0]
    u = _bdot(h, w_ref[...])
    q = u[:, :ATT_W]
    k = u[:, ATT_W:ATT_W + KV_W]
    ones_q = _group_ones(ATT_W, 6)
    ones_k = _group_ones(KV_W, 6)
    q = q * lax.rsqrt(_xdot(q * q, ones_q) * (1.0 / HEAD_DIM) + EPS) * qw_ref[...]
    k = k * lax.rsqrt(_xdot(k * k, ones_k) * (1.0 / HEAD_DIM) + EPS) * kw_ref[...]
    if use_rope:
        cos = cos_ref[...]
        sin = sin_ref[...]
        k = k * cos + _swap16(k) * sin
        cos4 = jnp.concatenate([cos] * 4, axis=1)
        sin4 = jnp.concatenate([sin] * 4, axis=1)
        q = q * cos4 + _swap16(q) * sin4
    q_ref[...] = q
    k_ref[...] = k
    v_ref[...] = u[:, ATT_W + KV_W:ATT_COLS]
    urw_ref[...] = u[:, ATT_COLS:ATT_COLS + RW_COLS]
    ussd_ref[...] = u[:, ATT_COLS + RW_COLS:]


def _inproj(x, mod_l, row_fn, n1, w_in_bf, qw, kw, rope, seq_len):
    n_tok = x.shape[0]
    tiles_per_seq = seq_len // TOKEN_TILE
    in_specs = [
        pl.BlockSpec((TOKEN_TILE, D_MODEL), lambda i: (i, 0)),
        _mod_spec(row_fn),
        pl.BlockSpec((1, D_MODEL), lambda i: (0, 0)),
        pl.BlockSpec((D_MODEL, IN_PAD), lambda i: (0, 0)),
        pl.BlockSpec((1, ATT_W), lambda i: (0, 0)),
        pl.BlockSpec((1, KV_W), lambda i: (0, 0)),
    ]
    args = [x, mod_l, n1, w_in_bf, qw, kw]
    if rope is not None:
        in_specs += [pl.BlockSpec((TOKEN_TILE, LANES), lambda i: (i % tiles_per_seq, 0))] * 2
        args += list(rope)
    widths = (ATT_W, KV_W, KV_W, RW_COLS, SSD_PAD)
    return pl.pallas_call(
        functools.partial(_inproj_kernel, rope is not None),
        grid=(n_tok // TOKEN_TILE,),
        in_specs=in_specs,
        out_specs=[pl.BlockSpec((TOKEN_TILE, w), lambda i: (i, 0)) for w in widths],
        out_shape=[jax.ShapeDtypeStruct((n_tok, w), F32) for w in widths],
        compiler_params=_params(("parallel",)),
        name="inproj",
    )(*args)


def _softmax_pv(s, sink, v_bf):
    m = jnp.maximum(jnp.max(s, -1, keepdims=True), sink)
    p = jnp.exp(s - m)
    den = jnp.sum(p, -1, keepdims=True) + jnp.exp(sink - m)
    return jnp.dot(p.astype(BF16), v_bf, preferred_element_type=F32) / den


def _attend_heads(sink_ref, q, k, v, mask):
    outs = []
    for hk in range(ATT_KV_HEADS):
        kh = k[:, hk * HEAD_DIM:(hk + 1) * HEAD_DIM].astype(BF16)
        vh = v[:, hk * HEAD_DIM:(hk + 1) * HEAD_DIM].astype(BF16)
        for g in range(ATT_HEADS // ATT_KV_HEADS):
            hd = hk * (ATT_HEADS // ATT_KV_HEADS) + g
            qh = q[:, hd * HEAD_DIM:(hd + 1) * HEAD_DIM]
            s = _bdot_nt(qh, kh) * (HEAD_DIM ** -0.5)
            if mask is not None:
                s = jnp.where(mask, s, -1e30)
            outs.append(_softmax_pv(s, sink_ref[hd], vh))
    return jnp.concatenate(outs, axis=1)


def _attn_ctx_kernel(sink_ref, q_ref, k_ref, v_ref, o_ref):
    o_ref[...] = _attend_heads(sink_ref, q_ref[...], k_ref[...], v_ref[...], None)


def _attn_ctx(sink, q, k, v, nb, seq_len):
    return pl.pallas_call(
        _attn_ctx_kernel,
        grid=(nb,),
        in_specs=[
            pl.BlockSpec(memory_space=pltpu.SMEM),
            pl.BlockSpec((seq_len, ATT_W), lambda b: (b, 0)),
            pl.BlockSpec((seq_len, KV_W), lambda b: (b, 0)),
            pl.BlockSpec((seq_len, KV_W), lambda b: (b, 0)),
        ],
        out_specs=pl.BlockSpec((seq_len, ATT_W), lambda b: (b, 0)),
        out_shape=jax.ShapeDtypeStruct((nb * seq_len, ATT_W), F32),
        compiler_params=_params(("parallel",)),
        name="attn_ctx",
    )(sink, q, k, v)


def _attn_lat_kernel(seq_len, sink_ref, q_ref, k_ref, v_ref, kc_ref, vc_ref, o_ref):
    i = pl.program_id(1)
    band = 3 * BLK
    start = pl.multiple_of(jnp.clip((i - 1) * BLK, 0, seq_len - band), BLK)
    k = jnp.concatenate([k_ref[pl.ds(start, band), :], kc_ref[...]], axis=0)
    v = jnp.concatenate([v_ref[pl.ds(start, band), :], vc_ref[...]], axis=0)
    n_keys = k.shape[0]
    col = _iota((BLK, n_keys), 1)
    dist = jnp.abs(i * BLK + _iota((BLK, n_keys), 0) - (start + col))
    mask = jnp.where(col >= band, 0, dist) <= WINDOW
    o_ref[...] = _attend_heads(sink_ref, q_ref[...], k, v, mask)


def _attn_lat(sink, q, k, v, kc, vc, nb, seq_len):
    nblk = seq_len // BLK
    past = kc.shape[1]
    return pl.pallas_call(
        functools.partial(_attn_lat_kernel, seq_len),
        grid=(nb, nblk),
        in_specs=[
            pl.BlockSpec(memory_space=pltpu.SMEM),
            pl.BlockSpec((BLK, ATT_W), lambda b, i: (b * nblk + i, 0)),
            pl.BlockSpec((seq_len, KV_W), lambda b, i: (b, 0)),
            pl.BlockSpec((seq_len, KV_W), lambda b, i: (b, 0)),
            pl.BlockSpec((None, past, KV_W), lambda b, i: (b, 0, 0)),
            pl.BlockSpec((None, past, KV_W), lambda b, i: (b, 0, 0)),
        ],
        out_specs=pl.BlockSpec((BLK, ATT_W), lambda b, i: (b * nblk + i, 0)),
        out_shape=jax.ShapeDtypeStruct((nb * seq_len, ATT_W), F32),
        compiler_params=_params(("parallel", "arbitrary")),
        name="attn_lat",
    )(sink, q, k, v, kc, vc)


RW_OPS = ("w_f", "kd_f", "b_f", "w_b", "kd_b", "b_b", "r", "kk")


def _rw_prep_kernel(lat, nb, tt, n_tiles, u_ref, up_ref, un_ref, mu_ref, wg_ref, ww_ref, wa_ref,
                    w0_ref, a0_ref, kk_ref, ka_ref, rk_ref, hsum_ref, *rest):
    if lat:
        ops_ref, vv_ref, g_ref, gb_ref, xs = rest
    else:
        ops_ref, g_ref, gb_ref, xs = rest
    i = pl.program_id(0)
    has_prev = (i > 0).astype(F32)
    has_next = (i < n_tiles - 1).astype(F32)
    row = _iota((tt, RW_COLS), 0)
    mu0 = mu_ref[0:1, :]
    mu1 = mu_ref[1:2, :]
    mixed = []
    for b in range(nb):
        ub = u_ref[b]
        prev = jnp.where(row == 0, up_ref[b, SUBLANES - 1:SUBLANES, :] * has_prev, pltpu.roll(ub, 1, 0))
        nxt = jnp.where(row == tt - 1, un_ref[b, 0:1, :] * has_next, pltpu.roll(ub, tt - 1, 0))
        mixed.append(ub + mu0 * (prev - ub) + mu1 * (nxt - ub))
    u = jnp.concatenate(mixed, axis=0)
    r = u[:, :RW_W]
    k = u[:, RW_W:2 * RW_W]
    v = u[:, 2 * RW_W:3 * RW_W]
    low = u[:, 3 * RW_W:]
    kk = k * kk_ref[...]
    kk = kk * lax.rsqrt(_xdot(kk * kk, _group_ones(RW_W, 6)) + 1e-12)
    tanh_low = jnp.tanh(low)

    def stash(j, x):
        xs[j, 0] = x[:, :LANES]
        xs[j, 1] = x[:, LANES:]

    for d in range(2):
        w = jnp.exp(-RW_DECAY_SCALE * _sigmoid(w0_ref[d:d + 1, :] + _bdot(tanh_low, ww_ref[d])))
        a = _sigmoid(a0_ref[d:d + 1, :] + _bdot(low, wa_ref[d]))
        stash(3 * d + 0, w)
        stash(3 * d + 1, k * (1.0 + (a - 1.0) * ka_ref[...]))
        stash(3 * d + 2, kk * a)
    stash(6, r)
    stash(7, kk)
    g = _bdot(_sigmoid(low), wg_ref[...])
    gb = _xdot(r * k * rk_ref[...], hsum_ref[...]) * v * g
    for b in range(nb):
        g_ref[b] = g[b * tt:(b + 1) * tt]
        gb_ref[b] = gb[b * tt:(b + 1) * tt]

    heads = range(RW_HEADS)
    if lat:
        def relayout(t, carry):
            for j in range(len(RW_OPS)):
                tiles = []
                for b in range(nb):
                    for hp in range(2):
                        tok = xs[j, hp, pl.ds(b * tt + t, 1), :]
                        tiles += [jnp.broadcast_to(tok[:, hh * RW_HD:(hh + 1) * RW_HD], (SUBLANES, RW_HD))
                                  for hh in range(2)]
                ops_ref[j, t] = jnp.concatenate(tiles, axis=0).T
            return carry

        lax.fori_loop(0, tt, relayout, 0)
        grp = LANES // nb
        for vi in range(SUBLANES):
            vv_ref[vi] = jnp.concatenate([v[b * tt:(b + 1) * tt, vi * grp:(vi + 1) * grp] for b in range(nb)], axis=1)
    else:
        stash(8, v)
        for t in range(tt):
            for j in range(len(RW_OPS) + 1):
                tiles = []
                for hp in range(2):
                    toks = xs[j, hp, pl.ds(t, nb, stride=tt), :]
                    tiles += [toks[:, hh * RW_HD:(hh + 1) * RW_HD] for hh in range(2)]
                ops_ref[j, t] = jnp.concatenate(tiles, axis=0).T


def _rw_prep(lat, u_rw, mu, wg, ww, wa, w0, a0, k_k, k_a, r_k, hsum):
    nb, seq_len, _ = u_rw.shape
    tt = (LANES // nb) if lat else SUBLANES
    n_tiles = seq_len // tt
    n_ops = len(RW_OPS) + (0 if lat else 1)
    sub = tt // SUBLANES
    last = seq_len // SUBLANES - 1
    full = lambda a: pl.BlockSpec(a.shape, lambda i: (0,) * a.ndim)
    tok_spec = pl.BlockSpec((nb, tt, RW_W), lambda i: (0, i, 0))
    tok_sds = jax.ShapeDtypeStruct((nb, seq_len, RW_W), F32)
    out_specs = [pl.BlockSpec((n_ops, tt, RW_HD, LANES), lambda i: (0, i, 0, 0))]
    out_shape = [jax.ShapeDtypeStruct((n_ops, seq_len, RW_HD, LANES), F32)]
    if lat:
        out_specs.append(pl.BlockSpec((SUBLANES, tt, LANES), lambda i: (0, i, 0)))
        out_shape.append(jax.ShapeDtypeStruct((SUBLANES, seq_len, LANES), F32))
    return pl.pallas_call(
        functools.partial(_rw_prep_kernel, lat, nb, tt, n_tiles),
        grid=(n_tiles,),
        in_specs=[
            pl.BlockSpec((nb, tt, RW_COLS), lambda i: (0, i, 0)),
            pl.BlockSpec((nb, SUBLANES, RW_COLS), lambda i: (0, jnp.maximum(i * sub - 1, 0), 0)),
            pl.BlockSpec((nb, SUBLANES, RW_COLS), lambda i: (0, jnp.minimum((i + 1) * sub, last), 0)),
        ] + [full(a) for a in (mu, wg, ww, wa, w0, a0, k_k, k_a, r_k, hsum)],
        out_specs=out_specs + [tok_spec, tok_spec],
        out_shape=out_shape + [tok_sds, tok_sds],
        scratch_shapes=[pltpu.VMEM((n_ops, 2, nb * tt, LANES), F32)],
        compiler_params=_params(("arbitrary",)),
        name="rw_prep_lat" if lat else "rw_prep_ctx",
    )(u_rw, u_rw, u_rw, mu, wg, ww, wa, w0, a0, k_k, k_a, r_k, hsum)


def _allsum_sublanes(x):
    x = x + pltpu.roll(x, 4, 0)
    x = x + pltpu.roll(x, 2, 0)
    return x + pltpu.roll(x, 1, 0)


def _rw_scan_kernel(lat, nv, n_chunks, *refs):
    if lat:
        af_ref, ab_ref, sf_ref, sb_ref, vf_ref, vb_ref, s0_ref, yf_ref, yb_ref, s_scr = refs
    else:
        af_ref, ab_ref, sf_ref, sb_ref, yf_ref, yb_ref, sfin_ref, s_scr = refs
        vf_ref = vb_ref = None
    c = pl.program_id(0)

    @pl.when(c == 0)
    def _():
        s_scr[...] = s0_ref[...] if lat else jnp.zeros(s_scr.shape, F32)

    sub = _iota((SUBLANES, LANES), 0)
    nkb = RW_HD // SUBLANES
    bcast = lambda x: jnp.broadcast_to(x, (SUBLANES, LANES))

    def step(tt, carry):
        dirs = ((af_ref, sf_ref, vf_ref, yf_ref, tt), (ab_ref, sb_ref, vb_ref, yb_ref, SCAN_CHUNK - 1 - tt))
        for d, (a_ref, s_ref, v_ref, y_ref, tl) in enumerate(dirs):
            keys = lambda ref, j, kb: ref[j, tl, pl.ds(kb * SUBLANES, SUBLANES), :]

            def value_block(vb, carry2):
                sa = [None] * SUBLANES
                for kb in range(nkb):
                    kk = keys(s_ref, 1, kb)
                    for vi in range(SUBLANES):
                        p = s_scr[d, vb, kb, vi] * kk
                        sa[vi] = p if kb == 0 else sa[vi] + p
                sa = [_allsum_sublanes(x) for x in sa]
                if lat:
                    vrow = [bcast(v_ref[vi, pl.ds(tl, 1), :]) for vi in range(SUBLANES)]
                else:
                    vrow = [bcast(s_ref[2, tl, pl.ds(vb * SUBLANES + vi, 1), :]) for vi in range(SUBLANES)]
                acc = [None] * SUBLANES
                for kb in range(nkb):
                    w, kd, bb, r = keys(a_ref, 0, kb), keys(a_ref, 1, kb), keys(a_ref, 2, kb), keys(s_ref, 0, kb)
                    for vi in range(SUBLANES):
                        sn = s_scr[d, vb, kb, vi] * w - sa[vi] * bb + vrow[vi] * kd
                        s_scr[d, vb, kb, vi] = sn
                        acc[vi] = sn * r if kb == 0 else acc[vi] + sn * r
                if lat:
                    for vi in range(SUBLANES):
                        y_ref[vi, pl.ds(tl, 1), :] = _allsum_sublanes(acc[vi])[0:1, :]
                else:
                    y = jnp.zeros((SUBLANES, LANES), F32)
                    for vi in range(SUBLANES):
                        y = jnp.where(sub == vi, _allsum_sublanes(acc[vi]), y)
                    y_ref[tl, pl.ds(pl.multiple_of(vb * SUBLANES, SUBLANES), SUBLANES), :] = y
                return carry2

            if nv == 1:
                value_block(0, 0)
            else:
                lax.fori_loop(0, nv, value_block, 0)
        return carry

    lax.fori_loop(0, SCAN_CHUNK, step, 0)

    if not lat:
        @pl.when(c == n_chunks - 1)
        def _():
            sfin_ref[...] = s_scr[...]


def _rw_scan(lat, ops, vv, s0):
    seq_len = ops.shape[1]
    nc = seq_len // SCAN_CHUNK
    nv = 1 if lat else RW_HD // SUBLANES
    n_shared = 2 if lat else 3
    op_spec = lambda n, blk, rev: pl.BlockSpec(
        (n, SCAN_CHUNK, RW_HD, LANES), lambda c: (blk, (nc - 1 - c) if rev else c, 0, 0))
    s_shape = (2, nv, SUBLANES, SUBLANES, SUBLANES, LANES)
    in_specs = [op_spec(3, 0, False), op_spec(3, 1, True),
                op_spec(n_shared, 6 // n_shared, False), op_spec(n_shared, 6 // n_shared, True)]
    args = [ops, ops, ops, ops]
    if lat:
        y_spec = lambda rev: pl.BlockSpec((SUBLANES, SCAN_CHUNK, LANES), lambda c: (0, (nc - 1 - c) if rev else c, 0))
        y_sds = jax.ShapeDtypeStruct((SUBLANES, seq_len, LANES), F32)
        in_specs += [y_spec(False), y_spec(True), pl.BlockSpec(s_shape, lambda c: (0,) * 6)]
        args += [vv, vv, s0]
        out_specs = [y_spec(False), y_spec(True)]
        out_shape = [y_sds, y_sds]
    else:
        y_spec = lambda rev: pl.BlockSpec((SCAN_CHUNK, RW_HD, LANES), lambda c: ((nc - 1 - c) if rev else c, 0, 0))
        y_sds = jax.ShapeDtypeStruct((seq_len, RW_HD, LANES), F32)
        out_specs = [y_spec(False), y_spec(True), pl.BlockSpec(s_shape, lambda c: (0,) * 6)]
        out_shape = [y_sds, y_sds, jax.ShapeDtypeStruct(s_shape, F32)]
    return pl.pallas_call(
        functools.partial(_rw_scan_kernel, lat, nv, nc),
        grid=(nc,),
        in_specs=in_specs,
        out_specs=out_specs,
        out_shape=out_shape,
        scratch_shapes=[pltpu.VMEM(s_shape, F32)],
        compiler_params=_params(("arbitrary",)),
        name="rw_scan_lat" if lat else "rw_scan_ctx",
    )(*args)


def _rw_post_ctx_kernel(nb, tt, yf_ref, yb_ref, g_ref, gb_ref, lnw_ref, lnb_ref, o_ref):
    for t in range(tt):
        y = yf_ref[t] + yb_ref[t]
        dev = y - jnp.mean(y, axis=0, keepdims=True)
        yn = (dev * lax.rsqrt(jnp.mean(dev * dev, axis=0, keepdims=True) + RW_GN_EPS)).T
        tok = jnp.concatenate([yn[h * nb:(h + 1) * nb, :] for h in range(RW_HEADS)], axis=1)
        o_ref[:, t, :] = (tok * lnw_ref[...] + lnb_ref[...]) * g_ref[:, t, :] + gb_ref[:, t, :]


def _rw_post_lat_kernel(nb, yf_ref, yb_ref, g_ref, gb_ref, lnw_ref, lnb_ref, hs_ref, o_ref):
    grp = LANES // nb
    ys = [yf_ref[vi] + yb_ref[vi] for vi in range(SUBLANES)]
    for b in range(nb):
        y = jnp.concatenate([yv[:, b * grp:(b + 1) * grp] for yv in ys], axis=1)
        dev = y - _xdot(y, hs_ref[...]) * (1.0 / RW_HD)
        yn = dev * lax.rsqrt(_xdot(dev * dev, hs_ref[...]) * (1.0 / RW_HD) + RW_GN_EPS)
        o_ref[b] = (yn * lnw_ref[...] + lnb_ref[...]) * g_ref[b] + gb_ref[b]


def _rw_post(lat, yf, yb, g, gb, lnw, lnb, hs):
    nb, seq_len, _ = g.shape
    tt = LANES if lat else SUBLANES
    full = lambda a: pl.BlockSpec(a.shape, lambda i: (0,) * a.ndim)
    tok_spec = pl.BlockSpec((nb, tt, RW_W), lambda i: (0, i, 0))
    if lat:
        y_spec = pl.BlockSpec((SUBLANES, tt, LANES), lambda i: (0, i, 0))
        body, extra = functools.partial(_rw_post_lat_kernel, nb), [hs]
    else:
        y_spec = pl.BlockSpec((tt, RW_HD, LANES), lambda i: (i, 0, 0))
        body, extra = functools.partial(_rw_post_ctx_kernel, nb, tt), []
    return pl.pallas_call(
        body,
        grid=(seq_len // tt,),
        in_specs=[y_spec, y_spec, tok_spec, tok_spec, full(lnw), full(lnb)] + [full(a) for a in extra],
        out_specs=tok_spec,
        out_shape=jax.ShapeDtypeStruct((nb, seq_len, RW_W), F32),
        compiler_params=_params(("parallel",)),
        name="rw_post_lat" if lat else "rw_post_ctx",
    )(yf, yb, g, gb, lnw, lnb, *extra)


def _rw_mixer(lat, u_rw, s0, prep_params, lnw, lnb, hs):
    nb = u_rw.shape[0]
    nkb = RW_HD // SUBLANES
    if lat:
        ops, vv, g, gb = _rw_prep(True, u_rw, *prep_params)
        grp = LANES // (nb * RW_HEADS)
        s0 = s0.reshape(nb, 2, RW_HEADS, grp, SUBLANES, nkb, SUBLANES)
        s0 = jnp.transpose(s0, (1, 5, 4, 6, 0, 2, 3)).reshape(2, 1, nkb, SUBLANES, SUBLANES, LANES)
        yf, yb = _rw_scan(True, ops, vv, s0)
        return _rw_post(True, yf, yb, g, gb, lnw, lnb, hs), None
    ops, g, gb = _rw_prep(False, u_rw, *prep_params)
    yf, yb, sfin = _rw_scan(False, ops, None, None)
    sfin = sfin.reshape(2, nkb, nkb, SUBLANES, SUBLANES, RW_HEADS, nb)
    sfin = jnp.transpose(sfin, (6, 0, 5, 1, 3, 2, 4)).reshape(nb, 2, RW_HEADS, RW_HD, RW_HD)
    return _rw_post(False, yf, yb, g, gb, lnw, lnb, hs), sfin


def _softplus(x):
    return jnp.maximum(x, 0.0) + jnp.log1p(jnp.exp(-jnp.abs(x)))


def _ssd_kernel(reverse, n_chunks, u_ref, up_ref, un_ref, h0_ref, cw_ref, cb_ref, dtb_ref, a_ref, *rest):
    if reverse:
        yf_ref, nw_ref, out_ref, hfin_ref, h_scr = rest
    else:
        dsk_ref, out_ref, hfin_ref, h_scr = rest
    s = pl.program_id(1)
    c = (n_chunks - 1 - s) if reverse else s

    @pl.when(s == 0)
    def _():
        h_scr[...] = h0_ref[...]

    q = SSD_CHUNK
    u = u_ref[...]
    xbc = u[:, SSD_W:SSD_W + 512]
    row = _iota(xbc.shape, 0)
    has_prev = (c > 0).astype(F32)
    has_next = (c < n_chunks - 1).astype(F32)
    prev = jnp.where(row == 0, up_ref[SUBLANES - 1:SUBLANES, SSD_W:SSD_W + 512] * has_prev, pltpu.roll(xbc, 1, 0))
    nxt = jnp.where(row == q - 1, un_ref[0:1, SSD_W:SSD_W + 512] * has_next, pltpu.roll(xbc, q - 1, 0))
    xc = _silu(cw_ref[0:1, :] * prev + cw_ref[1:2, :] * xbc + cw_ref[2:3, :] * nxt + cb_ref[...])
    x = xc[:, :SSD_W]
    b_all = xc[:, SSD_W:SSD_W + LANES]
    c_all = xc[:, SSD_W + LANES:]

    dt = _softplus(u[:, SSD_W + 512:] + dtb_ref[...])
    dta = dt * a_ref[...]
    ii = _iota((q, q), 0)
    jj = _iota((q, q), 1)
    tri = (jj >= ii) if reverse else (jj <= ii)
    cs = _xdot(tri.astype(F32), dta)
    cs_t = cs.T
    expand = ((_iota((LANES, SSD_W), 1) >> 6) == _iota((LANES, SSD_W), 0)).astype(F32)
    dt_e = _xdot(dt, expand)
    cs_e = _xdot(cs, expand)
    tot = cs_e[0:1, :] if reverse else cs_e[q - 1:q, :]
    xdt = x * dt_e
    dec_in = jnp.exp(tot - cs_e)
    dec_out = jnp.exp(cs_e)
    cdec = jnp.exp(tot)
    lane = _iota((q, LANES), 1)
    rowc = _iota((LANES, 1), 0)
    ys = []
    for g in range(2):
        bg = b_all[:, g * 64:(g + 1) * 64]
        cg = c_all[:, g * 64:(g + 1) * 64]
        gm = _bdot_nt(cg, bg)
        sl = slice(g * LANES, (g + 1) * LANES)
        xdt_g = xdt[:, sl]
        yd = []
        for hh in range(2):
            h = 2 * g + hh
            diff = cs[:, h:h + 1] - cs_t[h:h + 1, :]
            lm = jnp.exp(jnp.where(tri, diff, -jnp.inf))
            yd.append(_bdot(gm * lm, xdt_g))
        hg = h_scr[g]
        y = jnp.where(lane < 64, yd[0], yd[1]) + _bdot_nt(cg, hg) * dec_out[:, sl]
        st = _bdot((xdt_g * dec_in[:, sl]).T, bg)
        cd = cdec[:, sl]
        h_new = hg * jnp.where(rowc < 64, cd[:, 0:1], cd[:, 64:65]) + st
        h_scr[g] = h_new
        hfin_ref[g] = h_new
        ys.append(y)
    y = jnp.concatenate(ys, axis=1)
    if reverse:
        val = (yf_ref[...] + y) * _silu(u[:, :SSD_W])
        out_ref[...] = _rms(val, nw_ref[...])
    else:
        out_ref[...] = y + dsk_ref[...] * x


def _ssd_sweep(reverse, u_ssd, nb, seq_len, h0, cw, cb, dtb, a_neg, extra):
    nc = seq_len // SSD_CHUNK
    sub_per_chunk = SSD_CHUNK // SUBLANES
    last_sub = u_ssd.shape[0] // SUBLANES - 1
    chunk = (lambda s: nc - 1 - s) if reverse else (lambda s: s)
    rowblk = lambda b, s: b * nc + chunk(s)
    full = lambda a: pl.BlockSpec(a.shape, lambda b, s: (0,) * a.ndim)
    in_specs = [
        pl.BlockSpec((SSD_CHUNK, SSD_PAD), lambda b, s: (rowblk(b, s), 0)),
        pl.BlockSpec((SUBLANES, SSD_PAD), lambda b, s: (jnp.maximum(rowblk(b, s) * sub_per_chunk - 1, 0), 0)),
        pl.BlockSpec((SUBLANES, SSD_PAD),
                     lambda b, s: (jnp.minimum((rowblk(b, s) + 1) * sub_per_chunk, last_sub), 0)),
        pl.BlockSpec((None, 2, LANES, 64), lambda b, s: (b, 0, 0, 0)),
        full(cw), full(cb), full(dtb), full(a_neg),
    ]
    args = [u_ssd, u_ssd, u_ssd, h0, cw, cb, dtb, a_neg]
    if reverse:
        yf, nw = extra
        in_specs += [pl.BlockSpec((SSD_CHUNK, SSD_W), lambda b, s: (rowblk(b, s), 0)), full(nw)]
        args += [yf, nw]
    else:
        in_specs += [full(extra)]
        args += [extra]
    return pl.pallas_call(
        functools.partial(_ssd_kernel, reverse, nc),
        grid=(nb, nc),
        in_specs=in_specs,
        out_specs=[pl.BlockSpec((SSD_CHUNK, SSD_W), lambda b, s: (rowblk(b, s), 0)),
                   pl.BlockSpec((None, 2, LANES, 64), lambda b, s: (b, 0, 0, 0))],
        out_shape=[jax.ShapeDtypeStruct((nb * seq_len, SSD_W), F32),
                   jax.ShapeDtypeStruct((nb, 2, LANES, 64), F32)],
        scratch_shapes=[pltpu.VMEM((2, LANES, 64), F32)],
        compiler_params=_params(("parallel", "arbitrary")),
        name="ssd_bwd" if reverse else "ssd_fwd",
    )(*args)


def _ssd_mixer(u_ssd, nb, seq_len, h0, cw, cb, dtb, a_neg, dsk, nw):
    yf, hf = _ssd_sweep(False, u_ssd, nb, seq_len, h0[:, 0], cw, cb, dtb[0], a_neg[0], dsk)
    out, hb = _ssd_sweep(True, u_ssd, nb, seq_len, h0[:, 1], cw, cb, dtb[1], a_neg[1], (yf, nw))
    return out, jnp.stack([hf, hb], axis=1).reshape(nb, 2, 4, 64, 64)


def _group_peers(x, width, log2_stride):
    stride = 1 << log2_stride
    pos = (_iota(x.shape, 1) >> log2_stride) & 3
    peers = []
    for j in (1, 2, 3):
        ahead = pltpu.roll(x, width - j * stride, 1)
        behind = pltpu.roll(x, (4 - j) * stride, 1)
        peers.append((jnp.where(pos + j < 4, ahead, behind), pos + j >= 4))
    return peers


def _route(scores, bias):
    sel = scores + bias
    rank = jnp.zeros(sel.shape, jnp.int32)
    as_int = lambda cond: jnp.where(cond, 1, 0)
    for other, wrapped in _group_peers(sel, LANES, 0):
        rank = rank + jnp.where(wrapped, as_int(other >= sel), as_int(other > sel))
    top2 = rank < 2
    m = jnp.where(top2, sel, 0.0)
    grp = m
    for other, _ in _group_peers(m, LANES, 0):
        grp = grp + other
    best = jnp.ones(sel.shape, jnp.int32)
    for other, wrapped in _group_peers(grp, LANES, 2):
        loses = jnp.where(wrapped, as_int(other >= grp), as_int(other > grp))
        best = best * (1 - loses)
    chosen = jnp.where(top2, best, 0) > 0
    cw = jnp.where(chosen, scores, 0.0)
    den = cw
    for other, _ in _group_peers(cw, LANES, 0):
        den = den + other
    return jnp.where(chosen, cw / jnp.where(chosen, den, 1.0), 0.0)


def _outproj_kernel(oatt_ref, orw_ref, ossd_ref, x_ref, mod_ref, wout_ref,
                    n2_ref, rw_ref, rb_ref, x1_ref, h2_ref, comb_ref):
    o = jnp.concatenate([oatt_ref[...], orw_ref[...], ossd_ref[...]], axis=1)
    x1 = x_ref[...] + mod_ref[2] * _bdot(o, wout_ref[...])
    h2 = _rms(x1, n2_ref[...]) * (1.0 + mod_ref[4]) + mod_ref[3]
    scores = _sigmoid(_bdot(h2, rw_ref[...]))
    x1_ref[...] = x1
    h2_ref[...] = h2.astype(BF16)
    comb_ref[...] = _route(scores, rb_ref[...])


def _outproj(o_att, o_rw, o_ssd, x, mod_l, row_fn, w_out_bf, n2, router_w, router_b):
    n_tok = x.shape[0]
    tile = lambda w: pl.BlockSpec((TOKEN_TILE, w), lambda i: (i, 0))
    full = lambda a: pl.BlockSpec(a.shape, lambda i: (0,) * a.ndim)
    return pl.pallas_call(
        _outproj_kernel,
        grid=(n_tok // TOKEN_TILE,),
        in_specs=[tile(ATT_W), tile(RW_W), tile(SSD_W), tile(D_MODEL),
                  _mod_spec(row_fn)] + [full(a) for a in (w_out_bf, n2, router_w, router_b)],
        out_specs=[tile(D_MODEL), tile(D_MODEL), tile(LANES)],
        out_shape=[jax.ShapeDtypeStruct((n_tok, D_MODEL), F32),
                   jax.ShapeDtypeStruct((n_tok, D_MODEL), BF16),
                   jax.ShapeDtypeStruct((n_tok, LANES), F32)],
        compiler_params=_params(("parallel",)),
        name="outproj",
    )(o_att, o_rw, o_ssd, x, mod_l, w_out_bf, n2, router_w, router_b)


def _moe_kernel(h2_ref, comb_ref, x1_ref, mod_ref, wg_ref, wu_ref, wd_ref, o_ref, acc_ref):
    e = pl.program_id(1)

    @pl.when(e == 0)
    def _():
        acc_ref[...] = jnp.zeros_like(acc_ref)

    h2 = h2_ref[...]
    hid = _silu(jnp.dot(h2, wg_ref[...], preferred_element_type=F32)) * jnp.dot(
        h2, wu_ref[...], preferred_element_type=F32)
    comb = comb_ref[...]
    wcol = jnp.sum(jnp.where(_iota(comb.shape, 1) == e, comb, 0.0), axis=1, keepdims=True)
    acc_ref[...] += _bdot(hid * wcol, wd_ref[...])

    @pl.when(e == N_EXPERTS - 1)
    def _():
        o_ref[...] = x1_ref[...] + mod_ref[5] * acc_ref[...]


def _moe(h2, comb, x1, mod_l, row_fn, wg_bf, wu_bf, wd_bf):
    n_tok = x1.shape[0]
    tile = lambda w: pl.BlockSpec((MOE_TILE, w), lambda i, e: (i, 0))
    return pl.pallas_call(
        _moe_kernel,
        grid=(n_tok // MOE_TILE, N_EXPERTS),
        in_specs=[tile(D_MODEL), tile(LANES), tile(D_MODEL), _mod_spec(row_fn),
                  pl.BlockSpec((None, D_MODEL, D_FF), lambda i, e: (e, 0, 0)),
                  pl.BlockSpec((None, D_MODEL, D_FF), lambda i, e: (e, 0, 0)),
                  pl.BlockSpec((None, D_FF, D_MODEL), lambda i, e: (e, 0, 0))],
        out_specs=tile(D_MODEL),
        out_shape=jax.ShapeDtypeStruct((n_tok, D_MODEL), F32),
        scratch_shapes=[pltpu.VMEM((MOE_TILE, D_MODEL), F32)],
        compiler_params=_params(("parallel", "arbitrary")),
        name="moe",
    )(h2, comb, x1, mod_l, wg_bf, wu_bf, wd_bf)


def _rope_tables(seq_len):
    t = np.arange(seq_len)
    pos = np.stack([t // GRID_W, t % GRID_W], 0).astype(np.float32)
    n_freq = HEAD_DIM // 4
    inv = jnp.asarray(ROPE_BASE, F32) ** (-jnp.arange(n_freq, dtype=F32) / n_freq)
    ang = jnp.asarray(pos)[:, :, None] * inv
    cos, sin = jnp.cos(ang), jnp.sin(ang)
    cos_h = jnp.concatenate([cos[0], cos[0], cos[1], cos[1]], -1)
    sin_h = jnp.concatenate([-sin[0], sin[0], -sin[1], sin[1]], -1)
    return jnp.tile(cos_h, (1, 2)), jnp.tile(sin_h, (1, 2))


def _pad_rows(w, start, total):
    return jnp.pad(w, ((0, 0),) * (w.ndim - 2) + ((start, total - start - w.shape[-2]), (0, 0)))


def _latent_value_order(nb):
    grp = LANES // (nb * RW_HEADS)
    c = np.arange(RW_W)
    vi, h, vg = c // (RW_HEADS * grp), (c // grp) % RW_HEADS, c % grp
    return h * RW_HD + vg * SUBLANES + vi, h


def _rw_layer_params(l, P, order):
    row = lambda a: a.reshape(1, -1)
    perm, head = order
    v0 = 2 * RW_W
    in_cols = np.arange(IN_PAD)
    in_cols[ATT_COLS + v0:ATT_COLS + v0 + RW_W] = ATT_COLS + v0 + perm
    mu_cols = np.arange(RW_COLS)
    mu_cols[v0:v0 + RW_W] = v0 + perm
    out_rows = np.arange(D_MODEL)
    out_rows[ATT_W:ATT_W + RW_W] = ATT_W + perm
    w_in = jnp.pad(P["w_in"][l], ((0, 0), (0, SSD_PAD - SSD_COLS))).astype(BF16)
    hsum = (np.arange(RW_W)[:, None] // RW_HD == head[None, :]).astype(np.float32)
    rp = {}
    rp["w_in"] = w_in[:, in_cols]
    rp["w_out"] = P["w_out"][l].astype(BF16)[out_rows]
    rp["prep"] = (
        P["rw_mu"][l][:, mu_cols],
        _pad_rows(P["rw_g_up"][l][:, perm], 64, LANES),
        _pad_rows(P["rw_w_up"][l], 0, LANES),
        _pad_rows(P["rw_a_up"][l], 32, LANES),
        P["rw_w0"][l], P["rw_a0"][l],
        row(P["rw_k_k"][l]), row(P["rw_k_a"][l]), row(P["rw_r_k"][l]),
        jnp.asarray(hsum),
    )
    rp["lnw"] = row(P["rw_ln_w"][l][perm])
    rp["lnb"] = row(P["rw_ln_b"][l][perm])
    rp["hs"] = jnp.asarray((head[:, None] == head[None, :]).astype(np.float32))
    return rp


def _layer_params(l, P):
    row = lambda a: a.reshape(1, -1)
    lp = {}
    lp["n1"] = row(P["norm1_w"][l])
    lp["n2"] = row(P["norm2_w"][l])
    lp["qw"] = row(jnp.tile(P["q_norm_w"][l], ATT_HEADS))
    lp["kw"] = row(jnp.tile(P["k_norm_w"][l], ATT_KV_HEADS))
    lp["sink"] = P["attn_sink"][l]
    pad_heads = lambda a: jnp.pad(a, ((0, 0), (0, LANES - a.shape[1]))).reshape(2, 1, LANES)
    lp["ssd"] = (
        P["ssd_conv_w"][l], row(P["ssd_conv_b"][l]),
        pad_heads(P["ssd_dt_bias"][l]), pad_heads(-jnp.exp(P["ssd_a_log"][l])),
        row(jnp.repeat(P["ssd_d"][l], 64)), row(P["ssd_norm_w"][l]),
    )
    lp["wg"] = P["exp_gate"][l].astype(BF16)
    lp["wu"] = P["exp_up"][l].astype(BF16)
    lp["wd"] = P["exp_down"][l].astype(BF16)
    return lp


def _trunk(x, nb, seq_len, is_ctx, mod, layers, rw_layers, router, cache_k, cache_v, state_rwkv, state_ssd):
    tiles_per_seq = seq_len // TOKEN_TILE
    moe_per_seq = max(seq_len // MOE_TILE, 1)
    if is_ctx:
        row_tok = lambda i: 0
        row_moe = lambda i: 0
        rope = None
    else:
        row_tok = lambda i: 1 + i // tiles_per_seq
        row_moe = lambda i: 1 + i // moe_per_seq
        rope = _rope_tables(seq_len)
    new_k, new_v, new_rw, new_ssd = [], [], [], []
    for l, (lp, rp) in enumerate(zip(layers, rw_layers)):
        q, k, v, u_rw, u_ssd = _inproj(x, mod[l], row_tok, lp["n1"], rp["w_in"], lp["qw"], lp["kw"], rope, seq_len)
        if is_ctx:
            o_att = _attn_ctx(lp["sink"], q, k, v, nb, seq_len)
            rw0 = None
            ssd0 = jnp.zeros((nb, 2, 2, LANES, 64), F32)
        else:
            kc = cache_k[:, l].reshape(nb, -1, KV_W)
            vc = cache_v[:, l].reshape(nb, -1, KV_W)
            o_att = _attn_lat(lp["sink"], q, k, v, kc, vc, nb, seq_len)
            rw0 = state_rwkv[:, l]
            ssd0 = state_ssd[:, l].reshape(nb, 2, 2, LANES, 64)
        o_rw, s_rw = _rw_mixer(not is_ctx, u_rw.reshape(nb, seq_len, RW_COLS), rw0, rp["prep"],
                               rp["lnw"], rp["lnb"], rp["hs"])
        o_ssd, s_ssd = _ssd_mixer(u_ssd, nb, seq_len, ssd0, *lp["ssd"])
        if is_ctx:
            new_k.append(k.reshape(nb, seq_len, ATT_KV_HEADS, HEAD_DIM))
            new_v.append(v.reshape(nb, seq_len, ATT_KV_HEADS, HEAD_DIM))
            new_rw.append(s_rw)
            new_ssd.append(s_ssd)
        x1, h2, comb = _outproj(o_att, o_rw.reshape(nb * seq_len, RW_W), o_ssd, x, mod[l], row_tok, rp["w_out"],
                                lp["n2"], *router)
        x = _moe(h2, comb, x1, mod[l], row_moe, lp["wg"], lp["wu"], lp["wd"])
    if is_ctx:
        return x, tuple(jnp.stack(t, 1) for t in (new_k, new_v, new_rw, new_ssd))
    return x, None


def kernel(x_prompt, x_sample, cache_k, cache_v, state_rwkv, state_ssd, c, c_ctx, w_mod, b_mod, norm1_w, norm2_w, w_in, w_out, q_norm_w, k_norm_w, attn_sink, rw_mu, rw_w0, rw_w_up, rw_a0, rw_a_up, rw_g_up, rw_k_k, rw_k_a, rw_r_k, rw_ln_w, rw_ln_b, ssd_conv_w, ssd_conv_b, ssd_dt_bias, ssd_a_log, ssd_d, ssd_norm_w, router_w, router_bias, exp_gate, exp_up, exp_down):
    P = dict(norm1_w=norm1_w, norm2_w=norm2_w, w_in=w_in, w_out=w_out, q_norm_w=q_norm_w, k_norm_w=k_norm_w,
             attn_sink=attn_sink, rw_mu=rw_mu, rw_w0=rw_w0, rw_w_up=rw_w_up, rw_a0=rw_a0, rw_a_up=rw_a_up,
             rw_g_up=rw_g_up, rw_k_k=rw_k_k, rw_k_a=rw_k_a, rw_r_k=rw_r_k, rw_ln_w=rw_ln_w, rw_ln_b=rw_ln_b,
             ssd_conv_w=ssd_conv_w, ssd_conv_b=ssd_conv_b, ssd_dt_bias=ssd_dt_bias, ssd_a_log=ssd_a_log,
             ssd_d=ssd_d, ssd_norm_w=ssd_norm_w, exp_gate=exp_gate, exp_up=exp_up, exp_down=exp_down)
    nb_ctx, seq_ctx, _ = x_prompt.shape
    nb_lat, seq_lat, _ = x_sample.shape
    assert nb_lat + 1 <= SUBLANES and seq_ctx % TOKEN_TILE == 0 and seq_lat % MOE_TILE == 0
    assert nb_ctx * RW_HEADS == LANES and LANES % (nb_lat * RW_HEADS) == 0
    cond8 = jnp.zeros((SUBLANES, D_MODEL), F32).at[0].set(c_ctx).at[1:1 + nb_lat].set(c)
    mod = _modulation(cond8, w_mod, b_mod)
    layers = [_layer_params(l, P) for l in range(DEPTH)]
    ctx_order = (np.arange(RW_W), np.arange(RW_W) // RW_HD)
    rw_ctx = [_rw_layer_params(l, P, ctx_order) for l in range(DEPTH)]
    rw_lat = [_rw_layer_params(l, P, _latent_value_order(nb_lat)) for l in range(DEPTH)]
    router = (jnp.pad(router_w, ((0, 0), (0, LANES - N_EXPERTS))).astype(BF16),
              jnp.pad(router_bias, (0, LANES - N_EXPERTS)).reshape(1, LANES))
    y_prompt, ctx_state = _trunk(x_prompt.reshape(-1, D_MODEL), nb_ctx, seq_ctx, True, mod, layers, rw_ctx, router,
                                 None, None, None, None)
    y_sample, _ = _trunk(x_sample.reshape(-1, D_MODEL), nb_lat, seq_lat, False, mod, layers, rw_lat, router,
                         cache_k, cache_v, state_rwkv, state_ssd)
    return (y_prompt.reshape(x_prompt.shape), y_sample.reshape(x_sample.shape)) + ctx_state
```

```python
import functools
import math

import jax
import jax.numpy as jnp
import numpy as np
from jax import lax
from jax.experimental import pallas as pl
from jax.experimental.pallas import tpu as pltpu

F32 = jnp.float32
BF16 = jnp.bfloat16
HIGHEST = lax.Precision.HIGHEST

D_MODEL = 1024
DEPTH = 2
GRID_W = 64
EPS = 1e-6
ATT_HEADS = 8
ATT_KV_HEADS = 2
HEAD_DIM = 64
ATT_W = 512
KV_W = 128
ATT_COLS = 768
WINDOW = 128
BLK = 128
ROPE_BASE = 10000.0
RW_HEADS = 4
RW_HD = 64
RW_W = 256
RW_COLS = 896
RW_DECAY_SCALE = 0.606531
RW_GN_EPS = 64e-5
SSD_W = 256
SSD_CHUNK = 128
SSD_COLS = 772
SSD_PAD = 896
IN_PAD = ATT_COLS + RW_COLS + SSD_PAD
N_EXPERTS = 16
EXP_PER_GROUP = 4
D_FF = 256

LANES = 128
SUBLANES = 8
TOKEN_TILE = 256
MOE_TILE = 1024
SCAN_CHUNK = 16
VMEM_LIMIT = 48 * 1024 * 1024


def _params(sem):
    return pltpu.CompilerParams(dimension_semantics=sem, vmem_limit_bytes=VMEM_LIMIT)


def _bdot(a, b):
    return jnp.dot(a.astype(BF16), b.astype(BF16), preferred_element_type=F32)


def _bdot_nt(a, b):
    return lax.dot_general(a.astype(BF16), b.astype(BF16), (((1,), (1,)), ((), ())),
                           preferred_element_type=F32)


def _xdot(a, b):
    return jnp.dot(a, b, precision=HIGHEST, preferred_element_type=F32)


def _iota(shape, axis):
    return lax.broadcasted_iota(jnp.int32, shape, axis)


def _group_ones(n, log2_blk):
    return ((_iota((n, n), 0) >> log2_blk) == (_iota((n, n), 1) >> log2_blk)).astype(F32)


def _sigmoid(x):
    return 1.0 / (1.0 + jnp.exp(-x))


def _silu(x):
    return x * _sigmoid(x)


def _rms(x, w):
    return x * lax.rsqrt(jnp.mean(x * x, -1, keepdims=True) + EPS) * w


def _mod_kernel(c_ref, w_ref, b_ref, o_ref):
    o_ref[...] = _bdot(_silu(c_ref[...]), w_ref[...]) + b_ref[...]


def _modulation(cond8, w_mod, b_mod):
    out = pl.pallas_call(
        _mod_kernel,
        grid=(DEPTH, 6),
        in_specs=[
            pl.BlockSpec((SUBLANES, D_MODEL), lambda l, j: (0, 0)),
            pl.BlockSpec((None, D_MODEL, D_MODEL), lambda l, j: (l, 0, j)),
            pl.BlockSpec((None, 1, D_MODEL), lambda l, j: (l, 0, j)),
        ],
        out_specs=pl.BlockSpec((None, None, SUBLANES, D_MODEL), lambda l, j: (l, j, 0, 0)),
        out_shape=jax.ShapeDtypeStruct((DEPTH, 6, SUBLANES, D_MODEL), F32),
        compiler_params=_params(("arbitrary", "arbitrary")),
        name="modulation",
    )(cond8, w_mod, b_mod.reshape(DEPTH, 1, 6 * D_MODEL))
    return out.reshape(DEPTH, 6, SUBLANES, 1, D_MODEL)


def _mod_spec(row_fn):
    return pl.BlockSpec((6, None, 1, D_MODEL), lambda *idx: (0, row_fn(idx[0]), 0, 0))


def _swap16(x):
    w = x.shape[1]
    first = (_iota(x.shape, 1) & 31) < 16
    return jnp.where(first, pltpu.roll(x, w - 16, 1), pltpu.roll(x, 16, 1))


def _inproj_kernel(use_rope, x_ref, mod_ref, n1_ref, w_ref, qw_ref, kw_ref, *rest):
    if use_rope:
        cos_ref, sin_ref, q_ref, k_ref, v_ref, urw_ref, ussd_ref = rest
    else:
        q_ref, k_ref, v_ref, urw_ref, ussd_ref = rest
    h = _rms(x_ref[...], n1_ref[...]) * (1.0 + mod_ref[1]) + mod_ref[0]
    u = _bdot(h, w_ref[...])
    q = u[:, :ATT_W]
    k = u[:, ATT_W:ATT_W + KV_W]
    ones_q = _group_ones(ATT_W, 6)
    ones_k = _group_ones(KV_W, 6)
    q = q * lax.rsqrt(_xdot(q * q, ones_q) * (1.0 / HEAD_DIM) + EPS) * qw_ref[...]
    k = k * lax.rsqrt(_xdot(k * k, ones_k) * (1.0 / HEAD_DIM) + EPS) * kw_ref[...]
    if use_rope:
        cos = cos_ref[...]
        sin = sin_ref[...]
        k = k * cos + _swap16(k) * sin
        cos4 = jnp.concatenate([cos] * 4, axis=1)
        sin4 = jnp.concatenate([sin] * 4, axis=1)
        q = q * cos4 + _swap16(q) * sin4
    q_ref[...] = q
    k_ref[...] = k
    v_ref[...] = u[:, ATT_W + KV_W:ATT_COLS]
    urw_ref[...] = u[:, ATT_COLS:ATT_COLS + RW_COLS]
    ussd_ref[...] = u[:, ATT_COLS + RW_COLS:]


def _inproj(x, mod_l, row_fn, n1, w_in_bf, qw, kw, rope, seq_len):
    n_tok = x.shape[0]
    tiles_per_seq = seq_len // TOKEN_TILE
    in_specs = [
        pl.BlockSpec((TOKEN_TILE, D_MODEL), lambda i: (i, 0)),
        _mod_spec(row_fn),
        pl.BlockSpec((1, D_MODEL), lambda i: (0, 0)),
        pl.BlockSpec((D_MODEL, IN_PAD), lambda i: (0, 0)),
        pl.BlockSpec((1, ATT_W), lambda i: (0, 0)),
        pl.BlockSpec((1, KV_W), lambda i: (0, 0)),
    ]
    args = [x, mod_l, n1, w_in_bf, qw, kw]
    if rope is not None:
        in_specs += [pl.BlockSpec((TOKEN_TILE, LANES), lambda i: (i % tiles_per_seq, 0))] * 2
        args += list(rope)
    widths = (ATT_W, KV_W, KV_W, RW_COLS, SSD_PAD)
    return pl.pallas_call(
        functools.partial(_inproj_kernel, rope is not None),
        grid=(n_tok // TOKEN_TILE,),
        in_specs=in_specs,
        out_specs=[pl.BlockSpec((TOKEN_TILE, w), lambda i: (i, 0)) for w in widths],
        out_shape=[jax.ShapeDtypeStruct((n_tok, w), F32) for w in widths],
        compiler_params=_params(("parallel",)),
        name="inproj",
    )(*args)


def _softmax_pv(s, sink, v_bf):
    m = jnp.maximum(jnp.max(s, -1, keepdims=True), sink)
    p = jnp.exp(s - m)
    den = jnp.sum(p, -1, keepdims=True) + jnp.exp(sink - m)
    return jnp.dot(p.astype(BF16), v_bf, preferred_element_type=F32) / den


def _attend_heads(sink_ref, q, k, v, mask):
    outs = []
    for hk in range(ATT_KV_HEADS):
        kh = k[:, hk * HEAD_DIM:(hk + 1) * HEAD_DIM].astype(BF16)
        vh = v[:, hk * HEAD_DIM:(hk + 1) * HEAD_DIM].astype(BF16)
        for g in range(ATT_HEADS // ATT_KV_HEADS):
            hd = hk * (ATT_HEADS // ATT_KV_HEADS) + g
            qh = q[:, hd * HEAD_DIM:(hd + 1) * HEAD_DIM]
            s = _bdot_nt(qh, kh) * (HEAD_DIM ** -0.5)
            if mask is not None:
                s = jnp.where(mask, s, -1e30)
            outs.append(_softmax_pv(s, sink_ref[hd], vh))
    return jnp.concatenate(outs, axis=1)


def _attn_ctx_kernel(sink_ref, q_ref, k_ref, v_ref, o_ref):
    o_ref[...] = _attend_heads(sink_ref, q_ref[...], k_ref[...], v_ref[...], None)


def _attn_ctx(sink, q, k, v, nb, seq_len):
    return pl.pallas_call(
        _attn_ctx_kernel,
        grid=(nb,),
        in_specs=[
            pl.BlockSpec(memory_space=pltpu.SMEM),
            pl.BlockSpec((seq_len, ATT_W), lambda b: (b, 0)),
            pl.BlockSpec((seq_len, KV_W), lambda b: (b, 0)),
            pl.BlockSpec((seq_len, KV_W), lambda b: (b, 0)),
        ],
        out_specs=pl.BlockSpec((seq_len, ATT_W), lambda b: (b, 0)),
        out_shape=jax.ShapeDtypeStruct((nb * seq_len, ATT_W), F32),
        compiler_params=_params(("parallel",)),
        name="attn_ctx",
    )(sink, q, k, v)


def _attn_lat_kernel(seq_len, sink_ref, q_ref, k_ref, v_ref, kc_ref, vc_ref, o_ref):
    i = pl.program_id(1)
    band = 3 * BLK
    start = pl.multiple_of(jnp.clip((i - 1) * BLK, 0, seq_len - band), BLK)
    k = jnp.concatenate([k_ref[pl.ds(start, band), :], kc_ref[...]], axis=0)
    v = jnp.concatenate([v_ref[pl.ds(start, band), :], vc_ref[...]], axis=0)
    n_keys = k.shape[0]
    col = _iota((BLK, n_keys), 1)
    dist = jnp.abs(i * BLK + _iota((BLK, n_keys), 0) - (start + col))
    mask = jnp.where(col >= band, 0, dist) <= WINDOW
    o_ref[...] = _attend_heads(sink_ref, q_ref[...], k, v, mask)


def _attn_lat(sink, q, k, v, kc, vc, nb, seq_len):
    nblk = seq_len // BLK
    past = kc.shape[1]
    return pl.pallas_call(
        functools.partial(_attn_lat_kernel, seq_len),
        grid=(nb, nblk),
        in_specs=[
            pl.BlockSpec(memory_space=pltpu.SMEM),
            pl.BlockSpec((BLK, ATT_W), lambda b, i: (b * nblk + i, 0)),
            pl.BlockSpec((seq_len, KV_W), lambda b, i: (b, 0)),
            pl.BlockSpec((seq_len, KV_W), lambda b, i: (b, 0)),
            pl.BlockSpec((None, past, KV_W), lambda b, i: (b, 0, 0)),
            pl.BlockSpec((None, past, KV_W), lambda b, i: (b, 0, 0)),
        ],
        out_specs=pl.BlockSpec((BLK, ATT_W), lambda b, i: (b * nblk + i, 0)),
        out_shape=jax.ShapeDtypeStruct((nb * seq_len, ATT_W), F32),
        compiler_params=_params(("parallel", "arbitrary")),
        name="attn_lat",
    )(sink, q, k, v, kc, vc)


RW_OPS = ("w_f", "kd_f", "b_f", "w_b", "kd_b", "b_b", "r", "kk")


def _rw_prep_kernel(lat, nb, tt, n_tiles, u_ref, up_ref, un_ref, mu_ref, wg_ref, ww_ref, wa_ref,
                    w0_ref, a0_ref, kk_ref, ka_ref, rk_ref, hsum_ref, *rest):
    if lat:
        ops_ref, vv_ref, g_ref, gb_ref, xs = rest
    else:
        ops_ref, g_ref, gb_ref, xs = rest
    i = pl.program_id(0)
    has_prev = (i > 0).astype(F32)
    has_next = (i < n_tiles - 1).astype(F32)
    row = _iota((tt, RW_COLS), 0)
    mu0 = mu_ref[0:1, :]
    mu1 = mu_ref[1:2, :]
    mixed = []
    for b in range(nb):
        ub = u_ref[b]
        prev = jnp.where(row == 0, up_ref[b, SUBLANES - 1:SUBLANES, :] * has_prev, pltpu.roll(ub, 1, 0))
        nxt = jnp.where(row == tt - 1, un_ref[b, 0:1, :] * has_next, pltpu.roll(ub, tt - 1, 0))
        mixed.append(ub + mu0 * (prev - ub) + mu1 * (nxt - ub))
    u = jnp.concatenate(mixed, axis=0)
    r = u[:, :RW_W]
    k = u[:, RW_W:2 * RW_W]
    v = u[:, 2 * RW_W:3 * RW_W]
    low = u[:, 3 * RW_W:]
    kk = k * kk_ref[...]
    kk = kk * lax.rsqrt(_xdot(kk * kk, _group_ones(RW_W, 6)) + 1e-12)
    tanh_low = jnp.tanh(low)
    vals = [None] * len(RW_OPS)
    for d in range(2):
        w = jnp.exp(-RW_DECAY_SCALE * _sigmoid(w0_ref[d:d + 1, :] + _bdot(tanh_low, ww_ref[d])))
        a = _sigmoid(a0_ref[d:d + 1, :] + _bdot(low, wa_ref[d]))
        vals[3 * d:3 * d + 3] = [w, k * (1.0 + (a - 1.0) * ka_ref[...]), kk * a]
    vals[6:8] = [r, kk]
    g = _bdot(_sigmoid(low), wg_ref[...])
    gb = _xdot(r * k * rk_ref[...], hsum_ref[...]) * v * g
    for b in range(nb):
        g_ref[b] = g[b * tt:(b + 1) * tt]
        gb_ref[b] = gb[b * tt:(b + 1) * tt]

    if lat:
        low_half = _iota((nb * tt, LANES), 1) < RW_HD
        for p in range(len(RW_OPS) // 2):
            for hp in range(2):
                first = vals[2 * p][:, hp * LANES:(hp + 1) * LANES]
                second = vals[2 * p + 1][:, hp * LANES:(hp + 1) * LANES]
                xs[p, 2 * hp] = jnp.where(low_half, first, pltpu.roll(second, RW_HD, 1))
                xs[p, 2 * hp + 1] = jnp.where(low_half, pltpu.roll(first, RW_HD, 1), second)

        def relayout(t, carry):
            for p in range(len(RW_OPS) // 2):
                tiles = [jnp.broadcast_to(xs[p, h, pl.ds(b * tt + t, 1), :], (SUBLANES, LANES))
                         for b in range(nb) for h in range(RW_HEADS)]
                both = jnp.concatenate(tiles, axis=0).T
                ops_ref[2 * p, t] = both[:RW_HD]
                ops_ref[2 * p + 1, t] = both[RW_HD:]
            return carry

        lax.fori_loop(0, tt, relayout, 0)
        grp = LANES // nb
        for vi in range(SUBLANES):
            vv_ref[vi] = jnp.concatenate([v[b * tt:(b + 1) * tt, vi * grp:(vi + 1) * grp] for b in range(nb)], axis=1)
    else:
        for j, x in enumerate(vals + [v]):
            xs[j, 0] = x[:, :LANES]
            xs[j, 1] = x[:, LANES:]
        for t in range(tt):
            for j in range(len(RW_OPS) + 1):
                tiles = []
                for hp in range(2):
                    toks = xs[j, hp, pl.ds(t, nb, stride=tt), :]
                    tiles += [toks[:, hh * RW_HD:(hh + 1) * RW_HD] for hh in range(2)]
                ops_ref[j, t] = jnp.concatenate(tiles, axis=0).T


def _rw_prep(lat, u_rw, mu, wg, ww, wa, w0, a0, k_k, k_a, r_k, hsum):
    nb, seq_len, _ = u_rw.shape
    tt = (LANES // nb) if lat else SUBLANES
    n_tiles = seq_len // tt
    n_ops = len(RW_OPS) + (0 if lat else 1)
    sub = tt // SUBLANES
    last = seq_len // SUBLANES - 1
    full = lambda a: pl.BlockSpec(a.shape, lambda i: (0,) * a.ndim)
    tok_spec = pl.BlockSpec((nb, tt, RW_W), lambda i: (0, i, 0))
    tok_sds = jax.ShapeDtypeStruct((nb, seq_len, RW_W), F32)
    out_specs = [pl.BlockSpec((n_ops, tt, RW_HD, LANES), lambda i: (0, i, 0, 0))]
    out_shape = [jax.ShapeDtypeStruct((n_ops, seq_len, RW_HD, LANES), F32)]
    if lat:
        out_specs.append(pl.BlockSpec((SUBLANES, tt, LANES), lambda i: (0, i, 0)))
        out_shape.append(jax.ShapeDtypeStruct((SUBLANES, seq_len, LANES), F32))
    return pl.pallas_call(
        functools.partial(_rw_prep_kernel, lat, nb, tt, n_tiles),
        grid=(n_tiles,),
        in_specs=[
            pl.BlockSpec((nb, tt, RW_COLS), lambda i: (0, i, 0)),
            pl.BlockSpec((nb, SUBLANES, RW_COLS), lambda i: (0, jnp.maximum(i * sub - 1, 0), 0)),
            pl.BlockSpec((nb, SUBLANES, RW_COLS), lambda i: (0, jnp.minimum((i + 1) * sub, last), 0)),
        ] + [full(a) for a in (mu, wg, ww, wa, w0, a0, k_k, k_a, r_k, hsum)],
        out_specs=out_specs + [tok_spec, tok_spec],
        out_shape=out_shape + [tok_sds, tok_sds],
        scratch_shapes=[pltpu.VMEM((n_ops // 2, RW_HEADS, nb * tt, LANES) if lat else (n_ops, 2, nb * tt, LANES), F32)],
        compiler_params=_params(("arbitrary",)),
        name="rw_prep_lat" if lat else "rw_prep_ctx",
    )(u_rw, u_rw, u_rw, mu, wg, ww, wa, w0, a0, k_k, k_a, r_k, hsum)


def _allsum_sublanes(x):
    x = x + pltpu.roll(x, 4, 0)
    x = x + pltpu.roll(x, 2, 0)
    return x + pltpu.roll(x, 1, 0)


def _rw_scan_kernel(lat, nv, n_chunks, *refs):
    if lat:
        af_ref, ab_ref, sf_ref, sb_ref, vf_ref, vb_ref, s0_ref, yf_ref, yb_ref, s_scr = refs
    else:
        af_ref, ab_ref, sf_ref, sb_ref, yf_ref, yb_ref, sfin_ref, s_scr = refs
        vf_ref = vb_ref = None
    c = pl.program_id(0)

    @pl.when(c == 0)
    def _():
        s_scr[...] = s0_ref[...] if lat else jnp.zeros(s_scr.shape, F32)

    sub = _iota((SUBLANES, LANES), 0)
    nkb = RW_HD // SUBLANES
    bcast = lambda x: jnp.broadcast_to(x, (SUBLANES, LANES))

    def step(tt, carry):
        dirs = ((af_ref, sf_ref, vf_ref, yf_ref, tt), (ab_ref, sb_ref, vb_ref, yb_ref, SCAN_CHUNK - 1 - tt))
        for d, (a_ref, s_ref, v_ref, y_ref, tl) in enumerate(dirs):
            keys = lambda ref, j, kb: ref[j, tl, pl.ds(kb * SUBLANES, SUBLANES), :]

            def value_block(vb, carry2):
                sa = [None] * SUBLANES
                for kb in range(nkb):
                    kk = keys(s_ref, 1, kb)
                    for vi in range(SUBLANES):
                        p = s_scr[d, vb, kb, vi] * kk
                        sa[vi] = p if kb == 0 else sa[vi] + p
                sa = [_allsum_sublanes(x) for x in sa]
                if lat:
                    vrow = [bcast(v_ref[vi, pl.ds(tl, 1), :]) for vi in range(SUBLANES)]
                else:
                    vrow = [bcast(s_ref[2, tl, pl.ds(vb * SUBLANES + vi, 1), :]) for vi in range(SUBLANES)]
                acc = [None] * SUBLANES
                for kb in range(nkb):
                    w, kd, bb, r = keys(a_ref, 0, kb), keys(a_ref, 1, kb), keys(a_ref, 2, kb), keys(s_ref, 0, kb)
                    for vi in range(SUBLANES):
                        sn = s_scr[d, vb, kb, vi] * w - sa[vi] * bb + vrow[vi] * kd
                        s_scr[d, vb, kb, vi] = sn
                        acc[vi] = sn * r if kb == 0 else acc[vi] + sn * r
                if lat:
                    for vi in range(SUBLANES):
                        y_ref[vi, pl.ds(tl, 1), :] = _allsum_sublanes(acc[vi])[0:1, :]
                else:
                    y = jnp.zeros((SUBLANES, LANES), F32)
                    for vi in range(SUBLANES):
                        y = jnp.where(sub == vi, _allsum_sublanes(acc[vi]), y)
                    y_ref[tl, pl.ds(pl.multiple_of(vb * SUBLANES, SUBLANES), SUBLANES), :] = y
                return carry2

            if nv == 1:
                value_block(0, 0)
            else:
                lax.fori_loop(0, nv, value_block, 0)
        return carry

    lax.fori_loop(0, SCAN_CHUNK, step, 0)

    if not lat:
        @pl.when(c == n_chunks - 1)
        def _():
            sfin_ref[...] = s_scr[...]


def _rw_scan(lat, ops, vv, s0):
    seq_len = ops.shape[1]
    nc = seq_len // SCAN_CHUNK
    nv = 1 if lat else RW_HD // SUBLANES
    n_shared = 2 if lat else 3
    op_spec = lambda n, blk, rev: pl.BlockSpec(
        (n, SCAN_CHUNK, RW_HD, LANES), lambda c: (blk, (nc - 1 - c) if rev else c, 0, 0))
    s_shape = (2, nv, SUBLANES, SUBLANES, SUBLANES, LANES)
    in_specs = [op_spec(3, 0, False), op_spec(3, 1, True),
                op_spec(n_shared, 6 // n_shared, False), op_spec(n_shared, 6 // n_shared, True)]
    args = [ops, ops, ops, ops]
    if lat:
        y_spec = lambda rev: pl.BlockSpec((SUBLANES, SCAN_CHUNK, LANES), lambda c: (0, (nc - 1 - c) if rev else c, 0))
        y_sds = jax.ShapeDtypeStruct((SUBLANES, seq_len, LANES), F32)
        in_specs += [y_spec(False), y_spec(True), pl.BlockSpec(s_shape, lambda c: (0,) * 6)]
        args += [vv, vv, s0]
        out_specs = [y_spec(False), y_spec(True)]
        out_shape = [y_sds, y_sds]
    else:
        y_spec = lambda rev: pl.BlockSpec((SCAN_CHUNK, RW_HD, LANES), lambda c: ((nc - 1 - c) if rev else c, 0, 0))
        y_sds = jax.ShapeDtypeStruct((seq_len, RW_HD, LANES), F32)
        out_specs = [y_spec(False), y_spec(True), pl.BlockSpec(s_shape, lambda c: (0,) * 6)]
        out_shape = [y_sds, y_sds, jax.ShapeDtypeStruct(s_shape, F32)]
    return pl.pallas_call(
        functools.partial(_rw_scan_kernel, lat, nv, nc),
        grid=(nc,),
        in_specs=in_specs,
        out_specs=out_specs,
        out_shape=out_shape,
        scratch_shapes=[pltpu.VMEM(s_shape, F32)],
        compiler_params=_params(("arbitrary",)),
        name="rw_scan_lat" if lat else "rw_scan_ctx",
    )(*args)


def _rw_post_ctx_kernel(nb, tt, yf_ref, yb_ref, g_ref, gb_ref, lnw_ref, lnb_ref, o_ref):
    for t in range(tt):
        y = yf_ref[t] + yb_ref[t]
        dev = y - jnp.mean(y, axis=0, keepdims=True)
        yn = (dev * lax.rsqrt(jnp.mean(dev * dev, axis=0, keepdims=True) + RW_GN_EPS)).T
        tok = jnp.concatenate([yn[h * nb:(h + 1) * nb, :] for h in range(RW_HEADS)], axis=1)
        o_ref[:, t, :] = (tok * lnw_ref[...] + lnb_ref[...]) * g_ref[:, t, :] + gb_ref[:, t, :]


def _rw_post_lat_kernel(nb, yf_ref, yb_ref, g_ref, gb_ref, lnw_ref, lnb_ref, hs_ref, o_ref):
    grp = LANES // nb
    ys = [yf_ref[vi] + yb_ref[vi] for vi in range(SUBLANES)]
    for b in range(nb):
        y = jnp.concatenate([yv[:, b * grp:(b + 1) * grp] for yv in ys], axis=1)
        dev = y - _xdot(y, hs_ref[...]) * (1.0 / RW_HD)
        yn = dev * lax.rsqrt(_xdot(dev * dev, hs_ref[...]) * (1.0 / RW_HD) + RW_GN_EPS)
        o_ref[b] = (yn * lnw_ref[...] + lnb_ref[...]) * g_ref[b] + gb_ref[b]


def _rw_post(lat, yf, yb, g, gb, lnw, lnb, hs):
    nb, seq_len, _ = g.shape
    tt = LANES if lat else SUBLANES
    full = lambda a: pl.BlockSpec(a.shape, lambda i: (0,) * a.ndim)
    tok_spec = pl.BlockSpec((nb, tt, RW_W), lambda i: (0, i, 0))
    if lat:
        y_spec = pl.BlockSpec((SUBLANES, tt, LANES), lambda i: (0, i, 0))
        body, extra = functools.partial(_rw_post_lat_kernel, nb), [hs]
    else:
        y_spec = pl.BlockSpec((tt, RW_HD, LANES), lambda i: (i, 0, 0))
        body, extra = functools.partial(_rw_post_ctx_kernel, nb, tt), []
    return pl.pallas_call(
        body,
        grid=(seq_len // tt,),
        in_specs=[y_spec, y_spec, tok_spec, tok_spec, full(lnw), full(lnb)] + [full(a) for a in extra],
        out_specs=tok_spec,
        out_shape=jax.ShapeDtypeStruct((nb, seq_len, RW_W), F32),
        compiler_params=_params(("parallel",)),
        name="rw_post_lat" if lat else "rw_post_ctx",
    )(yf, yb, g, gb, lnw, lnb, *extra)


def _rw_mixer(lat, u_rw, s0, prep_params, lnw, lnb, hs):
    nb = u_rw.shape[0]
    nkb = RW_HD // SUBLANES
    if lat:
        ops, vv, g, gb = _rw_prep(True, u_rw, *prep_params)
        grp = LANES // (nb * RW_HEADS)
        s0 = s0.reshape(nb, 2, RW_HEADS, grp, SUBLANES, nkb, SUBLANES)
        s0 = jnp.transpose(s0, (1, 5, 4, 6, 0, 2, 3)).reshape(2, 1, nkb, SUBLANES, SUBLANES, LANES)
        yf, yb = _rw_scan(True, ops, vv, s0)
        return _rw_post(True, yf, yb, g, gb, lnw, lnb, hs), None
    ops, g, gb = _rw_prep(False, u_rw, *prep_params)
    yf, yb, sfin = _rw_scan(False, ops, None, None)
    sfin = sfin.reshape(2, nkb, nkb, SUBLANES, SUBLANES, RW_HEADS, nb)
    sfin = jnp.transpose(sfin, (6, 0, 5, 1, 3, 2, 4)).reshape(nb, 2, RW_HEADS, RW_HD, RW_HD)
    return _rw_post(False, yf, yb, g, gb, lnw, lnb, hs), sfin


def _softplus(x):
    return jnp.maximum(x, 0.0) + jnp.log1p(jnp.exp(-jnp.abs(x)))


def _ssd_kernel(reverse, n_chunks, u_ref, up_ref, un_ref, h0_ref, cw_ref, cb_ref, dtb_ref, a_ref, *rest):
    if reverse:
        yf_ref, nw_ref, out_ref, hfin_ref, h_scr = rest
    else:
        dsk_ref, out_ref, hfin_ref, h_scr = rest
    s = pl.program_id(1)
    c = (n_chunks - 1 - s) if reverse else s

    @pl.when(s == 0)
    def _():
        h_scr[...] = h0_ref[...]

    q = SSD_CHUNK
    u = u_ref[...]
    xbc = u[:, SSD_W:SSD_W + 512]
    row = _iota(xbc.shape, 0)
    has_prev = (c > 0).astype(F32)
    has_next = (c < n_chunks - 1).astype(F32)
    prev = jnp.where(row == 0, up_ref[SUBLANES - 1:SUBLANES, SSD_W:SSD_W + 512] * has_prev, pltpu.roll(xbc, 1, 0))
    nxt = jnp.where(row == q - 1, un_ref[0:1, SSD_W:SSD_W + 512] * has_next, pltpu.roll(xbc, q - 1, 0))
    xc = _silu(cw_ref[0:1, :] * prev + cw_ref[1:2, :] * xbc + cw_ref[2:3, :] * nxt + cb_ref[...])
    x = xc[:, :SSD_W]
    b_all = xc[:, SSD_W:SSD_W + LANES]
    c_all = xc[:, SSD_W + LANES:]

    dt = _softplus(u[:, SSD_W + 512:] + dtb_ref[...])
    dta = dt * a_ref[...]
    ii = _iota((q, q), 0)
    jj = _iota((q, q), 1)
    tri = (jj >= ii) if reverse else (jj <= ii)
    cs = _xdot(tri.astype(F32), dta)
    cs_t = cs.T
    expand = ((_iota((LANES, SSD_W), 1) >> 6) == _iota((LANES, SSD_W), 0)).astype(F32)
    dt_e = _xdot(dt, expand)
    cs_e = _xdot(cs, expand)
    tot = cs_e[0:1, :] if reverse else cs_e[q - 1:q, :]
    xdt = x * dt_e
    dec_in = jnp.exp(tot - cs_e)
    dec_out = jnp.exp(cs_e)
    cdec = jnp.exp(tot)
    lane = _iota((q, LANES), 1)
    rowc = _iota((LANES, 1), 0)
    ys = []
    for g in range(2):
        bg = b_all[:, g * 64:(g + 1) * 64]
        cg = c_all[:, g * 64:(g + 1) * 64]
        gm = _bdot_nt(cg, bg)
        sl = slice(g * LANES, (g + 1) * LANES)
        xdt_g = xdt[:, sl]
        yd = []
        for hh in range(2):
            h = 2 * g + hh
            diff = cs[:, h:h + 1] - cs_t[h:h + 1, :]
            lm = jnp.exp(jnp.where(tri, diff, -jnp.inf))
            yd.append(_bdot(gm * lm, xdt_g))
        hg = h_scr[g]
        y = jnp.where(lane < 64, yd[0], yd[1]) + _bdot_nt(cg, hg) * dec_out[:, sl]
        st = _bdot((xdt_g * dec_in[:, sl]).T, bg)
        cd = cdec[:, sl]
        h_new = hg * jnp.where(rowc < 64, cd[:, 0:1], cd[:, 64:65]) + st
        h_scr[g] = h_new
        hfin_ref[g] = h_new
        ys.append(y)
    y = jnp.concatenate(ys, axis=1)
    if reverse:
        val = (yf_ref[...] + y) * _silu(u[:, :SSD_W])
        out_ref[...] = _rms(val, nw_ref[...])
    else:
        out_ref[...] = y + dsk_ref[...] * x


def _ssd_sweep(reverse, u_ssd, nb, seq_len, h0, cw, cb, dtb, a_neg, extra):
    nc = seq_len // SSD_CHUNK
    sub_per_chunk = SSD_CHUNK // SUBLANES
    last_sub = u_ssd.shape[0] // SUBLANES - 1
    chunk = (lambda s: nc - 1 - s) if reverse else (lambda s: s)
    rowblk = lambda b, s: b * nc + chunk(s)
    full = lambda a: pl.BlockSpec(a.shape, lambda b, s: (0,) * a.ndim)
    in_specs = [
        pl.BlockSpec((SSD_CHUNK, SSD_PAD), lambda b, s: (rowblk(b, s), 0)),
        pl.BlockSpec((SUBLANES, SSD_PAD), lambda b, s: (jnp.maximum(rowblk(b, s) * sub_per_chunk - 1, 0), 0)),
        pl.BlockSpec((SUBLANES, SSD_PAD),
                     lambda b, s: (jnp.minimum((rowblk(b, s) + 1) * sub_per_chunk, last_sub), 0)),
        pl.BlockSpec((None, 2, LANES, 64), lambda b, s: (b, 0, 0, 0)),
        full(cw), full(cb), full(dtb), full(a_neg),
    ]
    args = [u_ssd, u_ssd, u_ssd, h0, cw, cb, dtb, a_neg]
    if reverse:
        yf, nw = extra
        in_specs += [pl.BlockSpec((SSD_CHUNK, SSD_W), lambda b, s: (rowblk(b, s), 0)), full(nw)]
        args += [yf, nw]
    else:
        in_specs += [full(extra)]
        args += [extra]
    return pl.pallas_call(
        functools.partial(_ssd_kernel, reverse, nc),
        grid=(nb, nc),
        in_specs=in_specs,
        out_specs=[pl.BlockSpec((SSD_CHUNK, SSD_W), lambda b, s: (rowblk(b, s), 0)),
                   pl.BlockSpec((None, 2, LANES, 64), lambda b, s: (b, 0, 0, 0))],
        out_shape=[jax.ShapeDtypeStruct((nb * seq_len, SSD_W), F32),
                   jax.ShapeDtypeStruct((nb, 2, LANES, 64), F32)],
        scratch_shapes=[pltpu.VMEM((2, LANES, 64), F32)],
        compiler_params=_params(("parallel", "arbitrary")),
        name="ssd_bwd" if reverse else "ssd_fwd",
    )(*args)


def _ssd_mixer(u_ssd, nb, seq_len, h0, cw, cb, dtb, a_neg, dsk, nw):
    yf, hf = _ssd_sweep(False, u_ssd, nb, seq_len, h0[:, 0], cw, cb, dtb[0], a_neg[0], dsk)
    out, hb = _ssd_sweep(True, u_ssd, nb, seq_len, h0[:, 1], cw, cb, dtb[1], a_neg[1], (yf, nw))
    return out, jnp.stack([hf, hb], axis=1).reshape(nb, 2, 4, 64, 64)


def _group_peers(x, width, log2_stride):
    stride = 1 << log2_stride
    pos = (_iota(x.shape, 1) >> log2_stride) & 3
    peers = []
    for j in (1, 2, 3):
        ahead = pltpu.roll(x, width - j * stride, 1)
        behind = pltpu.roll(x, (4 - j) * stride, 1)
        peers.append((jnp.where(pos + j < 4, ahead, behind), pos + j >= 4))
    return peers


def _route(scores, bias):
    sel = scores + bias
    rank = jnp.zeros(sel.shape, jnp.int32)
    as_int = lambda cond: jnp.where(cond, 1, 0)
    for other, wrapped in _group_peers(sel, LANES, 0):
        rank = rank + jnp.where(wrapped, as_int(other >= sel), as_int(other > sel))
    top2 = rank < 2
    m = jnp.where(top2, sel, 0.0)
    grp = m
    for other, _ in _group_peers(m, LANES, 0):
        grp = grp + other
    best = jnp.ones(sel.shape, jnp.int32)
    for other, wrapped in _group_peers(grp, LANES, 2):
        loses = jnp.where(wrapped, as_int(other >= grp), as_int(other > grp))
        best = best * (1 - loses)
    chosen = jnp.where(top2, best, 0) > 0
    cw = jnp.where(chosen, scores, 0.0)
    den = cw
    for other, _ in _group_peers(cw, LANES, 0):
        den = den + other
    return jnp.where(chosen, cw / jnp.where(chosen, den, 1.0), 0.0)


def _outproj_kernel(oatt_ref, orw_ref, ossd_ref, x_ref, mod_ref, wout_ref,
                    n2_ref, rw_ref, rb_ref, x1_ref, h2_ref, comb_ref):
    o = jnp.concatenate([oatt_ref[...], orw_ref[...], ossd_ref[...]], axis=1)
    x1 = x_ref[...] + mod_ref[2] * _bdot(o, wout_ref[...])
    h2 = _rms(x1, n2_ref[...]) * (1.0 + mod_ref[4]) + mod_ref[3]
    scores = _sigmoid(_bdot(h2, rw_ref[...]))
    x1_ref[...] = x1
    h2_ref[...] = h2.astype(BF16)
    comb_ref[...] = _route(scores, rb_ref[...])


def _outproj(o_att, o_rw, o_ssd, x, mod_l, row_fn, w_out_bf, n2, router_w, router_b):
    n_tok = x.shape[0]
    tile = lambda w: pl.BlockSpec((TOKEN_TILE, w), lambda i: (i, 0))
    full = lambda a: pl.BlockSpec(a.shape, lambda i: (0,) * a.ndim)
    return pl.pallas_call(
        _outproj_kernel,
        grid=(n_tok // TOKEN_TILE,),
        in_specs=[tile(ATT_W), tile(RW_W), tile(SSD_W), tile(D_MODEL),
                  _mod_spec(row_fn)] + [full(a) for a in (w_out_bf, n2, router_w, router_b)],
        out_specs=[tile(D_MODEL), tile(D_MODEL), tile(LANES)],
        out_shape=[jax.ShapeDtypeStruct((n_tok, D_MODEL), F32),
                   jax.ShapeDtypeStruct((n_tok, D_MODEL), BF16),
                   jax.ShapeDtypeStruct((n_tok, LANES), F32)],
        compiler_params=_params(("parallel",)),
        name="outproj",
    )(o_att, o_rw, o_ssd, x, mod_l, w_out_bf, n2, router_w, router_b)


def _moe_kernel(h2_ref, comb_ref, x1_ref, mod_ref, wg_ref, wu_ref, wd_ref, o_ref, acc_ref):
    e = pl.program_id(1)

    @pl.when(e == 0)
    def _():
        acc_ref[...] = jnp.zeros_like(acc_ref)

    h2 = h2_ref[...]
    hid = _silu(jnp.dot(h2, wg_ref[...], preferred_element_type=F32)) * jnp.dot(
        h2, wu_ref[...], preferred_element_type=F32)
    comb = comb_ref[...]
    wcol = jnp.sum(jnp.where(_iota(comb.shape, 1) == e, comb, 0.0), axis=1, keepdims=True)
    acc_ref[...] += _bdot(hid * wcol, wd_ref[...])

    @pl.when(e == N_EXPERTS - 1)
    def _():
        o_ref[...] = x1_ref[...] + mod_ref[5] * acc_ref[...]


def _moe(h2, comb, x1, mod_l, row_fn, wg_bf, wu_bf, wd_bf):
    n_tok = x1.shape[0]
    tile = lambda w: pl.BlockSpec((MOE_TILE, w), lambda i, e: (i, 0))
    return pl.pallas_call(
        _moe_kernel,
        grid=(n_tok // MOE_TILE, N_EXPERTS),
        in_specs=[tile(D_MODEL), tile(LANES), tile(D_MODEL), _mod_spec(row_fn),
                  pl.BlockSpec((None, D_MODEL, D_FF), lambda i, e: (e, 0, 0)),
                  pl.BlockSpec((None, D_MODEL, D_FF), lambda i, e: (e, 0, 0)),
                  pl.BlockSpec((None, D_FF, D_MODEL), lambda i, e: (e, 0, 0))],
        out_specs=tile(D_MODEL),
        out_shape=jax.ShapeDtypeStruct((n_tok, D_MODEL), F32),
        scratch_shapes=[pltpu.VMEM((MOE_TILE, D_MODEL), F32)],
        compiler_params=_params(("parallel", "arbitrary")),
        name="moe",
    )(h2, comb, x1, mod_l, wg_bf, wu_bf, wd_bf)


def _rope_tables(seq_len):
    t = np.arange(seq_len)
    pos = np.stack([t // GRID_W, t % GRID_W], 0).astype(np.float32)
    n_freq = HEAD_DIM // 4
    inv = jnp.asarray(ROPE_BASE, F32) ** (-jnp.arange(n_freq, dtype=F32) / n_freq)
    ang = jnp.asarray(pos)[:, :, None] * inv
    cos, sin = jnp.cos(ang), jnp.sin(ang)
    cos_h = jnp.concatenate([cos[0], cos[0], cos[1], cos[1]], -1)
    sin_h = jnp.concatenate([-sin[0], sin[0], -sin[1], sin[1]], -1)
    return jnp.tile(cos_h, (1, 2)), jnp.tile(sin_h, (1, 2))


def _pad_rows(w, start, total):
    return jnp.pad(w, ((0, 0),) * (w.ndim - 2) + ((start, total - start - w.shape[-2]), (0, 0)))


def _value_order(x, axis, start, grp):
    if grp is None:
        return x
    axis %= x.ndim
    sl = lambda a, b: lax.slice_in_dim(x, a, b, axis=axis)
    mid = sl(start, start + RW_W)
    shape = mid.shape
    mid = mid.reshape(shape[:axis] + (RW_HEADS, grp, RW_HD // grp) + shape[axis + 1:])
    order = tuple(range(axis)) + (axis + 2, axis, axis + 1) + tuple(range(axis + 3, mid.ndim))
    mid = jnp.transpose(mid, order).reshape(shape)
    return jnp.concatenate([sl(0, start), mid, sl(start + RW_W, x.shape[axis])], axis=axis)


def _rw_layer_params(l, P, grp):
    row = lambda a: a.reshape(1, -1)
    v0 = 2 * RW_W
    w_in = jnp.pad(P["w_in"][l], ((0, 0), (0, SSD_PAD - SSD_COLS))).astype(BF16)
    head = np.arange(RW_W) // RW_HD if grp is None else (np.arange(RW_W) // grp) % RW_HEADS
    rp = {}
    rp["w_in"] = _value_order(w_in, 1, ATT_COLS + v0, grp)
    rp["w_out"] = _value_order(P["w_out"][l].astype(BF16), 0, ATT_W, grp)
    rp["prep"] = (
        _value_order(P["rw_mu"][l], 1, v0, grp),
        _pad_rows(_value_order(P["rw_g_up"][l], 1, 0, grp), 64, LANES),
        _pad_rows(P["rw_w_up"][l], 0, LANES),
        _pad_rows(P["rw_a_up"][l], 32, LANES),
        P["rw_w0"][l], P["rw_a0"][l],
        row(P["rw_k_k"][l]), row(P["rw_k_a"][l]), row(P["rw_r_k"][l]),
        jnp.asarray((np.arange(RW_W)[:, None] // RW_HD == head[None, :]).astype(np.float32)),
    )
    rp["lnw"] = _value_order(row(P["rw_ln_w"][l]), 1, 0, grp)
    rp["lnb"] = _value_order(row(P["rw_ln_b"][l]), 1, 0, grp)
    rp["hs"] = jnp.asarray((head[:, None] == head[None, :]).astype(np.float32))
    return rp


def _layer_params(l, P):
    row = lambda a: a.reshape(1, -1)
    lp = {}
    lp["n1"] = row(P["norm1_w"][l])
    lp["n2"] = row(P["norm2_w"][l])
    lp["qw"] = row(jnp.tile(P["q_norm_w"][l], ATT_HEADS))
    lp["kw"] = row(jnp.tile(P["k_norm_w"][l], ATT_KV_HEADS))
    lp["sink"] = P["attn_sink"][l]
    pad_heads = lambda a: jnp.pad(a, ((0, 0), (0, LANES - a.shape[1]))).reshape(2, 1, LANES)
    lp["ssd"] = (
        P["ssd_conv_w"][l], row(P["ssd_conv_b"][l]),
        pad_heads(P["ssd_dt_bias"][l]), pad_heads(-jnp.exp(P["ssd_a_log"][l])),
        row(jnp.repeat(P["ssd_d"][l], 64)), row(P["ssd_norm_w"][l]),
    )
    lp["wg"] = P["exp_gate"][l].astype(BF16)
    lp["wu"] = P["exp_up"][l].astype(BF16)
    lp["wd"] = P["exp_down"][l].astype(BF16)
    return lp


def _trunk(x, nb, seq_len, is_ctx, mod, layers, rw_layers, router, cache_k, cache_v, state_rwkv, state_ssd):
    tiles_per_seq = seq_len // TOKEN_TILE
    moe_per_seq = max(seq_len // MOE_TILE, 1)
    if is_ctx:
        row_tok = lambda i: 0
        row_moe = lambda i: 0
        rope = None
    else:
        row_tok = lambda i: 1 + i // tiles_per_seq
        row_moe = lambda i: 1 + i // moe_per_seq
        rope = _rope_tables(seq_len)
    new_k, new_v, new_rw, new_ssd = [], [], [], []
    for l, (lp, rp) in enumerate(zip(layers, rw_layers)):
        q, k, v, u_rw, u_ssd = _inproj(x, mod[l], row_tok, lp["n1"], rp["w_in"], lp["qw"], lp["kw"], rope, seq_len)
        if is_ctx:
            o_att = _attn_ctx(lp["sink"], q, k, v, nb, seq_len)
            rw0 = None
            ssd0 = jnp.zeros((nb, 2, 2, LANES, 64), F32)
        else:
            kc = cache_k[:, l].reshape(nb, -1, KV_W)
            vc = cache_v[:, l].reshape(nb, -1, KV_W)
            o_att = _attn_lat(lp["sink"], q, k, v, kc, vc, nb, seq_len)
            rw0 = state_rwkv[:, l]
            ssd0 = state_ssd[:, l].reshape(nb, 2, 2, LANES, 64)
        o_rw, s_rw = _rw_mixer(not is_ctx, u_rw.reshape(nb, seq_len, RW_COLS), rw0, rp["prep"],
                               rp["lnw"], rp["lnb"], rp["hs"])
        o_ssd, s_ssd = _ssd_mixer(u_ssd, nb, seq_len, ssd0, *lp["ssd"])
        if is_ctx:
            new_k.append(k.reshape(nb, seq_len, ATT_KV_HEADS, HEAD_DIM))
            new_v.append(v.reshape(nb, seq_len, ATT_KV_HEADS, HEAD_DIM))
            new_rw.append(s_rw)
            new_ssd.append(s_ssd)
        x1, h2, comb = _outproj(o_att, o_rw.reshape(nb * seq_len, RW_W), o_ssd, x, mod[l], row_tok, rp["w_out"],
                                lp["n2"], *router)
        x = _moe(h2, comb, x1, mod[l], row_moe, lp["wg"], lp["wu"], lp["wd"])
    if is_ctx:
        return x, tuple(jnp.stack(t, 1) for t in (new_k, new_v, new_rw, new_ssd))
    return x, None


def kernel(x_prompt, x_sample, cache_k, cache_v, state_rwkv, state_ssd, c, c_ctx, w_mod, b_mod, norm1_w, norm2_w, w_in, w_out, q_norm_w, k_norm_w, attn_sink, rw_mu, rw_w0, rw_w_up, rw_a0, rw_a_up, rw_g_up, rw_k_k, rw_k_a, rw_r_k, rw_ln_w, rw_ln_b, ssd_conv_w, ssd_conv_b, ssd_dt_bias, ssd_a_log, ssd_d, ssd_norm_w, router_w, router_bias, exp_gate, exp_up, exp_down):
    P = dict(norm1_w=norm1_w, norm2_w=norm2_w, w_in=w_in, w_out=w_out, q_norm_w=q_norm_w, k_norm_w=k_norm_w,
             attn_sink=attn_sink, rw_mu=rw_mu, rw_w0=rw_w0, rw_w_up=rw_w_up, rw_a0=rw_a0, rw_a_up=rw_a_up,
             rw_g_up=rw_g_up, rw_k_k=rw_k_k, rw_k_a=rw_k_a, rw_r_k=rw_r_k, rw_ln_w=rw_ln_w, rw_ln_b=rw_ln_b,
             ssd_conv_w=ssd_conv_w, ssd_conv_b=ssd_conv_b, ssd_dt_bias=ssd_dt_bias, ssd_a_log=ssd_a_log,
             ssd_d=ssd_d, ssd_norm_w=ssd_norm_w, exp_gate=exp_gate, exp_up=exp_up, exp_down=exp_down)
    nb_ctx, seq_ctx, _ = x_prompt.shape
    nb_lat, seq_lat, _ = x_sample.shape
    assert nb_lat + 1 <= SUBLANES and seq_ctx % TOKEN_TILE == 0 and seq_lat % MOE_TILE == 0
    assert nb_ctx * RW_HEADS == LANES and LANES % (nb_lat * RW_HEADS) == 0
    cond8 = jnp.zeros((SUBLANES, D_MODEL), F32).at[0].set(c_ctx).at[1:1 + nb_lat].set(c)
    mod = _modulation(cond8, w_mod, b_mod)
    layers = [_layer_params(l, P) for l in range(DEPTH)]
    rw_ctx = [_rw_layer_params(l, P, None) for l in range(DEPTH)]
    rw_lat = [_rw_layer_params(l, P, LANES // (nb_lat * RW_HEADS)) for l in range(DEPTH)]
    router = (jnp.pad(router_w, ((0, 0), (0, LANES - N_EXPERTS))).astype(BF16),
              jnp.pad(router_bias, (0, LANES - N_EXPERTS)).reshape(1, LANES))
    y_prompt, ctx_state = _trunk(x_prompt.reshape(-1, D_MODEL), nb_ctx, seq_ctx, True, mod, layers, rw_ctx, router,
                                 None, None, None, None)
    y_sample, _ = _trunk(x_sample.reshape(-1, D_MODEL), nb_lat, seq_lat, False, mod, layers, rw_lat, router,
                         cache_k, cache_v, state_rwkv, state_ssd)
    return (y_prompt.reshape(x_prompt.shape), y_sample.reshape(x_sample.shape)) + ctx_state
```

```python
import functools
import math

import jax
import jax.numpy as jnp
import numpy as np
from jax import lax
from jax.experimental import pallas as pl
from jax.experimental.pallas import tpu as pltpu

F32 = jnp.float32
BF16 = jnp.bfloat16
HIGHEST = lax.Precision.HIGHEST

D_MODEL = 1024
DEPTH = 2
GRID_W = 64
EPS = 1e-6
ATT_HEADS = 8
ATT_KV_HEADS = 2
HEAD_DIM = 64
ATT_W = 512
KV_W = 128
ATT_COLS = 768
WINDOW = 128
BLK = 128
ROPE_BASE = 10000.0
RW_HEADS = 4
RW_HD = 64
RW_W = 256
RW_COLS = 896
RW_DECAY_SCALE = 0.606531
RW_GN_EPS = 64e-5
SSD_W = 256
SSD_CHUNK = 128
SSD_COLS = 772
SSD_PAD = 896
IN_PAD = ATT_COLS + RW_COLS + SSD_PAD
N_EXPERTS = 16
EXP_PER_GROUP = 4
D_FF = 256

LANES = 128
SUBLANES = 8
TOKEN_TILE = 256
MOE_TILE = 1024
SCAN_CHUNK = 16
VMEM_LIMIT = 48 * 1024 * 1024


def _params(sem):
    return pltpu.CompilerParams(dimension_semantics=sem, vmem_limit_bytes=VMEM_LIMIT)


def _bdot(a, b):
    return jnp.dot(a.astype(BF16), b.astype(BF16), preferred_element_type=F32)


def _bdot_nt(a, b):
    return lax.dot_general(a.astype(BF16), b.astype(BF16), (((1,), (1,)), ((), ())),
                           preferred_element_type=F32)


def _sel_dot(x, sel, terms, sel_first=False):
    acc = None
    rem = x
    for i in range(terms):
        piece = rem.astype(BF16)
        part = jnp.dot(sel, piece, preferred_element_type=F32) if sel_first else jnp.dot(
            piece, sel, preferred_element_type=F32)
        acc = part if acc is None else acc + part
        if i + 1 < terms:
            rem = rem - piece.astype(F32)
    return acc


def _iota(shape, axis):
    return lax.broadcasted_iota(jnp.int32, shape, axis)


def _head_selector(heads_in, heads_out=None):
    heads_out = heads_in if heads_out is None else heads_out
    return jnp.asarray(np.asarray(heads_in)[:, None] == np.asarray(heads_out)[None, :], BF16)


NORM_TERMS = 2
EXACT_TERMS = 3


def _sigmoid(x):
    return 1.0 / (1.0 + jnp.exp(-x))


def _silu(x):
    return x * _sigmoid(x)


def _rms(x, w):
    return x * lax.rsqrt(jnp.mean(x * x, -1, keepdims=True) + EPS) * w


def _mod_kernel(c_ref, w_ref, b_ref, o_ref):
    o_ref[...] = _bdot(_silu(c_ref[...]), w_ref[...]) + b_ref[...]


def _modulation(cond8, w_mod, b_mod):
    out = pl.pallas_call(
        _mod_kernel,
        grid=(DEPTH, 6),
        in_specs=[
            pl.BlockSpec((SUBLANES, D_MODEL), lambda l, j: (0, 0)),
            pl.BlockSpec((None, D_MODEL, D_MODEL), lambda l, j: (l, 0, j)),
            pl.BlockSpec((None, 1, D_MODEL), lambda l, j: (l, 0, j)),
        ],
        out_specs=pl.BlockSpec((None, None, SUBLANES, D_MODEL), lambda l, j: (l, j, 0, 0)),
        out_shape=jax.ShapeDtypeStruct((DEPTH, 6, SUBLANES, D_MODEL), F32),
        compiler_params=_params(("arbitrary", "arbitrary")),
        name="modulation",
    )(cond8, w_mod, b_mod.reshape(DEPTH, 1, 6 * D_MODEL))
    return out.reshape(DEPTH, 6, SUBLANES, 1, D_MODEL)


def _mod_spec(row_fn):
    return pl.BlockSpec((6, None, 1, D_MODEL), lambda *idx: (0, row_fn(idx[0]), 0, 0))


def _swap16(x):
    w = x.shape[1]
    first = (_iota(x.shape, 1) & 31) < 16
    return jnp.where(first, pltpu.roll(x, w - 16, 1), pltpu.roll(x, 16, 1))


def _inproj_kernel(use_rope, x_ref, mod_ref, n1_ref, w_ref, qw_ref, kw_ref, hq_ref, *rest):
    if use_rope:
        cos_ref, sin_ref, q_ref, k_ref, v_ref, urw_ref, ussd_ref = rest
    else:
        q_ref, k_ref, v_ref, urw_ref, ussd_ref = rest
    h = _rms(x_ref[...], n1_ref[...]) * (1.0 + mod_ref[1]) + mod_ref[0]
    u = _bdot(h, w_ref[...])
    q = u[:, :ATT_W]
    k = u[:, ATT_W:ATT_W + KV_W]
    q = q * lax.rsqrt(_sel_dot(q * q, hq_ref[...], NORM_TERMS) * (1.0 / HEAD_DIM) + EPS) * qw_ref[...]
    k = k * lax.rsqrt(_sel_dot(k * k, hq_ref[:KV_W, :KV_W], NORM_TERMS) * (1.0 / HEAD_DIM) + EPS) * kw_ref[...]
    if use_rope:
        cos = cos_ref[...]
        sin = sin_ref[...]
        k = k * cos + _swap16(k) * sin
        cos4 = jnp.concatenate([cos] * 4, axis=1)
        sin4 = jnp.concatenate([sin] * 4, axis=1)
        q = q * cos4 + _swap16(q) * sin4
    q_ref[...] = q
    k_ref[...] = k
    v_ref[...] = u[:, ATT_W + KV_W:ATT_COLS]
    urw_ref[...] = u[:, ATT_COLS:ATT_COLS + RW_COLS]
    ussd_ref[...] = u[:, ATT_COLS + RW_COLS:]


def _inproj(x, mod_l, row_fn, n1, w_in_bf, qw, kw, rope, seq_len):
    n_tok = x.shape[0]
    tiles_per_seq = seq_len // TOKEN_TILE
    in_specs = [
        pl.BlockSpec((TOKEN_TILE, D_MODEL), lambda i: (i, 0)),
        _mod_spec(row_fn),
        pl.BlockSpec((1, D_MODEL), lambda i: (0, 0)),
        pl.BlockSpec((D_MODEL, IN_PAD), lambda i: (0, 0)),
        pl.BlockSpec((1, ATT_W), lambda i: (0, 0)),
        pl.BlockSpec((1, KV_W), lambda i: (0, 0)),
        pl.BlockSpec((ATT_W, ATT_W), lambda i: (0, 0)),
    ]
    args = [x, mod_l, n1, w_in_bf, qw, kw, _head_selector(np.arange(ATT_W) // HEAD_DIM)]
    if rope is not None:
        in_specs += [pl.BlockSpec((TOKEN_TILE, LANES), lambda i: (i % tiles_per_seq, 0))] * 2
        args += list(rope)
    widths = (ATT_W, KV_W, KV_W, RW_COLS, SSD_PAD)
    return pl.pallas_call(
        functools.partial(_inproj_kernel, rope is not None),
        grid=(n_tok // TOKEN_TILE,),
        in_specs=in_specs,
        out_specs=[pl.BlockSpec((TOKEN_TILE, w), lambda i: (i, 0)) for w in widths],
        out_shape=[jax.ShapeDtypeStruct((n_tok, w), F32) for w in widths],
        compiler_params=_params(("parallel",)),
        name="inproj",
    )(*args)


LOG2E = 1.4426950408889634


def _attend_heads(sink_ref, q, parts):
    groups = ATT_HEADS // ATT_KV_HEADS
    outs = []
    for hk in range(ATT_KV_HEADS):
        cols = slice(hk * HEAD_DIM, (hk + 1) * HEAD_DIM)
        keys = [k[:, cols].astype(BF16) for k, _, _ in parts]
        vals = [v[:, cols].astype(BF16) for _, v, _ in parts]
        for g in range(groups):
            hd = hk * groups + g
            qh = q[:, hd * HEAD_DIM:(hd + 1) * HEAD_DIM].astype(BF16)
            sink = sink_ref[hd] * LOG2E
            scores = []
            for kh, (_, _, mask) in zip(keys, parts):
                s = lax.dot_general(qh, kh, (((1,), (1,)), ((), ())),
                                    preferred_element_type=F32) * (HEAD_DIM ** -0.5 * LOG2E)
                scores.append(s if mask is None else jnp.where(mask, s, -1e30))
            m = jnp.max(scores[0], -1, keepdims=True)
            for s in scores[1:]:
                m = jnp.maximum(m, jnp.max(s, -1, keepdims=True))
            m = jnp.maximum(m, sink)
            den = jnp.exp2(sink - m)
            o = None
            for s, vh in zip(scores, vals):
                p = jnp.exp2(s - m)
                den = den + jnp.sum(p, -1, keepdims=True)
                pv = jnp.dot(p.astype(BF16), vh, preferred_element_type=F32)
                o = pv if o is None else o + pv
            outs.append(o / den)
    return jnp.concatenate(outs, axis=1)


def _attn_ctx_kernel(sink_ref, q_ref, k_ref, v_ref, o_ref):
    o_ref[...] = _attend_heads(sink_ref, q_ref[...], [(k_ref[...], v_ref[...], None)])


def _attn_ctx(sink, q, k, v, nb, seq_len):
    return pl.pallas_call(
        _attn_ctx_kernel,
        grid=(nb,),
        in_specs=[
            pl.BlockSpec(memory_space=pltpu.SMEM),
            pl.BlockSpec((seq_len, ATT_W), lambda b: (b, 0)),
            pl.BlockSpec((seq_len, KV_W), lambda b: (b, 0)),
            pl.BlockSpec((seq_len, KV_W), lambda b: (b, 0)),
        ],
        out_specs=pl.BlockSpec((seq_len, ATT_W), lambda b: (b, 0)),
        out_shape=jax.ShapeDtypeStruct((nb * seq_len, ATT_W), F32),
        compiler_params=_params(("parallel",)),
        name="attn_ctx",
    )(sink, q, k, v)


def _attn_lat_kernel(seq_len, sink_ref, q_ref, k_ref, v_ref, kc_ref, vc_ref, o_ref):
    i = pl.program_id(1)
    band = 3 * BLK
    start = pl.multiple_of(jnp.clip((i - 1) * BLK, 0, seq_len - band), BLK)
    dist = jnp.abs(i * BLK + _iota((BLK, band), 0) - (start + _iota((BLK, band), 1)))
    parts = [(k_ref[pl.ds(start, band), :], v_ref[pl.ds(start, band), :], dist <= WINDOW),
             (kc_ref[...], vc_ref[...], None)]
    o_ref[...] = _attend_heads(sink_ref, q_ref[...], parts)


def _attn_lat(sink, q, k, v, kc, vc, nb, seq_len):
    nblk = seq_len // BLK
    past = kc.shape[1]
    return pl.pallas_call(
        functools.partial(_attn_lat_kernel, seq_len),
        grid=(nb, nblk),
        in_specs=[
            pl.BlockSpec(memory_space=pltpu.SMEM),
            pl.BlockSpec((BLK, ATT_W), lambda b, i: (b * nblk + i, 0)),
            pl.BlockSpec((seq_len, KV_W), lambda b, i: (b, 0)),
            pl.BlockSpec((seq_len, KV_W), lambda b, i: (b, 0)),
            pl.BlockSpec((None, past, KV_W), lambda b, i: (b, 0, 0)),
            pl.BlockSpec((None, past, KV_W), lambda b, i: (b, 0, 0)),
        ],
        out_specs=pl.BlockSpec((BLK, ATT_W), lambda b, i: (b * nblk + i, 0)),
        out_shape=jax.ShapeDtypeStruct((nb * seq_len, ATT_W), F32),
        compiler_params=_params(("parallel", "arbitrary")),
        name="attn_lat",
    )(sink, q, k, v, kc, vc)


RW_OPS = ("w_f", "kd_f", "b_f", "w_b", "kd_b", "b_b", "r", "kk")


def _rw_prep_kernel(lat, nb, tt, n_tiles, u_ref, up_ref, un_ref, mu_ref, wg_ref, ww_ref, wa_ref,
                    w0_ref, a0_ref, kk_ref, ka_ref, rk_ref, ones_ref, hsum_ref, *rest):
    if lat:
        ops_ref, vv_ref, g_ref, gb_ref, xs = rest
    else:
        ops_ref, g_ref, gb_ref, xs = rest
    i = pl.program_id(0)
    has_prev = (i > 0).astype(F32)
    has_next = (i < n_tiles - 1).astype(F32)
    row = _iota((tt, RW_COLS), 0)
    mu0 = mu_ref[0:1, :]
    mu1 = mu_ref[1:2, :]
    mixed = []
    for b in range(nb):
        ub = u_ref[b]
        prev = jnp.where(row == 0, up_ref[b, SUBLANES - 1:SUBLANES, :] * has_prev, pltpu.roll(ub, 1, 0))
        nxt = jnp.where(row == tt - 1, un_ref[b, 0:1, :] * has_next, pltpu.roll(ub, tt - 1, 0))
        mixed.append(ub + mu0 * (prev - ub) + mu1 * (nxt - ub))
    u = jnp.concatenate(mixed, axis=0)
    r = u[:, :RW_W]
    k = u[:, RW_W:2 * RW_W]
    v = u[:, 2 * RW_W:3 * RW_W]
    low = u[:, 3 * RW_W:]
    kk = k * kk_ref[...]
    kk = kk * lax.rsqrt(_sel_dot(kk * kk, ones_ref[...], NORM_TERMS) + 1e-12)
    tanh_low = jnp.tanh(low)
    vals = [None] * len(RW_OPS)
    for d in range(2):
        w = jnp.exp(-RW_DECAY_SCALE * _sigmoid(w0_ref[d:d + 1, :] + _bdot(tanh_low, ww_ref[d])))
        a = _sigmoid(a0_ref[d:d + 1, :] + _bdot(low, wa_ref[d]))
        vals[3 * d:3 * d + 3] = [w, k * (1.0 + (a - 1.0) * ka_ref[...]), kk * a]
    vals[6:8] = [r, kk]
    g = _bdot(_sigmoid(low), wg_ref[...])
    gb = _sel_dot(r * k * rk_ref[...], hsum_ref[...], EXACT_TERMS) * v * g
    for b in range(nb):
        g_ref[b] = g[b * tt:(b + 1) * tt]
        gb_ref[b] = gb[b * tt:(b + 1) * tt]

    if lat:
        low_half = _iota((nb * tt, LANES), 1) < RW_HD
        for p in range(len(RW_OPS) // 2):
            for hp in range(2):
                first = vals[2 * p][:, hp * LANES:(hp + 1) * LANES]
                second = vals[2 * p + 1][:, hp * LANES:(hp + 1) * LANES]
                xs[p, 2 * hp] = jnp.where(low_half, first, pltpu.roll(second, RW_HD, 1))
                xs[p, 2 * hp + 1] = jnp.where(low_half, pltpu.roll(first, RW_HD, 1), second)

        def relayout(t, carry):
            for p in range(len(RW_OPS) // 2):
                tiles = [jnp.broadcast_to(xs[p, h, pl.ds(b * tt + t, 1), :], (SUBLANES, LANES))
                         for b in range(nb) for h in range(RW_HEADS)]
                both = jnp.concatenate(tiles, axis=0).T
                ops_ref[2 * p, t] = both[:RW_HD]
                ops_ref[2 * p + 1, t] = both[RW_HD:]
            return carry

        lax.fori_loop(0, tt, relayout, 0)
        grp = LANES // nb
        for vi in range(SUBLANES):
            vv_ref[vi] = jnp.concatenate([v[b * tt:(b + 1) * tt, vi * grp:(vi + 1) * grp] for b in range(nb)], axis=1)
    else:
        for j, x in enumerate(vals + [v]):
            xs[j, 0] = x[:, :LANES]
            xs[j, 1] = x[:, LANES:]
        for t in range(tt):
            for j in range(len(RW_OPS) + 1):
                tiles = []
                for hp in range(2):
                    toks = xs[j, hp, pl.ds(t, nb, stride=tt), :]
                    tiles += [toks[:, hh * RW_HD:(hh + 1) * RW_HD] for hh in range(2)]
                ops_ref[j, t] = jnp.concatenate(tiles, axis=0).T


def _rw_prep(lat, u_rw, mu, wg, ww, wa, w0, a0, k_k, k_a, r_k, ones, hsum):
    nb, seq_len, _ = u_rw.shape
    tt = (LANES // nb) if lat else SUBLANES
    n_tiles = seq_len // tt
    n_ops = len(RW_OPS) + (0 if lat else 1)
    sub = tt // SUBLANES
    last = seq_len // SUBLANES - 1
    full = lambda a: pl.BlockSpec(a.shape, lambda i: (0,) * a.ndim)
    tok_spec = pl.BlockSpec((nb, tt, RW_W), lambda i: (0, i, 0))
    tok_sds = jax.ShapeDtypeStruct((nb, seq_len, RW_W), F32)
    out_specs = [pl.BlockSpec((n_ops, tt, RW_HD, LANES), lambda i: (0, i, 0, 0))]
    out_shape = [jax.ShapeDtypeStruct((n_ops, seq_len, RW_HD, LANES), F32)]
    if lat:
        out_specs.append(pl.BlockSpec((SUBLANES, tt, LANES), lambda i: (0, i, 0)))
        out_shape.append(jax.ShapeDtypeStruct((SUBLANES, seq_len, LANES), F32))
    return pl.pallas_call(
        functools.partial(_rw_prep_kernel, lat, nb, tt, n_tiles),
        grid=(n_tiles,),
        in_specs=[
            pl.BlockSpec((nb, tt, RW_COLS), lambda i: (0, i, 0)),
            pl.BlockSpec((nb, SUBLANES, RW_COLS), lambda i: (0, jnp.maximum(i * sub - 1, 0), 0)),
            pl.BlockSpec((nb, SUBLANES, RW_COLS), lambda i: (0, jnp.minimum((i + 1) * sub, last), 0)),
        ] + [full(a) for a in (mu, wg, ww, wa, w0, a0, k_k, k_a, r_k, ones, hsum)],
        out_specs=out_specs + [tok_spec, tok_spec],
        out_shape=out_shape + [tok_sds, tok_sds],
        scratch_shapes=[pltpu.VMEM((n_ops // 2, RW_HEADS, nb * tt, LANES) if lat else (n_ops, 2, nb * tt, LANES), F32)],
        compiler_params=_params(("arbitrary",)),
        name="rw_prep_lat" if lat else "rw_prep_ctx",
    )(u_rw, u_rw, u_rw, mu, wg, ww, wa, w0, a0, k_k, k_a, r_k, ones, hsum)


def _allsum_sublanes(x):
    x = x + pltpu.roll(x, 4, 0)
    x = x + pltpu.roll(x, 2, 0)
    return x + pltpu.roll(x, 1, 0)


def _rw_scan_kernel(lat, nv, n_chunks, *refs):
    if lat:
        af_ref, ab_ref, sf_ref, sb_ref, vf_ref, vb_ref, s0_ref, yf_ref, yb_ref, s_scr = refs
    else:
        af_ref, ab_ref, sf_ref, sb_ref, yf_ref, yb_ref, sfin_ref, s_scr = refs
        vf_ref = vb_ref = None
    c = pl.program_id(0)

    @pl.when(c == 0)
    def _():
        s_scr[...] = s0_ref[...] if lat else jnp.zeros(s_scr.shape, F32)

    sub = _iota((SUBLANES, LANES), 0)
    nkb = RW_HD // SUBLANES
    bcast = lambda x: jnp.broadcast_to(x, (SUBLANES, LANES))

    def step(tt, carry):
        dirs = ((af_ref, sf_ref, vf_ref, yf_ref, tt), (ab_ref, sb_ref, vb_ref, yb_ref, SCAN_CHUNK - 1 - tt))
        for d, (a_ref, s_ref, v_ref, y_ref, tl) in enumerate(dirs):
            keys = lambda ref, j, kb: ref[j, tl, pl.ds(kb * SUBLANES, SUBLANES), :]

            def value_block(vb, carry2):
                sa = [None] * SUBLANES
                for kb in range(nkb):
                    kk = keys(s_ref, 1, kb)
                    for vi in range(SUBLANES):
                        p = s_scr[d, vb, kb, vi] * kk
                        sa[vi] = p if kb == 0 else sa[vi] + p
                sa = [_allsum_sublanes(x) for x in sa]
                if lat:
                    vrow = [bcast(v_ref[vi, pl.ds(tl, 1), :]) for vi in range(SUBLANES)]
                else:
                    vrow = [bcast(s_ref[2, tl, pl.ds(vb * SUBLANES + vi, 1), :]) for vi in range(SUBLANES)]
                acc = [None] * SUBLANES
                for kb in range(nkb):
                    w, kd, bb, r = keys(a_ref, 0, kb), keys(a_ref, 1, kb), keys(a_ref, 2, kb), keys(s_ref, 0, kb)
                    for vi in range(SUBLANES):
                        sn = s_scr[d, vb, kb, vi] * w - sa[vi] * bb + vrow[vi] * kd
                        s_scr[d, vb, kb, vi] = sn
                        acc[vi] = sn * r if kb == 0 else acc[vi] + sn * r
                if lat:
                    for vi in range(SUBLANES):
                        y_ref[vi, pl.ds(tl, 1), :] = _allsum_sublanes(acc[vi])[0:1, :]
                else:
                    y = jnp.zeros((SUBLANES, LANES), F32)
                    for vi in range(SUBLANES):
                        y = jnp.where(sub == vi, _allsum_sublanes(acc[vi]), y)
                    y_ref[tl, pl.ds(pl.multiple_of(vb * SUBLANES, SUBLANES), SUBLANES), :] = y
                return carry2

            if nv == 1:
                value_block(0, 0)
            else:
                lax.fori_loop(0, nv, value_block, 0)
        return carry

    lax.fori_loop(0, SCAN_CHUNK, step, 0)

    if not lat:
        @pl.when(c == n_chunks - 1)
        def _():
            sfin_ref[...] = s_scr[...]


def _rw_scan(lat, ops, vv, s0):
    seq_len = ops.shape[1]
    nc = seq_len // SCAN_CHUNK
    nv = 1 if lat else RW_HD // SUBLANES
    n_shared = 2 if lat else 3
    op_spec = lambda n, blk, rev: pl.BlockSpec(
        (n, SCAN_CHUNK, RW_HD, LANES), lambda c: (blk, (nc - 1 - c) if rev else c, 0, 0))
    s_shape = (2, nv, SUBLANES, SUBLANES, SUBLANES, LANES)
    in_specs = [op_spec(3, 0, False), op_spec(3, 1, True),
                op_spec(n_shared, 6 // n_shared, False), op_spec(n_shared, 6 // n_shared, True)]
    args = [ops, ops, ops, ops]
    if lat:
        y_spec = lambda rev: pl.BlockSpec((SUBLANES, SCAN_CHUNK, LANES), lambda c: (0, (nc - 1 - c) if rev else c, 0))
        y_sds = jax.ShapeDtypeStruct((SUBLANES, seq_len, LANES), F32)
        in_specs += [y_spec(False), y_spec(True), pl.BlockSpec(s_shape, lambda c: (0,) * 6)]
        args += [vv, vv, s0]
        out_specs = [y_spec(False), y_spec(True)]
        out_shape = [y_sds, y_sds]
    else:
        y_spec = lambda rev: pl.BlockSpec((SCAN_CHUNK, RW_HD, LANES), lambda c: ((nc - 1 - c) if rev else c, 0, 0))
        y_sds = jax.ShapeDtypeStruct((seq_len, RW_HD, LANES), F32)
        out_specs = [y_spec(False), y_spec(True), pl.BlockSpec(s_shape, lambda c: (0,) * 6)]
        out_shape = [y_sds, y_sds, jax.ShapeDtypeStruct(s_shape, F32)]
    return pl.pallas_call(
        functools.partial(_rw_scan_kernel, lat, nv, nc),
        grid=(nc,),
        in_specs=in_specs,
        out_specs=out_specs,
        out_shape=out_shape,
        scratch_shapes=[pltpu.VMEM(s_shape, F32)],
        compiler_params=_params(("arbitrary",)),
        name="rw_scan_lat" if lat else "rw_scan_ctx",
    )(*args)


def _rw_post_ctx_kernel(nb, tt, yf_ref, yb_ref, g_ref, gb_ref, lnw_ref, lnb_ref, o_ref):
    for t in range(tt):
        y = yf_ref[t] + yb_ref[t]
        dev = y - jnp.mean(y, axis=0, keepdims=True)
        yn = (dev * lax.rsqrt(jnp.mean(dev * dev, axis=0, keepdims=True) + RW_GN_EPS)).T
        tok = jnp.concatenate([yn[h * nb:(h + 1) * nb, :] for h in range(RW_HEADS)], axis=1)
        o_ref[:, t, :] = (tok * lnw_ref[...] + lnb_ref[...]) * g_ref[:, t, :] + gb_ref[:, t, :]


def _rw_post_lat_kernel(nb, yf_ref, yb_ref, g_ref, gb_ref, lnw_ref, lnb_ref, hs_ref, o_ref):
    grp = LANES // nb
    ys = [yf_ref[vi] + yb_ref[vi] for vi in range(SUBLANES)]
    for b in range(nb):
        y = jnp.concatenate([yv[:, b * grp:(b + 1) * grp] for yv in ys], axis=1)
        dev = y - _sel_dot(y, hs_ref[...], EXACT_TERMS) * (1.0 / RW_HD)
        yn = dev * lax.rsqrt(_sel_dot(dev * dev, hs_ref[...], NORM_TERMS) * (1.0 / RW_HD) + RW_GN_EPS)
        o_ref[b] = (yn * lnw_ref[...] + lnb_ref[...]) * g_ref[b] + gb_ref[b]


def _rw_post(lat, yf, yb, g, gb, lnw, lnb, hs):
    nb, seq_len, _ = g.shape
    tt = LANES if lat else SUBLANES
    full = lambda a: pl.BlockSpec(a.shape, lambda i: (0,) * a.ndim)
    tok_spec = pl.BlockSpec((nb, tt, RW_W), lambda i: (0, i, 0))
    if lat:
        y_spec = pl.BlockSpec((SUBLANES, tt, LANES), lambda i: (0, i, 0))
        body, extra = functools.partial(_rw_post_lat_kernel, nb), [hs]
    else:
        y_spec = pl.BlockSpec((tt, RW_HD, LANES), lambda i: (i, 0, 0))
        body, extra = functools.partial(_rw_post_ctx_kernel, nb, tt), []
    return pl.pallas_call(
        body,
        grid=(seq_len // tt,),
        in_specs=[y_spec, y_spec, tok_spec, tok_spec, full(lnw), full(lnb)] + [full(a) for a in extra],
        out_specs=tok_spec,
        out_shape=jax.ShapeDtypeStruct((nb, seq_len, RW_W), F32),
        compiler_params=_params(("parallel",)),
        name="rw_post_lat" if lat else "rw_post_ctx",
    )(yf, yb, g, gb, lnw, lnb, *extra)


def _rw_mixer(lat, u_rw, s0, prep_params, lnw, lnb, hs):
    nb = u_rw.shape[0]
    nkb = RW_HD // SUBLANES
    if lat:
        ops, vv, g, gb = _rw_prep(True, u_rw, *prep_params)
        grp = LANES // (nb * RW_HEADS)
        s0 = s0.reshape(nb, 2, RW_HEADS, grp, SUBLANES, nkb, SUBLANES)
        s0 = jnp.transpose(s0, (1, 5, 4, 6, 0, 2, 3)).reshape(2, 1, nkb, SUBLANES, SUBLANES, LANES)
        yf, yb = _rw_scan(True, ops, vv, s0)
        return _rw_post(True, yf, yb, g, gb, lnw, lnb, hs), None
    ops, g, gb = _rw_prep(False, u_rw, *prep_params)
    yf, yb, sfin = _rw_scan(False, ops, None, None)
    sfin = sfin.reshape(2, nkb, nkb, SUBLANES, SUBLANES, RW_HEADS, nb)
    sfin = jnp.transpose(sfin, (6, 0, 5, 1, 3, 2, 4)).reshape(nb, 2, RW_HEADS, RW_HD, RW_HD)
    return _rw_post(False, yf, yb, g, gb, lnw, lnb, hs), sfin


def _softplus(x):
    return jnp.maximum(x, 0.0) + jnp.log1p(jnp.exp(-jnp.abs(x)))


def _ssd_kernel(reverse, n_chunks, u_ref, up_ref, un_ref, h0_ref, cw_ref, cb_ref, dtb_ref, a_ref,
                tri_ref, exp_ref, *rest):
    if reverse:
        yf_ref, nw_ref, out_ref, hfin_ref, h_scr = rest
    else:
        dsk_ref, out_ref, hfin_ref, h_scr = rest
    s = pl.program_id(1)
    c = (n_chunks - 1 - s) if reverse else s

    @pl.when(s == 0)
    def _():
        h_scr[...] = h0_ref[...]

    q = SSD_CHUNK
    u = u_ref[...]
    xbc = u[:, SSD_W:SSD_W + 512]
    row = _iota(xbc.shape, 0)
    has_prev = (c > 0).astype(F32)
    has_next = (c < n_chunks - 1).astype(F32)
    prev = jnp.where(row == 0, up_ref[SUBLANES - 1:SUBLANES, SSD_W:SSD_W + 512] * has_prev, pltpu.roll(xbc, 1, 0))
    nxt = jnp.where(row == q - 1, un_ref[0:1, SSD_W:SSD_W + 512] * has_next, pltpu.roll(xbc, q - 1, 0))
    xc = _silu(cw_ref[0:1, :] * prev + cw_ref[1:2, :] * xbc + cw_ref[2:3, :] * nxt + cb_ref[...])
    x = xc[:, :SSD_W]
    b_all = xc[:, SSD_W:SSD_W + LANES]
    c_all = xc[:, SSD_W + LANES:]

    dt = _softplus(u[:, SSD_W + 512:] + dtb_ref[...])
    dta = dt * a_ref[...]
    ii = _iota((q, q), 0)
    jj = _iota((q, q), 1)
    tri = (jj >= ii) if reverse else (jj <= ii)
    cs = _sel_dot(dta, tri_ref[...], EXACT_TERMS, sel_first=True)
    cs_t = cs.T
    dt_e = _sel_dot(dt, exp_ref[...], EXACT_TERMS)
    cs_e = _sel_dot(cs, exp_ref[...], EXACT_TERMS)
    tot = cs_e[0:1, :] if reverse else cs_e[q - 1:q, :]
    xdt = x * dt_e
    dec_in = jnp.exp(tot - cs_e)
    dec_out = jnp.exp(cs_e)
    cdec = jnp.exp(tot)
    lane = _iota((q, LANES), 1)
    rowc = _iota((LANES, 1), 0)
    ys = []
    for g in range(2):
        bg = b_all[:, g * 64:(g + 1) * 64]
        cg = c_all[:, g * 64:(g + 1) * 64]
        gm = _bdot_nt(cg, bg)
        sl = slice(g * LANES, (g + 1) * LANES)
        xdt_g = xdt[:, sl]
        yd = []
        for hh in range(2):
            h = 2 * g + hh
            diff = cs[:, h:h + 1] - cs_t[h:h + 1, :]
            lm = jnp.exp(jnp.where(tri, diff, -jnp.inf))
            yd.append(_bdot(gm * lm, xdt_g))
        hg = h_scr[g]
        y = jnp.where(lane < 64, yd[0], yd[1]) + _bdot_nt(cg, hg) * dec_out[:, sl]
        st = _bdot((xdt_g * dec_in[:, sl]).T, bg)
        cd = cdec[:, sl]
        h_new = hg * jnp.where(rowc < 64, cd[:, 0:1], cd[:, 64:65]) + st
        h_scr[g] = h_new
        hfin_ref[g] = h_new
        ys.append(y)
    y = jnp.concatenate(ys, axis=1)
    if reverse:
        val = (yf_ref[...] + y) * _silu(u[:, :SSD_W])
        out_ref[...] = _rms(val, nw_ref[...])
    else:
        out_ref[...] = y + dsk_ref[...] * x


def _ssd_sweep(reverse, u_ssd, nb, seq_len, h0, cw, cb, dtb, a_neg, extra):
    nc = seq_len // SSD_CHUNK
    sub_per_chunk = SSD_CHUNK // SUBLANES
    last_sub = u_ssd.shape[0] // SUBLANES - 1
    chunk = (lambda s: nc - 1 - s) if reverse else (lambda s: s)
    rowblk = lambda b, s: b * nc + chunk(s)
    full = lambda a: pl.BlockSpec(a.shape, lambda b, s: (0,) * a.ndim)
    steps = np.arange(SSD_CHUNK)
    tri = jnp.asarray((steps[None, :] >= steps[:, None]) if reverse else (steps[None, :] <= steps[:, None]), BF16)
    expand = jnp.asarray(np.arange(LANES)[:, None] == np.arange(SSD_W)[None, :] // 64, BF16)
    in_specs = [
        pl.BlockSpec((SSD_CHUNK, SSD_PAD), lambda b, s: (rowblk(b, s), 0)),
        pl.BlockSpec((SUBLANES, SSD_PAD), lambda b, s: (jnp.maximum(rowblk(b, s) * sub_per_chunk - 1, 0), 0)),
        pl.BlockSpec((SUBLANES, SSD_PAD),
                     lambda b, s: (jnp.minimum((rowblk(b, s) + 1) * sub_per_chunk, last_sub), 0)),
        pl.BlockSpec((None, 2, LANES, 64), lambda b, s: (b, 0, 0, 0)),
        full(cw), full(cb), full(dtb), full(a_neg), full(tri), full(expand),
    ]
    args = [u_ssd, u_ssd, u_ssd, h0, cw, cb, dtb, a_neg, tri, expand]
    if reverse:
        yf, nw = extra
        in_specs += [pl.BlockSpec((SSD_CHUNK, SSD_W), lambda b, s: (rowblk(b, s), 0)), full(nw)]
        args += [yf, nw]
    else:
        in_specs += [full(extra)]
        args += [extra]
    return pl.pallas_call(
        functools.partial(_ssd_kernel, reverse, nc),
        grid=(nb, nc),
        in_specs=in_specs,
        out_specs=[pl.BlockSpec((SSD_CHUNK, SSD_W), lambda b, s: (rowblk(b, s), 0)),
                   pl.BlockSpec((None, 2, LANES, 64), lambda b, s: (b, 0, 0, 0))],
        out_shape=[jax.ShapeDtypeStruct((nb * seq_len, SSD_W), F32),
                   jax.ShapeDtypeStruct((nb, 2, LANES, 64), F32)],
        scratch_shapes=[pltpu.VMEM((2, LANES, 64), F32)],
        compiler_params=_params(("parallel", "arbitrary")),
        name="ssd_bwd" if reverse else "ssd_fwd",
    )(*args)


def _ssd_mixer(u_ssd, nb, seq_len, h0, cw, cb, dtb, a_neg, dsk, nw):
    yf, hf = _ssd_sweep(False, u_ssd, nb, seq_len, h0[:, 0], cw, cb, dtb[0], a_neg[0], dsk)
    out, hb = _ssd_sweep(True, u_ssd, nb, seq_len, h0[:, 1], cw, cb, dtb[1], a_neg[1], (yf, nw))
    return out, jnp.stack([hf, hb], axis=1).reshape(nb, 2, 4, 64, 64)


def _route(scores, bias):
    n_grp = N_EXPERTS // EXP_PER_GROUP
    sel = scores + bias
    sc = [scores[e:e + 1, :] for e in range(N_EXPERTS)]
    sl = [sel[e:e + 1, :] for e in range(N_EXPERTS)]
    as_int = lambda cond: jnp.where(cond, 1, 0)
    top2, grp = [], []
    for g in range(n_grp):
        members = range(g * EXP_PER_GROUP, (g + 1) * EXP_PER_GROUP)
        total = None
        for e in members:
            ahead = [as_int(sl[o] >= sl[e]) if o < e else as_int(sl[o] > sl[e]) for o in members if o != e]
            top2.append(ahead[0] + ahead[1] + ahead[2] < 2)
            kept = jnp.where(top2[e], sl[e], 0.0)
            total = kept if total is None else total + kept
        grp.append(total)
    rows = []
    for g in range(n_grp):
        beaten = [as_int(grp[o] >= grp[g]) if o < g else as_int(grp[o] > grp[g]) for o in range(n_grp) if o != g]
        best = beaten[0] + beaten[1] + beaten[2] == 0
        members = range(g * EXP_PER_GROUP, (g + 1) * EXP_PER_GROUP)
        chosen = [jnp.where(best, as_int(top2[e]), 0) > 0 for e in members]
        cw = [jnp.where(c, sc[e], 0.0) for c, e in zip(chosen, members)]
        den = cw[0] + cw[1] + cw[2] + cw[3]
        rows += [jnp.where(c, w / jnp.where(c, den, 1.0), 0.0) for c, w in zip(chosen, cw)]
    return jnp.concatenate(rows, axis=0)


def _outproj_kernel(oatt_ref, orw_ref, ossd_ref, x_ref, mod_ref, wout_ref,
                    n2_ref, rw_ref, rb_ref, x1_ref, h2_ref, comb_ref):
    o = jnp.concatenate([oatt_ref[...], orw_ref[...], ossd_ref[...]], axis=1)
    x1 = x_ref[...] + mod_ref[2] * _bdot(o, wout_ref[...])
    h2 = _rms(x1, n2_ref[...]) * (1.0 + mod_ref[4]) + mod_ref[3]
    comb = _route(_sigmoid(_bdot_nt(rw_ref[...], h2)), rb_ref[...])
    x1_ref[...] = x1
    h2_ref[...] = h2.astype(BF16)
    comb_ref[...] = jnp.concatenate([comb, jnp.zeros((LANES - N_EXPERTS, comb.shape[1]), F32)], axis=0).T


def _outproj(o_att, o_rw, o_ssd, x, mod_l, row_fn, w_out_bf, n2, router_w, router_b):
    n_tok = x.shape[0]
    tile = lambda w: pl.BlockSpec((TOKEN_TILE, w), lambda i: (i, 0))
    full = lambda a: pl.BlockSpec(a.shape, lambda i: (0,) * a.ndim)
    return pl.pallas_call(
        _outproj_kernel,
        grid=(n_tok // TOKEN_TILE,),
        in_specs=[tile(ATT_W), tile(RW_W), tile(SSD_W), tile(D_MODEL),
                  _mod_spec(row_fn)] + [full(a) for a in (w_out_bf, n2, router_w, router_b)],
        out_specs=[tile(D_MODEL), tile(D_MODEL), tile(LANES)],
        out_shape=[jax.ShapeDtypeStruct((n_tok, D_MODEL), F32),
                   jax.ShapeDtypeStruct((n_tok, D_MODEL), BF16),
                   jax.ShapeDtypeStruct((n_tok, LANES), F32)],
        compiler_params=_params(("parallel",)),
        name="outproj",
    )(o_att, o_rw, o_ssd, x, mod_l, w_out_bf, n2, router_w, router_b)


def _moe_kernel(h2_ref, comb_ref, x1_ref, mod_ref, wg_ref, wu_ref, wd_ref, o_ref, acc_ref):
    e = pl.program_id(1)

    @pl.when(e == 0)
    def _():
        acc_ref[...] = jnp.zeros_like(acc_ref)

    h2 = h2_ref[...]
    hid = _silu(jnp.dot(h2, wg_ref[...], preferred_element_type=F32)) * jnp.dot(
        h2, wu_ref[...], preferred_element_type=F32)
    comb = comb_ref[...]
    wcol = jnp.sum(jnp.where(_iota(comb.shape, 1) == e, comb, 0.0), axis=1, keepdims=True)
    acc_ref[...] += _bdot(hid * wcol, wd_ref[...])

    @pl.when(e == N_EXPERTS - 1)
    def _():
        o_ref[...] = x1_ref[...] + mod_ref[5] * acc_ref[...]


def _moe(h2, comb, x1, mod_l, row_fn, wg_bf, wu_bf, wd_bf):
    n_tok = x1.shape[0]
    tile = lambda w: pl.BlockSpec((MOE_TILE, w), lambda i, e: (i, 0))
    return pl.pallas_call(
        _moe_kernel,
        grid=(n_tok // MOE_TILE, N_EXPERTS),
        in_specs=[tile(D_MODEL), tile(LANES), tile(D_MODEL), _mod_spec(row_fn),
                  pl.BlockSpec((None, D_MODEL, D_FF), lambda i, e: (e, 0, 0)),
                  pl.BlockSpec((None, D_MODEL, D_FF), lambda i, e: (e, 0, 0)),
                  pl.BlockSpec((None, D_FF, D_MODEL), lambda i, e: (e, 0, 0))],
        out_specs=tile(D_MODEL),
        out_shape=jax.ShapeDtypeStruct((n_tok, D_MODEL), F32),
        scratch_shapes=[pltpu.VMEM((MOE_TILE, D_MODEL), F32)],
        compiler_params=_params(("parallel", "arbitrary")),
        name="moe",
    )(h2, comb, x1, mod_l, wg_bf, wu_bf, wd_bf)


def _rope_tables(seq_len):
    t = np.arange(seq_len)
    pos = np.stack([t // GRID_W, t % GRID_W], 0).astype(np.float32)
    n_freq = HEAD_DIM // 4
    inv = jnp.asarray(ROPE_BASE, F32) ** (-jnp.arange(n_freq, dtype=F32) / n_freq)
    ang = jnp.asarray(pos)[:, :, None] * inv
    cos, sin = jnp.cos(ang), jnp.sin(ang)
    cos_h = jnp.concatenate([cos[0], cos[0], cos[1], cos[1]], -1)
    sin_h = jnp.concatenate([-sin[0], sin[0], -sin[1], sin[1]], -1)
    return jnp.tile(cos_h, (1, 2)), jnp.tile(sin_h, (1, 2))


def _pad_rows(w, start, total):
    return jnp.pad(w, ((0, 0),) * (w.ndim - 2) + ((start, total - start - w.shape[-2]), (0, 0)))


def _value_order(x, axis, start, grp):
    if grp is None:
        return x
    axis %= x.ndim
    sl = lambda a, b: lax.slice_in_dim(x, a, b, axis=axis)
    mid = sl(start, start + RW_W)
    shape = mid.shape
    mid = mid.reshape(shape[:axis] + (RW_HEADS, grp, RW_HD // grp) + shape[axis + 1:])
    order = tuple(range(axis)) + (axis + 2, axis, axis + 1) + tuple(range(axis + 3, mid.ndim))
    mid = jnp.transpose(mid, order).reshape(shape)
    return jnp.concatenate([sl(0, start), mid, sl(start + RW_W, x.shape[axis])], axis=axis)


def _rw_layer_params(l, P, grp):
    row = lambda a: a.reshape(1, -1)
    v0 = 2 * RW_W
    w_in = jnp.pad(P["w_in"][l], ((0, 0), (0, SSD_PAD - SSD_COLS))).astype(BF16)
    head = np.arange(RW_W) // RW_HD if grp is None else (np.arange(RW_W) // grp) % RW_HEADS
    rp = {}
    rp["w_in"] = _value_order(w_in, 1, ATT_COLS + v0, grp)
    rp["w_out"] = _value_order(P["w_out"][l].astype(BF16), 0, ATT_W, grp)
    rp["prep"] = (
        _value_order(P["rw_mu"][l], 1, v0, grp),
        _pad_rows(_value_order(P["rw_g_up"][l], 1, 0, grp), 64, LANES),
        _pad_rows(P["rw_w_up"][l], 0, LANES),
        _pad_rows(P["rw_a_up"][l], 32, LANES),
        P["rw_w0"][l], P["rw_a0"][l],
        row(P["rw_k_k"][l]), row(P["rw_k_a"][l]), row(P["rw_r_k"][l]),
        _head_selector(np.arange(RW_W) // RW_HD), _head_selector(np.arange(RW_W) // RW_HD, head),
    )
    rp["lnw"] = _value_order(row(P["rw_ln_w"][l]), 1, 0, grp)
    rp["lnb"] = _value_order(row(P["rw_ln_b"][l]), 1, 0, grp)
    rp["hs"] = _head_selector(head)
    return rp


def _layer_params(l, P):
    row = lambda a: a.reshape(1, -1)
    lp = {}
    lp["n1"] = row(P["norm1_w"][l])
    lp["n2"] = row(P["norm2_w"][l])
    lp["qw"] = row(jnp.tile(P["q_norm_w"][l], ATT_HEADS))
    lp["kw"] = row(jnp.tile(P["k_norm_w"][l], ATT_KV_HEADS))
    lp["sink"] = P["attn_sink"][l]
    pad_heads = lambda a: jnp.pad(a, ((0, 0), (0, LANES - a.shape[1]))).reshape(2, 1, LANES)
    lp["ssd"] = (
        P["ssd_conv_w"][l], row(P["ssd_conv_b"][l]),
        pad_heads(P["ssd_dt_bias"][l]), pad_heads(-jnp.exp(P["ssd_a_log"][l])),
        row(jnp.repeat(P["ssd_d"][l], 64)), row(P["ssd_norm_w"][l]),
    )
    lp["wg"] = P["exp_gate"][l].astype(BF16)
    lp["wu"] = P["exp_up"][l].astype(BF16)
    lp["wd"] = P["exp_down"][l].astype(BF16)
    return lp


def _trunk(x, nb, seq_len, is_ctx, mod, layers, rw_layers, router, cache_k, cache_v, state_rwkv, state_ssd):
    tiles_per_seq = seq_len // TOKEN_TILE
    moe_per_seq = max(seq_len // MOE_TILE, 1)
    if is_ctx:
        row_tok = lambda i: 0
        row_moe = lambda i: 0
        rope = None
    else:
        row_tok = lambda i: 1 + i // tiles_per_seq
        row_moe = lambda i: 1 + i // moe_per_seq
        rope = _rope_tables(seq_len)
    new_k, new_v, new_rw, new_ssd = [], [], [], []
    for l, (lp, rp) in enumerate(zip(layers, rw_layers)):
        q, k, v, u_rw, u_ssd = _inproj(x, mod[l], row_tok, lp["n1"], rp["w_in"], lp["qw"], lp["kw"], rope, seq_len)
        if is_ctx:
            o_att = _attn_ctx(lp["sink"], q, k, v, nb, seq_len)
            rw0 = None
            ssd0 = jnp.zeros((nb, 2, 2, LANES, 64), F32)
        else:
            kc = cache_k[:, l].reshape(nb, -1, KV_W)
            vc = cache_v[:, l].reshape(nb, -1, KV_W)
            o_att = _attn_lat(lp["sink"], q, k, v, kc, vc, nb, seq_len)
            rw0 = state_rwkv[:, l]
            ssd0 = state_ssd[:, l].reshape(nb, 2, 2, LANES, 64)
        o_rw, s_rw = _rw_mixer(not is_ctx, u_rw.reshape(nb, seq_len, RW_COLS), rw0, rp["prep"],
                               rp["lnw"], rp["lnb"], rp["hs"])
        o_ssd, s_ssd = _ssd_mixer(u_ssd, nb, seq_len, ssd0, *lp["ssd"])
        if is_ctx:
            new_k.append(k.reshape(nb, seq_len, ATT_KV_HEADS, HEAD_DIM))
            new_v.append(v.reshape(nb, seq_len, ATT_KV_HEADS, HEAD_DIM))
            new_rw.append(s_rw)
            new_ssd.append(s_ssd)
        x1, h2, comb = _outproj(o_att, o_rw.reshape(nb * seq_len, RW_W), o_ssd, x, mod[l], row_tok, rp["w_out"],
                                lp["n2"], *router)
        x = _moe(h2, comb, x1, mod[l], row_moe, lp["wg"], lp["wu"], lp["wd"])
    if is_ctx:
        return x, tuple(jnp.stack(t, 1) for t in (new_k, new_v, new_rw, new_ssd))
    return x, None


def kernel(x_prompt, x_sample, cache_k, cache_v, state_rwkv, state_ssd, c, c_ctx, w_mod, b_mod, norm1_w, norm2_w, w_in, w_out, q_norm_w, k_norm_w, attn_sink, rw_mu, rw_w0, rw_w_up, rw_a0, rw_a_up, rw_g_up, rw_k_k, rw_k_a, rw_r_k, rw_ln_w, rw_ln_b, ssd_conv_w, ssd_conv_b, ssd_dt_bias, ssd_a_log, ssd_d, ssd_norm_w, router_w, router_bias, exp_gate, exp_up, exp_down):
    P = dict(norm1_w=norm1_w, norm2_w=norm2_w, w_in=w_in, w_out=w_out, q_norm_w=q_norm_w, k_norm_w=k_norm_w,
             attn_sink=attn_sink, rw_mu=rw_mu, rw_w0=rw_w0, rw_w_up=rw_w_up, rw_a0=rw_a0, rw_a_up=rw_a_up,
             rw_g_up=rw_g_up, rw_k_k=rw_k_k, rw_k_a=rw_k_a, rw_r_k=rw_r_k, rw_ln_w=rw_ln_w, rw_ln_b=rw_ln_b,
             ssd_conv_w=ssd_conv_w, ssd_conv_b=ssd_conv_b, ssd_dt_bias=ssd_dt_bias, ssd_a_log=ssd_a_log,
             ssd_d=ssd_d, ssd_norm_w=ssd_norm_w, exp_gate=exp_gate, exp_up=exp_up, exp_down=exp_down)
    nb_ctx, seq_ctx, _ = x_prompt.shape
    nb_lat, seq_lat, _ = x_sample.shape
    assert nb_lat + 1 <= SUBLANES and seq_ctx % TOKEN_TILE == 0 and seq_lat % MOE_TILE == 0
    assert nb_ctx * RW_HEADS == LANES and LANES % (nb_lat * RW_HEADS) == 0
    cond8 = jnp.zeros((SUBLANES, D_MODEL), F32).at[0].set(c_ctx).at[1:1 + nb_lat].set(c)
    mod = _modulation(cond8, w_mod, b_mod)
    layers = [_layer_params(l, P) for l in range(DEPTH)]
    rw_ctx = [_rw_layer_params(l, P, None) for l in range(DEPTH)]
    rw_lat = [_rw_layer_params(l, P, LANES // (nb_lat * RW_HEADS)) for l in range(DEPTH)]
    router = (router_w.T.astype(BF16), router_bias.reshape(N_EXPERTS, 1))
    y_prompt, ctx_state = _trunk(x_prompt.reshape(-1, D_MODEL), nb_ctx, seq_ctx, True, mod, layers, rw_ctx, router,
                                 None, None, None, None)
    y_sample, _ = _trunk(x_sample.reshape(-1, D_MODEL), nb_lat, seq_lat, False, mod, layers, rw_lat, router,
                         cache_k, cache_v, state_rwkv, state_ssd)
    return (y_prompt.reshape(x_prompt.shape), y_sample.reshape(x_sample.shape)) + ctx_state
```

```python
import functools
import math

import jax
import jax.numpy as jnp
import numpy as np
from jax import lax
from jax.experimental import pallas as pl
from jax.experimental.pallas import tpu as pltpu

F32 = jnp.float32
BF16 = jnp.bfloat16
HIGHEST = lax.Precision.HIGHEST

D_MODEL = 1024
DEPTH = 2
GRID_W = 64
EPS = 1e-6
ATT_HEADS = 8
ATT_KV_HEADS = 2
HEAD_DIM = 64
ATT_W = 512
KV_W = 128
ATT_COLS = 768
WINDOW = 128
BLK = 128
ROPE_BASE = 10000.0
RW_HEADS = 4
RW_HD = 64
RW_W = 256
RW_COLS = 896
RW_DECAY_SCALE = 0.606531
RW_GN_EPS = 64e-5
SSD_W = 256
SSD_CHUNK = 128
SSD_COLS = 772
SSD_PAD = 896
IN_PAD = ATT_COLS + RW_COLS + SSD_PAD
N_EXPERTS = 16
EXP_PER_GROUP = 4
D_FF = 256

LANES = 128
SUBLANES = 8
TOKEN_TILE = 256
MOE_TILE = 1024
SCAN_CHUNK = 16
SCAN_BLOCKS = 4
SCAN_CHAINS = 8
VMEM_LIMIT = 48 * 1024 * 1024


def _params(sem):
    return pltpu.CompilerParams(dimension_semantics=sem, vmem_limit_bytes=VMEM_LIMIT)


def _bdot(a, b):
    return jnp.dot(a.astype(BF16), b.astype(BF16), preferred_element_type=F32)


def _bdot_nt(a, b):
    return lax.dot_general(a.astype(BF16), b.astype(BF16), (((1,), (1,)), ((), ())),
                           preferred_element_type=F32)


def _sel_dot(x, sel, terms, sel_first=False):
    acc = None
    rem = x
    for i in range(terms):
        piece = rem.astype(BF16)
        part = jnp.dot(sel, piece, preferred_element_type=F32) if sel_first else jnp.dot(
            piece, sel, preferred_element_type=F32)
        acc = part if acc is None else acc + part
        if i + 1 < terms:
            rem = rem - piece.astype(F32)
    return acc


def _iota(shape, axis):
    return lax.broadcasted_iota(jnp.int32, shape, axis)


def _head_selector(heads_in, heads_out=None):
    heads_out = heads_in if heads_out is None else heads_out
    return jnp.asarray(np.asarray(heads_in)[:, None] == np.asarray(heads_out)[None, :], BF16)


NORM_TERMS = 2
EXACT_TERMS = 3


def _sigmoid(x):
    return 1.0 / (1.0 + jnp.exp(-x))


def _silu(x):
    return x * _sigmoid(x)


def _rms(x, w):
    return x * lax.rsqrt(jnp.mean(x * x, -1, keepdims=True) + EPS) * w


def _mod_kernel(c_ref, w_ref, b_ref, o_ref):
    o_ref[...] = _bdot(_silu(c_ref[...]), w_ref[...]) + b_ref[...]


def _modulation(cond8, w_mod, b_mod):
    out = pl.pallas_call(
        _mod_kernel,
        grid=(DEPTH, 6),
        in_specs=[
            pl.BlockSpec((SUBLANES, D_MODEL), lambda l, j: (0, 0)),
            pl.BlockSpec((None, D_MODEL, D_MODEL), lambda l, j: (l, 0, j)),
            pl.BlockSpec((None, 1, D_MODEL), lambda l, j: (l, 0, j)),
        ],
        out_specs=pl.BlockSpec((None, None, SUBLANES, D_MODEL), lambda l, j: (l, j, 0, 0)),
        out_shape=jax.ShapeDtypeStruct((DEPTH, 6, SUBLANES, D_MODEL), F32),
        compiler_params=_params(("arbitrary", "arbitrary")),
        name="modulation",
    )(cond8, w_mod, b_mod.reshape(DEPTH, 1, 6 * D_MODEL))
    return out.reshape(DEPTH, 6, SUBLANES, 1, D_MODEL)


def _mod_spec(row_fn):
    return pl.BlockSpec((6, None, 1, D_MODEL), lambda *idx: (0, row_fn(idx[0]), 0, 0))


def _swap16(x):
    w = x.shape[1]
    first = (_iota(x.shape, 1) & 31) < 16
    return jnp.where(first, pltpu.roll(x, w - 16, 1), pltpu.roll(x, 16, 1))


def _inproj_kernel(use_rope, x_ref, mod_ref, n1_ref, w_ref, qw_ref, kw_ref, hq_ref, *rest):
    if use_rope:
        cos_ref, sin_ref, q_ref, k_ref, v_ref, urw_ref, ussd_ref = rest
    else:
        q_ref, k_ref, v_ref, urw_ref, ussd_ref = rest
    h = _rms(x_ref[...], n1_ref[...]) * (1.0 + mod_ref[1]) + mod_ref[0]
    u = _bdot(h, w_ref[...])
    q = u[:, :ATT_W]
    k = u[:, ATT_W:ATT_W + KV_W]
    q = q * lax.rsqrt(_sel_dot(q * q, hq_ref[...], NORM_TERMS) * (1.0 / HEAD_DIM) + EPS) * qw_ref[...]
    k = k * lax.rsqrt(_sel_dot(k * k, hq_ref[:KV_W, :KV_W], NORM_TERMS) * (1.0 / HEAD_DIM) + EPS) * kw_ref[...]
    if use_rope:
        cos = cos_ref[...]
        sin = sin_ref[...]
        k = k * cos + _swap16(k) * sin
        cos4 = jnp.concatenate([cos] * 4, axis=1)
        sin4 = jnp.concatenate([sin] * 4, axis=1)
        q = q * cos4 + _swap16(q) * sin4
    q_ref[...] = q
    k_ref[...] = k
    v_ref[...] = u[:, ATT_W + KV_W:ATT_COLS]
    urw_ref[...] = u[:, ATT_COLS:ATT_COLS + RW_COLS]
    ussd_ref[...] = u[:, ATT_COLS + RW_COLS:]


def _inproj(x, mod_l, row_fn, n1, w_in_bf, qw, kw, rope, seq_len):
    n_tok = x.shape[0]
    tiles_per_seq = seq_len // TOKEN_TILE
    in_specs = [
        pl.BlockSpec((TOKEN_TILE, D_MODEL), lambda i: (i, 0)),
        _mod_spec(row_fn),
        pl.BlockSpec((1, D_MODEL), lambda i: (0, 0)),
        pl.BlockSpec((D_MODEL, IN_PAD), lambda i: (0, 0)),
        pl.BlockSpec((1, ATT_W), lambda i: (0, 0)),
        pl.BlockSpec((1, KV_W), lambda i: (0, 0)),
        pl.BlockSpec((ATT_W, ATT_W), lambda i: (0, 0)),
    ]
    args = [x, mod_l, n1, w_in_bf, qw, kw, _head_selector(np.arange(ATT_W) // HEAD_DIM)]
    if rope is not None:
        in_specs += [pl.BlockSpec((TOKEN_TILE, LANES), lambda i: (i % tiles_per_seq, 0))] * 2
        args += list(rope)
    widths = (ATT_W, KV_W, KV_W, RW_COLS, SSD_PAD)
    return pl.pallas_call(
        functools.partial(_inproj_kernel, rope is not None),
        grid=(n_tok // TOKEN_TILE,),
        in_specs=in_specs,
        out_specs=[pl.BlockSpec((TOKEN_TILE, w), lambda i: (i, 0)) for w in widths],
        out_shape=[jax.ShapeDtypeStruct((n_tok, w), F32) for w in widths],
        compiler_params=_params(("parallel",)),
        name="inproj",
    )(*args)


LOG2E = 1.4426950408889634


def _attend_heads(sink_ref, q, parts):
    groups = ATT_HEADS // ATT_KV_HEADS
    outs = []
    for hk in range(ATT_KV_HEADS):
        cols = slice(hk * HEAD_DIM, (hk + 1) * HEAD_DIM)
        keys = [k[:, cols].astype(BF16) for k, _, _ in parts]
        vals = [v[:, cols].astype(BF16) for _, v, _ in parts]
        for g in range(groups):
            hd = hk * groups + g
            qh = q[:, hd * HEAD_DIM:(hd + 1) * HEAD_DIM].astype(BF16)
            sink = sink_ref[hd] * LOG2E
            scores = []
            for kh, (_, _, mask) in zip(keys, parts):
                s = lax.dot_general(qh, kh, (((1,), (1,)), ((), ())),
                                    preferred_element_type=F32) * (HEAD_DIM ** -0.5 * LOG2E)
                scores.append(s if mask is None else jnp.where(mask, s, -1e30))
            m = jnp.max(scores[0], -1, keepdims=True)
            for s in scores[1:]:
                m = jnp.maximum(m, jnp.max(s, -1, keepdims=True))
            m = jnp.maximum(m, sink)
            den = jnp.exp2(sink - m)
            o = None
            for s, vh in zip(scores, vals):
                p = jnp.exp2(s - m)
                den = den + jnp.sum(p, -1, keepdims=True)
                pv = jnp.dot(p.astype(BF16), vh, preferred_element_type=F32)
                o = pv if o is None else o + pv
            outs.append(o / den)
    return jnp.concatenate(outs, axis=1)


def _attn_ctx_kernel(sink_ref, q_ref, k_ref, v_ref, o_ref):
    o_ref[...] = _attend_heads(sink_ref, q_ref[...], [(k_ref[...], v_ref[...], None)])


def _attn_ctx(sink, q, k, v, nb, seq_len):
    return pl.pallas_call(
        _attn_ctx_kernel,
        grid=(nb,),
        in_specs=[
            pl.BlockSpec(memory_space=pltpu.SMEM),
            pl.BlockSpec((seq_len, ATT_W), lambda b: (b, 0)),
            pl.BlockSpec((seq_len, KV_W), lambda b: (b, 0)),
            pl.BlockSpec((seq_len, KV_W), lambda b: (b, 0)),
        ],
        out_specs=pl.BlockSpec((seq_len, ATT_W), lambda b: (b, 0)),
        out_shape=jax.ShapeDtypeStruct((nb * seq_len, ATT_W), F32),
        compiler_params=_params(("parallel",)),
        name="attn_ctx",
    )(sink, q, k, v)


def _attn_lat_kernel(seq_len, sink_ref, q_ref, k_ref, v_ref, kc_ref, vc_ref, o_ref):
    i = pl.program_id(1)
    band = 3 * BLK
    start = pl.multiple_of(jnp.clip((i - 1) * BLK, 0, seq_len - band), BLK)
    dist = jnp.abs(i * BLK + _iota((BLK, band), 0) - (start + _iota((BLK, band), 1)))
    parts = [(k_ref[pl.ds(start, band), :], v_ref[pl.ds(start, band), :], dist <= WINDOW),
             (kc_ref[...], vc_ref[...], None)]
    o_ref[...] = _attend_heads(sink_ref, q_ref[...], parts)


def _attn_lat(sink, q, k, v, kc, vc, nb, seq_len):
    nblk = seq_len // BLK
    past = kc.shape[1]
    return pl.pallas_call(
        functools.partial(_attn_lat_kernel, seq_len),
        grid=(nb, nblk),
        in_specs=[
            pl.BlockSpec(memory_space=pltpu.SMEM),
            pl.BlockSpec((BLK, ATT_W), lambda b, i: (b * nblk + i, 0)),
            pl.BlockSpec((seq_len, KV_W), lambda b, i: (b, 0)),
            pl.BlockSpec((seq_len, KV_W), lambda b, i: (b, 0)),
            pl.BlockSpec((None, past, KV_W), lambda b, i: (b, 0, 0)),
            pl.BlockSpec((None, past, KV_W), lambda b, i: (b, 0, 0)),
        ],
        out_specs=pl.BlockSpec((BLK, ATT_W), lambda b, i: (b * nblk + i, 0)),
        out_shape=jax.ShapeDtypeStruct((nb * seq_len, ATT_W), F32),
        compiler_params=_params(("parallel", "arbitrary")),
        name="attn_lat",
    )(sink, q, k, v, kc, vc)


RW_OPS = ("w_f", "kd_f", "b_f", "w_b", "kd_b", "b_b", "r", "kk")


def _rw_prep_kernel(lat, nb, tt, n_tiles, u_ref, up_ref, un_ref, mu_ref, wg_ref, ww_ref, wa_ref,
                    w0_ref, a0_ref, kk_ref, ka_ref, rk_ref, ones_ref, hsum_ref, *rest):
    if lat:
        ops_ref, vv_ref, g_ref, gb_ref, xs = rest
    else:
        ops_ref, g_ref, gb_ref, xs, xv = rest
    i = pl.program_id(0)
    has_prev = (i > 0).astype(F32)
    has_next = (i < n_tiles - 1).astype(F32)
    row = _iota((tt, RW_COLS), 0)
    mu0 = mu_ref[0:1, :]
    mu1 = mu_ref[1:2, :]
    mixed = []
    for b in range(nb):
        ub = u_ref[b]
        prev = jnp.where(row == 0, up_ref[b, SUBLANES - 1:SUBLANES, :] * has_prev, pltpu.roll(ub, 1, 0))
        nxt = jnp.where(row == tt - 1, un_ref[b, 0:1, :] * has_next, pltpu.roll(ub, tt - 1, 0))
        mixed.append(ub + mu0 * (prev - ub) + mu1 * (nxt - ub))
    u = jnp.concatenate(mixed, axis=0)
    r = u[:, :RW_W]
    k = u[:, RW_W:2 * RW_W]
    v = u[:, 2 * RW_W:3 * RW_W]
    low = u[:, 3 * RW_W:]
    kk = k * kk_ref[...]
    kk = kk * lax.rsqrt(_sel_dot(kk * kk, ones_ref[...], NORM_TERMS) + 1e-12)
    tanh_low = jnp.tanh(low)
    vals = [None] * len(RW_OPS)
    for d in range(2):
        w = jnp.exp(-RW_DECAY_SCALE * _sigmoid(w0_ref[d:d + 1, :] + _bdot(tanh_low, ww_ref[d])))
        a = _sigmoid(a0_ref[d:d + 1, :] + _bdot(low, wa_ref[d]))
        vals[3 * d:3 * d + 3] = [w, k * (1.0 + (a - 1.0) * ka_ref[...]), kk * a]
    vals[6:8] = [r, kk]
    g = _bdot(_sigmoid(low), wg_ref[...])
    gb = _sel_dot(r * k * rk_ref[...], hsum_ref[...], EXACT_TERMS) * v * g
    for b in range(nb):
        g_ref[b] = g[b * tt:(b + 1) * tt]
        gb_ref[b] = gb[b * tt:(b + 1) * tt]

    n_pairs = len(RW_OPS) // 2
    low_half = _iota((nb * tt, LANES), 1) < RW_HD
    for p in range(n_pairs):
        for hp in range(2):
            first = vals[2 * p][:, hp * LANES:(hp + 1) * LANES]
            second = vals[2 * p + 1][:, hp * LANES:(hp + 1) * LANES]
            xs[p, 2 * hp] = jnp.where(low_half, first, pltpu.roll(second, RW_HD, 1))
            xs[p, 2 * hp + 1] = jnp.where(low_half, pltpu.roll(first, RW_HD, 1), second)

    def emit(p, t, tiles):
        both = jnp.concatenate(tiles, axis=0).T
        ops_ref[2 * p, t] = both[:RW_HD]
        ops_ref[2 * p + 1, t] = both[RW_HD:]

    if lat:
        def relayout(t, carry):
            for p in range(n_pairs):
                emit(p, t, [jnp.broadcast_to(xs[p, h, pl.ds(b * tt + t, 1), :], (SUBLANES, LANES))
                            for b in range(nb) for h in range(RW_HEADS)])
            return carry

        lax.fori_loop(0, tt, relayout, 0)
        grp = LANES // nb
        for vi in range(SUBLANES):
            vv_ref[:, vi, :] = jnp.concatenate(
                [v[b * tt:(b + 1) * tt, vi * grp:(vi + 1) * grp] for b in range(nb)], axis=1)
    else:
        xv[0] = v[:, :LANES]
        xv[1] = v[:, LANES:]
        for t in range(tt):
            at_t = pl.ds(t, nb, stride=tt)
            for p in range(n_pairs):
                emit(p, t, [xs[p, h, at_t, :] for h in range(RW_HEADS)])
            halves = [xv[hp, at_t, :] for hp in range(2)]
            ops_ref[2 * n_pairs, t] = jnp.concatenate(
                [halves[hp][:, hh * RW_HD:(hh + 1) * RW_HD] for hp in range(2) for hh in range(2)], axis=0).T


def _rw_prep(lat, u_rw, mu, wg, ww, wa, w0, a0, k_k, k_a, r_k, ones, hsum):
    nb, seq_len, _ = u_rw.shape
    tt = (LANES // nb) if lat else SUBLANES
    n_tiles = seq_len // tt
    n_ops = len(RW_OPS) + (0 if lat else 1)
    sub = tt // SUBLANES
    last = seq_len // SUBLANES - 1
    full = lambda a: pl.BlockSpec(a.shape, lambda i: (0,) * a.ndim)
    tok_spec = pl.BlockSpec((nb, tt, RW_W), lambda i: (0, i, 0))
    tok_sds = jax.ShapeDtypeStruct((nb, seq_len, RW_W), F32)
    out_specs = [pl.BlockSpec((n_ops, tt, RW_HD, LANES), lambda i: (0, i, 0, 0))]
    out_shape = [jax.ShapeDtypeStruct((n_ops, seq_len, RW_HD, LANES), F32)]
    scratch = [pltpu.VMEM((len(RW_OPS) // 2, RW_HEADS, nb * tt, LANES), F32)]
    if lat:
        out_specs.append(pl.BlockSpec((tt, SUBLANES, LANES), lambda i: (i, 0, 0)))
        out_shape.append(jax.ShapeDtypeStruct((seq_len, SUBLANES, LANES), F32))
    else:
        scratch.append(pltpu.VMEM((2, nb * tt, LANES), F32))
    return pl.pallas_call(
        functools.partial(_rw_prep_kernel, lat, nb, tt, n_tiles),
        grid=(n_tiles,),
        in_specs=[
            pl.BlockSpec((nb, tt, RW_COLS), lambda i: (0, i, 0)),
            pl.BlockSpec((nb, SUBLANES, RW_COLS), lambda i: (0, jnp.maximum(i * sub - 1, 0), 0)),
            pl.BlockSpec((nb, SUBLANES, RW_COLS), lambda i: (0, jnp.minimum((i + 1) * sub, last), 0)),
        ] + [full(a) for a in (mu, wg, ww, wa, w0, a0, k_k, k_a, r_k, ones, hsum)],
        out_specs=out_specs + [tok_spec, tok_spec],
        out_shape=out_shape + [tok_sds, tok_sds],
        scratch_shapes=scratch,
        compiler_params=_params(("arbitrary",)),
        name="rw_prep_lat" if lat else "rw_prep_ctx",
    )(u_rw, u_rw, u_rw, mu, wg, ww, wa, w0, a0, k_k, k_a, r_k, ones, hsum)


def _allsum_sublanes(x):
    x = x + pltpu.roll(x, 4, 0)
    x = x + pltpu.roll(x, 2, 0)
    return x + pltpu.roll(x, 1, 0)


def _rw_scan_kernel(lat, nv, n_chunks, *refs):
    if lat:
        af_ref, ab_ref, sf_ref, sb_ref, vf_ref, vb_ref, s0_ref, yf_ref, yb_ref, s_scr = refs
    else:
        af_ref, ab_ref, sf_ref, sb_ref, yf_ref, yb_ref, sfin_ref, s_scr = refs
        vf_ref = vb_ref = None
    c = pl.program_id(0)

    @pl.when(c == 0)
    def _():
        s_scr[...] = s0_ref[...] if lat else jnp.zeros(s_scr.shape, F32)

    n_part = SCAN_CHAINS // min(nv, SCAN_BLOCKS)

    def total(parts):
        while len(parts) > 1:
            parts = [parts[i] + parts[i + 1] for i in range(0, len(parts), 2)]
        return parts[0]

    def step(tt, carry):
        dirs = ((af_ref, sf_ref, vf_ref, yf_ref, tt), (ab_ref, sb_ref, vb_ref, yb_ref, SCAN_CHUNK - 1 - tt))
        for d, (a_ref, s_ref, v_ref, y_ref, tl) in enumerate(dirs):
            key_row = lambda ref, j, k: jnp.broadcast_to(ref[j, tl, pl.ds(k, 1), :], (SUBLANES, LANES))
            for first in range(0, nv, SCAN_BLOCKS):
                blocks = range(first, min(first + SCAN_BLOCKS, nv))
                sa = {vb: [None] * n_part for vb in blocks}
                for k in range(RW_HD):
                    kk = key_row(s_ref, 1, k)
                    for vb in blocks:
                        p = s_scr[d, vb, k] * kk
                        sa[vb][k % n_part] = p if k < n_part else sa[vb][k % n_part] + p
                sa = {vb: total(parts) for vb, parts in sa.items()}
                if lat:
                    vals = {0: v_ref[tl]}
                else:
                    vals = {vb: s_ref[2, tl, vb * SUBLANES:(vb + 1) * SUBLANES, :] for vb in blocks}
                y = {vb: [None] * n_part for vb in blocks}
                for k in range(RW_HD):
                    w, kd, bb = key_row(a_ref, 0, k), key_row(a_ref, 1, k), key_row(a_ref, 2, k)
                    r = key_row(s_ref, 0, k)
                    for vb in blocks:
                        sn = s_scr[d, vb, k] * w - sa[vb] * bb + vals[vb] * kd
                        s_scr[d, vb, k] = sn
                        y[vb][k % n_part] = sn * r if k < n_part else y[vb][k % n_part] + sn * r
                for vb in blocks:
                    if lat:
                        y_ref[tl] = total(y[vb])
                    else:
                        y_ref[tl, vb * SUBLANES:(vb + 1) * SUBLANES, :] = total(y[vb])
        return carry

    lax.fori_loop(0, SCAN_CHUNK, step, 0)

    if not lat:
        @pl.when(c == n_chunks - 1)
        def _():
            sfin_ref[...] = s_scr[...]


def _rw_scan(lat, ops, vv, s0):
    seq_len = ops.shape[1]
    nc = seq_len // SCAN_CHUNK
    nv = 1 if lat else RW_HD // SUBLANES
    n_shared = 2 if lat else 3
    s_shape = (2, nv, RW_HD, SUBLANES, LANES)
    state_spec = pl.BlockSpec(s_shape, lambda c: (0,) * 5)
    scratch = [pltpu.VMEM(s_shape, F32)]
    op_spec = lambda n, blk, rev: pl.BlockSpec(
        (n, SCAN_CHUNK, RW_HD, LANES), lambda c: (blk, (nc - 1 - c) if rev else c, 0, 0))
    in_specs = [op_spec(3, 0, False), op_spec(3, 1, True),
                op_spec(n_shared, 6 // n_shared, False), op_spec(n_shared, 6 // n_shared, True)]
    args = [ops] * 4
    if lat:
        y_spec = lambda rev: pl.BlockSpec((SCAN_CHUNK, SUBLANES, LANES), lambda c: ((nc - 1 - c) if rev else c, 0, 0))
        y_sds = jax.ShapeDtypeStruct((seq_len, SUBLANES, LANES), F32)
        in_specs += [y_spec(False), y_spec(True), state_spec]
        args += [vv, vv, s0]
        out_specs = [y_spec(False), y_spec(True)]
        out_shape = [y_sds, y_sds]
    else:
        y_spec = lambda rev: pl.BlockSpec((SCAN_CHUNK, RW_HD, LANES), lambda c: ((nc - 1 - c) if rev else c, 0, 0))
        y_sds = jax.ShapeDtypeStruct((seq_len, RW_HD, LANES), F32)
        out_specs = [y_spec(False), y_spec(True), state_spec]
        out_shape = [y_sds, y_sds, jax.ShapeDtypeStruct(s_shape, F32)]
    return pl.pallas_call(
        functools.partial(_rw_scan_kernel, lat, nv, nc),
        grid=(nc,),
        in_specs=in_specs,
        out_specs=out_specs,
        out_shape=out_shape,
        scratch_shapes=scratch,
        compiler_params=_params(("arbitrary",)),
        name="rw_scan_lat" if lat else "rw_scan_ctx",
    )(*args)


def _rw_post_ctx_kernel(nb, tt, yf_ref, yb_ref, g_ref, gb_ref, lnw_ref, lnb_ref, o_ref):
    for t in range(tt):
        y = yf_ref[t] + yb_ref[t]
        dev = y - jnp.mean(y, axis=0, keepdims=True)
        yn = (dev * lax.rsqrt(jnp.mean(dev * dev, axis=0, keepdims=True) + RW_GN_EPS)).T
        tok = jnp.concatenate([yn[h * nb:(h + 1) * nb, :] for h in range(RW_HEADS)], axis=1)
        o_ref[:, t, :] = (tok * lnw_ref[...] + lnb_ref[...]) * g_ref[:, t, :] + gb_ref[:, t, :]


def _rw_post_lat_kernel(nb, yf_ref, yb_ref, g_ref, gb_ref, lnw_ref, lnb_ref, hs_ref, o_ref):
    grp = LANES // nb
    ys = [yf_ref[:, vi, :] + yb_ref[:, vi, :] for vi in range(SUBLANES)]
    for b in range(nb):
        y = jnp.concatenate([yv[:, b * grp:(b + 1) * grp] for yv in ys], axis=1)
        dev = y - _sel_dot(y, hs_ref[...], EXACT_TERMS) * (1.0 / RW_HD)
        yn = dev * lax.rsqrt(_sel_dot(dev * dev, hs_ref[...], NORM_TERMS) * (1.0 / RW_HD) + RW_GN_EPS)
        o_ref[b] = (yn * lnw_ref[...] + lnb_ref[...]) * g_ref[b] + gb_ref[b]


def _rw_post(lat, yf, yb, g, gb, lnw, lnb, hs):
    nb, seq_len, _ = g.shape
    tt = LANES if lat else SUBLANES
    full = lambda a: pl.BlockSpec(a.shape, lambda i: (0,) * a.ndim)
    tok_spec = pl.BlockSpec((nb, tt, RW_W), lambda i: (0, i, 0))
    if lat:
        y_spec = pl.BlockSpec((tt, SUBLANES, LANES), lambda i: (i, 0, 0))
        body, extra = functools.partial(_rw_post_lat_kernel, nb), [hs]
    else:
        y_spec = pl.BlockSpec((tt, RW_HD, LANES), lambda i: (i, 0, 0))
        body, extra = functools.partial(_rw_post_ctx_kernel, nb, tt), []
    return pl.pallas_call(
        body,
        grid=(seq_len // tt,),
        in_specs=[y_spec, y_spec, tok_spec, tok_spec, full(lnw), full(lnb)] + [full(a) for a in extra],
        out_specs=tok_spec,
        out_shape=jax.ShapeDtypeStruct((nb, seq_len, RW_W), F32),
        compiler_params=_params(("parallel",)),
        name="rw_post_lat" if lat else "rw_post_ctx",
    )(yf, yb, g, gb, lnw, lnb, *extra)


def _rw_mixer(lat, u_rw, s0, prep_params, lnw, lnb, hs):
    nb = u_rw.shape[0]
    nvb = RW_HD // SUBLANES
    if lat:
        ops, vv, g, gb = _rw_prep(True, u_rw, *prep_params)
        grp = LANES // (nb * RW_HEADS)
        s0 = s0.reshape(nb, 2, RW_HEADS, grp, SUBLANES, RW_HD)
        s0 = jnp.transpose(s0, (1, 5, 4, 0, 2, 3)).reshape(2, 1, RW_HD, SUBLANES, LANES)
        yf, yb = _rw_scan(True, ops, vv, s0)
        return _rw_post(True, yf, yb, g, gb, lnw, lnb, hs), None
    ops, g, gb = _rw_prep(False, u_rw, *prep_params)
    yf, yb, sfin = _rw_scan(False, ops, None, None)
    sfin = sfin.reshape(2, nvb, RW_HD, SUBLANES, RW_HEADS, nb)
    sfin = jnp.transpose(sfin, (5, 0, 4, 1, 3, 2)).reshape(nb, 2, RW_HEADS, RW_HD, RW_HD)
    return _rw_post(False, yf, yb, g, gb, lnw, lnb, hs), sfin


def _softplus(x):
    return jnp.maximum(x, 0.0) + jnp.log1p(jnp.exp(-jnp.abs(x)))


def _ssd_kernel(reverse, n_chunks, u_ref, up_ref, un_ref, h0_ref, cw_ref, cb_ref, dtb_ref, a_ref,
                tri_ref, exp_ref, *rest):
    if reverse:
        yf_ref, nw_ref, out_ref, hfin_ref, h_scr = rest
    else:
        dsk_ref, out_ref, hfin_ref, h_scr = rest
    s = pl.program_id(1)
    c = (n_chunks - 1 - s) if reverse else s

    @pl.when(s == 0)
    def _():
        h_scr[...] = h0_ref[...]

    q = SSD_CHUNK
    u = u_ref[...]
    xbc = u[:, SSD_W:SSD_W + 512]
    row = _iota(xbc.shape, 0)
    has_prev = (c > 0).astype(F32)
    has_next = (c < n_chunks - 1).astype(F32)
    prev = jnp.where(row == 0, up_ref[SUBLANES - 1:SUBLANES, SSD_W:SSD_W + 512] * has_prev, pltpu.roll(xbc, 1, 0))
    nxt = jnp.where(row == q - 1, un_ref[0:1, SSD_W:SSD_W + 512] * has_next, pltpu.roll(xbc, q - 1, 0))
    xc = _silu(cw_ref[0:1, :] * prev + cw_ref[1:2, :] * xbc + cw_ref[2:3, :] * nxt + cb_ref[...])
    x = xc[:, :SSD_W]
    b_all = xc[:, SSD_W:SSD_W + LANES]
    c_all = xc[:, SSD_W + LANES:]

    dt = _softplus(u[:, SSD_W + 512:] + dtb_ref[...])
    dta = dt * a_ref[...]
    ii = _iota((q, q), 0)
    jj = _iota((q, q), 1)
    tri = (jj >= ii) if reverse else (jj <= ii)
    cs = _sel_dot(dta, tri_ref[...], EXACT_TERMS, sel_first=True)
    cs_t = cs.T
    dt_e = _sel_dot(dt, exp_ref[...], EXACT_TERMS)
    cs_e = _sel_dot(cs, exp_ref[...], EXACT_TERMS)
    tot = cs_e[0:1, :] if reverse else cs_e[q - 1:q, :]
    xdt = x * dt_e
    dec_in = jnp.exp(tot - cs_e)
    dec_out = jnp.exp(cs_e)
    cdec = jnp.exp(tot)
    lane = _iota((q, LANES), 1)
    rowc = _iota((LANES, 1), 0)
    ys = []
    for g in range(2):
        bg = b_all[:, g * 64:(g + 1) * 64]
        cg = c_all[:, g * 64:(g + 1) * 64]
        gm = _bdot_nt(cg, bg)
        sl = slice(g * LANES, (g + 1) * LANES)
        xdt_g = xdt[:, sl]
        yd = []
        for hh in range(2):
            h = 2 * g + hh
            diff = cs[:, h:h + 1] - cs_t[h:h + 1, :]
            lm = jnp.exp(jnp.where(tri, diff, -jnp.inf))
            yd.append(_bdot(gm * lm, xdt_g))
        hg = h_scr[g]
        y = jnp.where(lane < 64, yd[0], yd[1]) + _bdot_nt(cg, hg) * dec_out[:, sl]
        st = _bdot((xdt_g * dec_in[:, sl]).T, bg)
        cd = cdec[:, sl]
        h_new = hg * jnp.where(rowc < 64, cd[:, 0:1], cd[:, 64:65]) + st
        h_scr[g] = h_new
        hfin_ref[g] = h_new
        ys.append(y)
    y = jnp.concatenate(ys, axis=1)
    if reverse:
        val = (yf_ref[...] + y) * _silu(u[:, :SSD_W])
        out_ref[...] = _rms(val, nw_ref[...])
    else:
        out_ref[...] = y + dsk_ref[...] * x


def _ssd_sweep(reverse, u_ssd, nb, seq_len, h0, cw, cb, dtb, a_neg, extra):
    nc = seq_len // SSD_CHUNK
    sub_per_chunk = SSD_CHUNK // SUBLANES
    last_sub = u_ssd.shape[0] // SUBLANES - 1
    chunk = (lambda s: nc - 1 - s) if reverse else (lambda s: s)
    rowblk = lambda b, s: b * nc + chunk(s)
    full = lambda a: pl.BlockSpec(a.shape, lambda b, s: (0,) * a.ndim)
    steps = np.arange(SSD_CHUNK)
    tri = jnp.asarray((steps[None, :] >= steps[:, None]) if reverse else (steps[None, :] <= steps[:, None]), BF16)
    expand = jnp.asarray(np.arange(LANES)[:, None] == np.arange(SSD_W)[None, :] // 64, BF16)
    in_specs = [
        pl.BlockSpec((SSD_CHUNK, SSD_PAD), lambda b, s: (rowblk(b, s), 0)),
        pl.BlockSpec((SUBLANES, SSD_PAD), lambda b, s: (jnp.maximum(rowblk(b, s) * sub_per_chunk - 1, 0), 0)),
        pl.BlockSpec((SUBLANES, SSD_PAD),
                     lambda b, s: (jnp.minimum((rowblk(b, s) + 1) * sub_per_chunk, last_sub), 0)),
        pl.BlockSpec((None, 2, LANES, 64), lambda b, s: (b, 0, 0, 0)),
        full(cw), full(cb), full(dtb), full(a_neg), full(tri), full(expand),
    ]
    args = [u_ssd, u_ssd, u_ssd, h0, cw, cb, dtb, a_neg, tri, expand]
    if reverse:
        yf, nw = extra
        in_specs += [pl.BlockSpec((SSD_CHUNK, SSD_W), lambda b, s: (rowblk(b, s), 0)), full(nw)]
        args += [yf, nw]
    else:
        in_specs += [full(extra)]
        args += [extra]
    return pl.pallas_call(
        functools.partial(_ssd_kernel, reverse, nc),
        grid=(nb, nc),
        in_specs=in_specs,
        out_specs=[pl.BlockSpec((SSD_CHUNK, SSD_W), lambda b, s: (rowblk(b, s), 0)),
                   pl.BlockSpec((None, 2, LANES, 64), lambda b, s: (b, 0, 0, 0))],
        out_shape=[jax.ShapeDtypeStruct((nb * seq_len, SSD_W), F32),
                   jax.ShapeDtypeStruct((nb, 2, LANES, 64), F32)],
        scratch_shapes=[pltpu.VMEM((2, LANES, 64), F32)],
        compiler_params=_params(("parallel", "arbitrary")),
        name="ssd_bwd" if reverse else "ssd_fwd",
    )(*args)


def _ssd_mixer(u_ssd, nb, seq_len, h0, cw, cb, dtb, a_neg, dsk, nw):
    yf, hf = _ssd_sweep(False, u_ssd, nb, seq_len, h0[:, 0], cw, cb, dtb[0], a_neg[0], dsk)
    out, hb = _ssd_sweep(True, u_ssd, nb, seq_len, h0[:, 1], cw, cb, dtb[1], a_neg[1], (yf, nw))
    return out, jnp.stack([hf, hb], axis=1).reshape(nb, 2, 4, 64, 64)


def _route(scores, bias):
    n_grp = N_EXPERTS // EXP_PER_GROUP
    sel = scores + bias
    sc = [scores[e:e + 1, :] for e in range(N_EXPERTS)]
    sl = [sel[e:e + 1, :] for e in range(N_EXPERTS)]
    as_int = lambda cond: jnp.where(cond, 1, 0)
    top2, grp = [], []
    for g in range(n_grp):
        members = range(g * EXP_PER_GROUP, (g + 1) * EXP_PER_GROUP)
        total = None
        for e in members:
            ahead = [as_int(sl[o] >= sl[e]) if o < e else as_int(sl[o] > sl[e]) for o in members if o != e]
            top2.append(ahead[0] + ahead[1] + ahead[2] < 2)
            kept = jnp.where(top2[e], sl[e], 0.0)
            total = kept if total is None else total + kept
        grp.append(total)
    rows = []
    for g in range(n_grp):
        beaten = [as_int(grp[o] >= grp[g]) if o < g else as_int(grp[o] > grp[g]) for o in range(n_grp) if o != g]
        best = beaten[0] + beaten[1] + beaten[2] == 0
        members = range(g * EXP_PER_GROUP, (g + 1) * EXP_PER_GROUP)
        chosen = [jnp.where(best, as_int(top2[e]), 0) > 0 for e in members]
        cw = [jnp.where(c, sc[e], 0.0) for c, e in zip(chosen, members)]
        den = cw[0] + cw[1] + cw[2] + cw[3]
        rows += [jnp.where(c, w / jnp.where(c, den, 1.0), 0.0) for c, w in zip(chosen, cw)]
    return jnp.concatenate(rows, axis=0)


def _outproj_kernel(oatt_ref, orw_ref, ossd_ref, x_ref, mod_ref, wout_ref,
                    n2_ref, rw_ref, rb_ref, x1_ref, h2_ref, comb_ref):
    o = jnp.concatenate([oatt_ref[...], orw_ref[...], ossd_ref[...]], axis=1)
    x1 = x_ref[...] + mod_ref[2] * _bdot(o, wout_ref[...])
    h2 = _rms(x1, n2_ref[...]) * (1.0 + mod_ref[4]) + mod_ref[3]
    comb = _route(_sigmoid(_bdot_nt(rw_ref[...], h2)), rb_ref[...])
    x1_ref[...] = x1
    h2_ref[...] = h2.astype(BF16)
    comb_ref[...] = jnp.concatenate([comb, jnp.zeros((LANES - N_EXPERTS, comb.shape[1]), F32)], axis=0).T


def _outproj(o_att, o_rw, o_ssd, x, mod_l, row_fn, w_out_bf, n2, router_w, router_b):
    n_tok = x.shape[0]
    tile = lambda w: pl.BlockSpec((TOKEN_TILE, w), lambda i: (i, 0))
    full = lambda a: pl.BlockSpec(a.shape, lambda i: (0,) * a.ndim)
    return pl.pallas_call(
        _outproj_kernel,
        grid=(n_tok // TOKEN_TILE,),
        in_specs=[tile(ATT_W), tile(RW_W), tile(SSD_W), tile(D_MODEL),
                  _mod_spec(row_fn)] + [full(a) for a in (w_out_bf, n2, router_w, router_b)],
        out_specs=[tile(D_MODEL), tile(D_MODEL), tile(LANES)],
        out_shape=[jax.ShapeDtypeStruct((n_tok, D_MODEL), F32),
                   jax.ShapeDtypeStruct((n_tok, D_MODEL), BF16),
                   jax.ShapeDtypeStruct((n_tok, LANES), F32)],
        compiler_params=_params(("parallel",)),
        name="outproj",
    )(o_att, o_rw, o_ssd, x, mod_l, w_out_bf, n2, router_w, router_b)


def _moe_kernel(h2_ref, comb_ref, x1_ref, mod_ref, wg_ref, wu_ref, wd_ref, o_ref, acc_ref):
    e = pl.program_id(1)

    @pl.when(e == 0)
    def _():
        acc_ref[...] = jnp.zeros_like(acc_ref)

    h2 = h2_ref[...]
    hid = _silu(jnp.dot(h2, wg_ref[...], preferred_element_type=F32)) * jnp.dot(
        h2, wu_ref[...], preferred_element_type=F32)
    comb = comb_ref[...]
    wcol = jnp.sum(jnp.where(_iota(comb.shape, 1) == e, comb, 0.0), axis=1, keepdims=True)
    acc_ref[...] += _bdot(hid * wcol, wd_ref[...])

    @pl.when(e == N_EXPERTS - 1)
    def _():
        o_ref[...] = x1_ref[...] + mod_ref[5] * acc_ref[...]


def _moe(h2, comb, x1, mod_l, row_fn, wg_bf, wu_bf, wd_bf):
    n_tok = x1.shape[0]
    tile = lambda w: pl.BlockSpec((MOE_TILE, w), lambda i, e: (i, 0))
    return pl.pallas_call(
        _moe_kernel,
        grid=(n_tok // MOE_TILE, N_EXPERTS),
        in_specs=[tile(D_MODEL), tile(LANES), tile(D_MODEL), _mod_spec(row_fn),
                  pl.BlockSpec((None, D_MODEL, D_FF), lambda i, e: (e, 0, 0)),
                  pl.BlockSpec((None, D_MODEL, D_FF), lambda i, e: (e, 0, 0)),
                  pl.BlockSpec((None, D_FF, D_MODEL), lambda i, e: (e, 0, 0))],
        out_specs=tile(D_MODEL),
        out_shape=jax.ShapeDtypeStruct((n_tok, D_MODEL), F32),
        scratch_shapes=[pltpu.VMEM((MOE_TILE, D_MODEL), F32)],
        compiler_params=_params(("parallel", "arbitrary")),
        name="moe",
    )(h2, comb, x1, mod_l, wg_bf, wu_bf, wd_bf)


def _rope_tables(seq_len):
    t = np.arange(seq_len)
    pos = np.stack([t // GRID_W, t % GRID_W], 0).astype(np.float32)
    n_freq = HEAD_DIM // 4
    inv = jnp.asarray(ROPE_BASE, F32) ** (-jnp.arange(n_freq, dtype=F32) / n_freq)
    ang = jnp.asarray(pos)[:, :, None] * inv
    cos, sin = jnp.cos(ang), jnp.sin(ang)
    cos_h = jnp.concatenate([cos[0], cos[0], cos[1], cos[1]], -1)
    sin_h = jnp.concatenate([-sin[0], sin[0], -sin[1], sin[1]], -1)
    return jnp.tile(cos_h, (1, 2)), jnp.tile(sin_h, (1, 2))


def _pad_rows(w, start, total):
    return jnp.pad(w, ((0, 0),) * (w.ndim - 2) + ((start, total - start - w.shape[-2]), (0, 0)))


def _value_order(x, axis, start, grp):
    if grp is None:
        return x
    axis %= x.ndim
    sl = lambda a, b: lax.slice_in_dim(x, a, b, axis=axis)
    mid = sl(start, start + RW_W)
    shape = mid.shape
    mid = mid.reshape(shape[:axis] + (RW_HEADS, grp, RW_HD // grp) + shape[axis + 1:])
    order = tuple(range(axis)) + (axis + 2, axis, axis + 1) + tuple(range(axis + 3, mid.ndim))
    mid = jnp.transpose(mid, order).reshape(shape)
    return jnp.concatenate([sl(0, start), mid, sl(start + RW_W, x.shape[axis])], axis=axis)


def _rw_layer_params(l, P, grp):
    row = lambda a: a.reshape(1, -1)
    v0 = 2 * RW_W
    w_in = jnp.pad(P["w_in"][l], ((0, 0), (0, SSD_PAD - SSD_COLS))).astype(BF16)
    head = np.arange(RW_W) // RW_HD if grp is None else (np.arange(RW_W) // grp) % RW_HEADS
    rp = {}
    rp["w_in"] = _value_order(w_in, 1, ATT_COLS + v0, grp)
    rp["w_out"] = _value_order(P["w_out"][l].astype(BF16), 0, ATT_W, grp)
    rp["prep"] = (
        _value_order(P["rw_mu"][l], 1, v0, grp),
        _pad_rows(_value_order(P["rw_g_up"][l], 1, 0, grp), 64, LANES),
        _pad_rows(P["rw_w_up"][l], 0, LANES),
        _pad_rows(P["rw_a_up"][l], 32, LANES),
        P["rw_w0"][l], P["rw_a0"][l],
        row(P["rw_k_k"][l]), row(P["rw_k_a"][l]), row(P["rw_r_k"][l]),
        _head_selector(np.arange(RW_W) // RW_HD), _head_selector(np.arange(RW_W) // RW_HD, head),
    )
    rp["lnw"] = _value_order(row(P["rw_ln_w"][l]), 1, 0, grp)
    rp["lnb"] = _value_order(row(P["rw_ln_b"][l]), 1, 0, grp)
    rp["hs"] = _head_selector(head)
    return rp


def _layer_params(l, P):
    row = lambda a: a.reshape(1, -1)
    lp = {}
    lp["n1"] = row(P["norm1_w"][l])
    lp["n2"] = row(P["norm2_w"][l])
    lp["qw"] = row(jnp.tile(P["q_norm_w"][l], ATT_HEADS))
    lp["kw"] = row(jnp.tile(P["k_norm_w"][l], ATT_KV_HEADS))
    lp["sink"] = P["attn_sink"][l]
    pad_heads = lambda a: jnp.pad(a, ((0, 0), (0, LANES - a.shape[1]))).reshape(2, 1, LANES)
    lp["ssd"] = (
        P["ssd_conv_w"][l], row(P["ssd_conv_b"][l]),
        pad_heads(P["ssd_dt_bias"][l]), pad_heads(-jnp.exp(P["ssd_a_log"][l])),
        row(jnp.repeat(P["ssd_d"][l], 64)), row(P["ssd_norm_w"][l]),
    )
    lp["wg"] = P["exp_gate"][l].astype(BF16)
    lp["wu"] = P["exp_up"][l].astype(BF16)
    lp["wd"] = P["exp_down"][l].astype(BF16)
    return lp


def _trunk(x, nb, seq_len, is_ctx, mod, layers, rw_layers, router, cache_k, cache_v, state_rwkv, state_ssd):
    tiles_per_seq = seq_len // TOKEN_TILE
    moe_per_seq = max(seq_len // MOE_TILE, 1)
    if is_ctx:
        row_tok = lambda i: 0
        row_moe = lambda i: 0
        rope = None
    else:
        row_tok = lambda i: 1 + i // tiles_per_seq
        row_moe = lambda i: 1 + i // moe_per_seq
        rope = _rope_tables(seq_len)
    new_k, new_v, new_rw, new_ssd = [], [], [], []
    for l, (lp, rp) in enumerate(zip(layers, rw_layers)):
        q, k, v, u_rw, u_ssd = _inproj(x, mod[l], row_tok, lp["n1"], rp["w_in"], lp["qw"], lp["kw"], rope, seq_len)
        if is_ctx:
            o_att = _attn_ctx(lp["sink"], q, k, v, nb, seq_len)
            rw0 = None
            ssd0 = jnp.zeros((nb, 2, 2, LANES, 64), F32)
        else:
            kc = cache_k[:, l].reshape(nb, -1, KV_W)
            vc = cache_v[:, l].reshape(nb, -1, KV_W)
            o_att = _attn_lat(lp["sink"], q, k, v, kc, vc, nb, seq_len)
            rw0 = state_rwkv[:, l]
            ssd0 = state_ssd[:, l].reshape(nb, 2, 2, LANES, 64)
        o_rw, s_rw = _rw_mixer(not is_ctx, u_rw.reshape(nb, seq_len, RW_COLS), rw0, rp["prep"],
                               rp["lnw"], rp["lnb"], rp["hs"])
        o_ssd, s_ssd = _ssd_mixer(u_ssd, nb, seq_len, ssd0, *lp["ssd"])
        if is_ctx:
            new_k.append(k.reshape(nb, seq_len, ATT_KV_HEADS, HEAD_DIM))
            new_v.append(v.reshape(nb, seq_len, ATT_KV_HEADS, HEAD_DIM))
            new_rw.append(s_rw)
            new_ssd.append(s_ssd)
        x1, h2, comb = _outproj(o_att, o_rw.reshape(nb * seq_len, RW_W), o_ssd, x, mod[l], row_tok, rp["w_out"],
                                lp["n2"], *router)
        x = _moe(h2, comb, x1, mod[l], row_moe, lp["wg"], lp["wu"], lp["wd"])
    if is_ctx:
        return x, tuple(jnp.stack(t, 1) for t in (new_k, new_v, new_rw, new_ssd))
    return x, None


def kernel(x_prompt, x_sample, cache_k, cache_v, state_rwkv, state_ssd, c, c_ctx, w_mod, b_mod, norm1_w, norm2_w, w_in, w_out, q_norm_w, k_norm_w, attn_sink, rw_mu, rw_w0, rw_w_up, rw_a0, rw_a_up, rw_g_up, rw_k_k, rw_k_a, rw_r_k, rw_ln_w, rw_ln_b, ssd_conv_w, ssd_conv_b, ssd_dt_bias, ssd_a_log, ssd_d, ssd_norm_w, router_w, router_bias, exp_gate, exp_up, exp_down):
    P = dict(norm1_w=norm1_w, norm2_w=norm2_w, w_in=w_in, w_out=w_out, q_norm_w=q_norm_w, k_norm_w=k_norm_w,
             attn_sink=attn_sink, rw_mu=rw_mu, rw_w0=rw_w0, rw_w_up=rw_w_up, rw_a0=rw_a0, rw_a_up=rw_a_up,
             rw_g_up=rw_g_up, rw_k_k=rw_k_k, rw_k_a=rw_k_a, rw_r_k=rw_r_k, rw_ln_w=rw_ln_w, rw_ln_b=rw_ln_b,
             ssd_conv_w=ssd_conv_w, ssd_conv_b=ssd_conv_b, ssd_dt_bias=ssd_dt_bias, ssd_a_log=ssd_a_log,
             ssd_d=ssd_d, ssd_norm_w=ssd_norm_w, exp_gate=exp_gate, exp_up=exp_up, exp_down=exp_down)
    nb_ctx, seq_ctx, _ = x_prompt.shape
    nb_lat, seq_lat, _ = x_sample.shape
    assert nb_lat + 1 <= SUBLANES and seq_ctx % TOKEN_TILE == 0 and seq_lat % MOE_TILE == 0
    assert nb_ctx * RW_HEADS == LANES and LANES % (nb_lat * RW_HEADS) == 0
    cond8 = jnp.zeros((SUBLANES, D_MODEL), F32).at[0].set(c_ctx).at[1:1 + nb_lat].set(c)
    mod = _modulation(cond8, w_mod, b_mod)
    layers = [_layer_params(l, P) for l in range(DEPTH)]
    rw_ctx = [_rw_layer_params(l, P, None) for l in range(DEPTH)]
    rw_lat = [_rw_layer_params(l, P, LANES // (nb_lat * RW_HEADS)) for l in range(DEPTH)]
    router = (router_w.T.astype(BF16), router_bias.reshape(N_EXPERTS, 1))
    y_prompt, ctx_state = _trunk(x_prompt.reshape(-1, D_MODEL), nb_ctx, seq_ctx, True, mod, layers, rw_ctx, router,
                                 None, None, None, None)
    y_sample, _ = _trunk(x_sample.reshape(-1, D_MODEL), nb_lat, seq_lat, False, mod, layers, rw_lat, router,
                         cache_k, cache_v, state_rwkv, state_ssd)
    return (y_prompt.reshape(x_prompt.shape), y_sample.reshape(x_sample.shape)) + ctx_state
```

```python
import functools
import math

import jax
import jax.numpy as jnp
import numpy as np
from jax import lax
from jax.experimental import pallas as pl
from jax.experimental.pallas import tpu as pltpu

F32 = jnp.float32
BF16 = jnp.bfloat16
HIGHEST = lax.Precision.HIGHEST

D_MODEL = 1024
DEPTH = 2
GRID_W = 64
EPS = 1e-6
ATT_HEADS = 8
ATT_KV_HEADS = 2
HEAD_DIM = 64
ATT_W = 512
KV_W = 128
ATT_COLS = 768
WINDOW = 128
BLK = 128
ROPE_BASE = 10000.0
RW_HEADS = 4
RW_HD = 64
RW_W = 256
RW_COLS = 896
RW_DECAY_SCALE = 0.606531
RW_GN_EPS = 64e-5
SSD_W = 256
SSD_CHUNK = 128
SSD_CHUNKS_PER_STEP = 2
SSD_COLS = 772
SSD_PAD = 896
IN_PAD = ATT_COLS + RW_COLS + SSD_PAD
N_EXPERTS = 16
EXP_PER_GROUP = 4
D_FF = 256

LANES = 128
SUBLANES = 8
TOKEN_TILE = 256
MOE_TILE = 1024
MOE_EXPERTS_PER_STEP = 4
SCAN_CHUNK = 16
SCAN_BLOCKS = 4
SCAN_CHAINS = 8
VMEM_LIMIT = 48 * 1024 * 1024


def _params(sem):
    return pltpu.CompilerParams(dimension_semantics=sem, vmem_limit_bytes=VMEM_LIMIT)


def _bdot(a, b):
    return jnp.dot(a.astype(BF16), b.astype(BF16), preferred_element_type=F32)


def _bdot_nt(a, b):
    return lax.dot_general(a.astype(BF16), b.astype(BF16), (((1,), (1,)), ((), ())),
                           preferred_element_type=F32)


def _sel_dot(x, sel, terms, sel_first=False):
    acc = None
    rem = x
    for i in range(terms):
        piece = rem.astype(BF16)
        part = jnp.dot(sel, piece, preferred_element_type=F32) if sel_first else jnp.dot(
            piece, sel, preferred_element_type=F32)
        acc = part if acc is None else acc + part
        if i + 1 < terms:
            rem = rem - piece.astype(F32)
    return acc


def _iota(shape, axis):
    return lax.broadcasted_iota(jnp.int32, shape, axis)


def _head_selector(heads_in, heads_out=None):
    heads_out = heads_in if heads_out is None else heads_out
    return jnp.asarray(np.asarray(heads_in)[:, None] == np.asarray(heads_out)[None, :], BF16)


NORM_TERMS = 2
EXACT_TERMS = 3


def _sigmoid(x):
    return 1.0 / (1.0 + jnp.exp(-x))


def _silu(x):
    return x * _sigmoid(x)


def _rms(x, w):
    return x * lax.rsqrt(jnp.mean(x * x, -1, keepdims=True) + EPS) * w


def _mod_kernel(c_ref, w_ref, b_ref, o_ref):
    o_ref[...] = _bdot(_silu(c_ref[...]), w_ref[...]) + b_ref[...]


def _modulation(cond8, w_mod, b_mod):
    out = pl.pallas_call(
        _mod_kernel,
        grid=(DEPTH, 6),
        in_specs=[
            pl.BlockSpec((SUBLANES, D_MODEL), lambda l, j: (0, 0)),
            pl.BlockSpec((None, D_MODEL, D_MODEL), lambda l, j: (l, 0, j)),
            pl.BlockSpec((None, 1, D_MODEL), lambda l, j: (l, 0, j)),
        ],
        out_specs=pl.BlockSpec((None, None, SUBLANES, D_MODEL), lambda l, j: (l, j, 0, 0)),
        out_shape=jax.ShapeDtypeStruct((DEPTH, 6, SUBLANES, D_MODEL), F32),
        compiler_params=_params(("arbitrary", "arbitrary")),
        name="modulation",
    )(cond8, w_mod, b_mod.reshape(DEPTH, 1, 6 * D_MODEL))
    return out.reshape(DEPTH, 6, SUBLANES, 1, D_MODEL)


def _mod_spec(row_fn):
    return pl.BlockSpec((6, None, 1, D_MODEL), lambda *idx: (0, row_fn(idx[0]), 0, 0))


def _swap16(x):
    w = x.shape[1]
    first = (_iota(x.shape, 1) & 31) < 16
    return jnp.where(first, pltpu.roll(x, w - 16, 1), pltpu.roll(x, 16, 1))


def _inproj_kernel(use_rope, x_ref, mod_ref, n1_ref, w_ref, qw_ref, kw_ref, hq_ref, *rest):
    if use_rope:
        cos_ref, sin_ref, q_ref, k_ref, v_ref, urw_ref, ussd_ref = rest
    else:
        q_ref, k_ref, v_ref, urw_ref, ussd_ref = rest
    h = _rms(x_ref[...], n1_ref[...]) * (1.0 + mod_ref[1]) + mod_ref[0]
    u = _bdot(h, w_ref[...])
    q = u[:, :ATT_W]
    k = u[:, ATT_W:ATT_W + KV_W]
    q = q * lax.rsqrt(_sel_dot(q * q, hq_ref[...], NORM_TERMS) * (1.0 / HEAD_DIM) + EPS) * qw_ref[...]
    k = k * lax.rsqrt(_sel_dot(k * k, hq_ref[:KV_W, :KV_W], NORM_TERMS) * (1.0 / HEAD_DIM) + EPS) * kw_ref[...]
    if use_rope:
        cos = cos_ref[...]
        sin = sin_ref[...]
        k = k * cos + _swap16(k) * sin
        cos4 = jnp.concatenate([cos] * 4, axis=1)
        sin4 = jnp.concatenate([sin] * 4, axis=1)
        q = q * cos4 + _swap16(q) * sin4
    q_ref[...] = q
    k_ref[...] = k
    v_ref[...] = u[:, ATT_W + KV_W:ATT_COLS]
    urw_ref[...] = u[:, ATT_COLS:ATT_COLS + RW_COLS]
    ussd_ref[...] = u[:, ATT_COLS + RW_COLS:]


def _inproj(x, mod_l, row_fn, n1, w_in_bf, qw, kw, rope, seq_len):
    n_tok = x.shape[0]
    tiles_per_seq = seq_len // TOKEN_TILE
    in_specs = [
        pl.BlockSpec((TOKEN_TILE, D_MODEL), lambda i: (i, 0)),
        _mod_spec(row_fn),
        pl.BlockSpec((1, D_MODEL), lambda i: (0, 0)),
        pl.BlockSpec((D_MODEL, IN_PAD), lambda i: (0, 0)),
        pl.BlockSpec((1, ATT_W), lambda i: (0, 0)),
        pl.BlockSpec((1, KV_W), lambda i: (0, 0)),
        pl.BlockSpec((ATT_W, ATT_W), lambda i: (0, 0)),
    ]
    args = [x, mod_l, n1, w_in_bf, qw, kw, _head_selector(np.arange(ATT_W) // HEAD_DIM)]
    if rope is not None:
        in_specs += [pl.BlockSpec((TOKEN_TILE, LANES), lambda i: (i % tiles_per_seq, 0))] * 2
        args += list(rope)
    widths = (ATT_W, KV_W, KV_W, RW_COLS, SSD_PAD)
    return pl.pallas_call(
        functools.partial(_inproj_kernel, rope is not None),
        grid=(n_tok // TOKEN_TILE,),
        in_specs=in_specs,
        out_specs=[pl.BlockSpec((TOKEN_TILE, w), lambda i: (i, 0)) for w in widths],
        out_shape=[jax.ShapeDtypeStruct((n_tok, w), F32) for w in widths],
        compiler_params=_params(("parallel",)),
        name="inproj",
    )(*args)


LOG2E = 1.4426950408889634


def _attend_heads(sink_ref, q, parts):
    groups = ATT_HEADS // ATT_KV_HEADS
    outs = []
    for hk in range(ATT_KV_HEADS):
        cols = slice(hk * HEAD_DIM, (hk + 1) * HEAD_DIM)
        keys = [k[:, cols].astype(BF16) for k, _, _ in parts]
        vals = [v[:, cols].astype(BF16) for _, v, _ in parts]
        for g in range(groups):
            hd = hk * groups + g
            qh = q[:, hd * HEAD_DIM:(hd + 1) * HEAD_DIM].astype(BF16)
            sink = sink_ref[hd] * LOG2E
            scores = []
            for kh, (_, _, mask) in zip(keys, parts):
                s = lax.dot_general(qh, kh, (((1,), (1,)), ((), ())),
                                    preferred_element_type=F32) * (HEAD_DIM ** -0.5 * LOG2E)
                scores.append(s if mask is None else jnp.where(mask, s, -1e30))
            m = jnp.max(scores[0], -1, keepdims=True)
            for s in scores[1:]:
                m = jnp.maximum(m, jnp.max(s, -1, keepdims=True))
            m = jnp.maximum(m, sink)
            den = jnp.exp2(sink - m)
            o = None
            for s, vh in zip(scores, vals):
                p = jnp.exp2(s - m)
                den = den + jnp.sum(p, -1, keepdims=True)
                pv = jnp.dot(p.astype(BF16), vh, preferred_element_type=F32)
                o = pv if o is None else o + pv
            outs.append(o / den)
    return jnp.concatenate(outs, axis=1)


def _attn_ctx_kernel(sink_ref, q_ref, k_ref, v_ref, o_ref):
    o_ref[...] = _attend_heads(sink_ref, q_ref[...], [(k_ref[...], v_ref[...], None)])


def _attn_ctx(sink, q, k, v, nb, seq_len):
    return pl.pallas_call(
        _attn_ctx_kernel,
        grid=(nb,),
        in_specs=[
            pl.BlockSpec(memory_space=pltpu.SMEM),
            pl.BlockSpec((seq_len, ATT_W), lambda b: (b, 0)),
            pl.BlockSpec((seq_len, KV_W), lambda b: (b, 0)),
            pl.BlockSpec((seq_len, KV_W), lambda b: (b, 0)),
        ],
        out_specs=pl.BlockSpec((seq_len, ATT_W), lambda b: (b, 0)),
        out_shape=jax.ShapeDtypeStruct((nb * seq_len, ATT_W), F32),
        compiler_params=_params(("parallel",)),
        name="attn_ctx",
    )(sink, q, k, v)


def _attn_lat_kernel(seq_len, sink_ref, q_ref, k_ref, v_ref, kc_ref, vc_ref, o_ref):
    i = pl.program_id(1)
    band = 3 * BLK
    start = pl.multiple_of(jnp.clip((i - 1) * BLK, 0, seq_len - band), BLK)
    dist = jnp.abs(i * BLK + _iota((BLK, band), 0) - (start + _iota((BLK, band), 1)))
    parts = [(k_ref[pl.ds(start, band), :], v_ref[pl.ds(start, band), :], dist <= WINDOW),
             (kc_ref[...], vc_ref[...], None)]
    o_ref[...] = _attend_heads(sink_ref, q_ref[...], parts)


def _attn_lat(sink, q, k, v, kc, vc, nb, seq_len):
    nblk = seq_len // BLK
    past = kc.shape[1]
    return pl.pallas_call(
        functools.partial(_attn_lat_kernel, seq_len),
        grid=(nb, nblk),
        in_specs=[
            pl.BlockSpec(memory_space=pltpu.SMEM),
            pl.BlockSpec((BLK, ATT_W), lambda b, i: (b * nblk + i, 0)),
            pl.BlockSpec((seq_len, KV_W), lambda b, i: (b, 0)),
            pl.BlockSpec((seq_len, KV_W), lambda b, i: (b, 0)),
            pl.BlockSpec((None, past, KV_W), lambda b, i: (b, 0, 0)),
            pl.BlockSpec((None, past, KV_W), lambda b, i: (b, 0, 0)),
        ],
        out_specs=pl.BlockSpec((BLK, ATT_W), lambda b, i: (b * nblk + i, 0)),
        out_shape=jax.ShapeDtypeStruct((nb * seq_len, ATT_W), F32),
        compiler_params=_params(("parallel", "arbitrary")),
        name="attn_lat",
    )(sink, q, k, v, kc, vc)


RW_OPS = ("w_f", "kd_f", "b_f", "w_b", "kd_b", "b_b", "r", "kk")


def _rw_prep_kernel(lat, nb, tt, n_tiles, u_ref, up_ref, un_ref, mu_ref, wg_ref, ww_ref, wa_ref,
                    w0_ref, a0_ref, kk_ref, ka_ref, rk_ref, ones_ref, hsum_ref, *rest):
    if lat:
        ops_ref, vv_ref, g_ref, gb_ref, xs = rest
    else:
        ops_ref, g_ref, gb_ref, xs, xv = rest
    i = pl.program_id(0)
    has_prev = (i > 0).astype(F32)
    has_next = (i < n_tiles - 1).astype(F32)
    row = _iota((tt, RW_COLS), 0)
    mu0 = mu_ref[0:1, :]
    mu1 = mu_ref[1:2, :]
    mixed = []
    for b in range(nb):
        ub = u_ref[b]
        prev = jnp.where(row == 0, up_ref[b, SUBLANES - 1:SUBLANES, :] * has_prev, pltpu.roll(ub, 1, 0))
        nxt = jnp.where(row == tt - 1, un_ref[b, 0:1, :] * has_next, pltpu.roll(ub, tt - 1, 0))
        mixed.append(ub + mu0 * (prev - ub) + mu1 * (nxt - ub))
    u = jnp.concatenate(mixed, axis=0)
    r = u[:, :RW_W]
    k = u[:, RW_W:2 * RW_W]
    v = u[:, 2 * RW_W:3 * RW_W]
    low = u[:, 3 * RW_W:]
    kk = k * kk_ref[...]
    kk = kk * lax.rsqrt(_sel_dot(kk * kk, ones_ref[...], NORM_TERMS) + 1e-12)
    tanh_low = jnp.tanh(low)
    vals = [None] * len(RW_OPS)
    for d in range(2):
        w = jnp.exp(-RW_DECAY_SCALE * _sigmoid(w0_ref[d:d + 1, :] + _bdot(tanh_low, ww_ref[d])))
        a = _sigmoid(a0_ref[d:d + 1, :] + _bdot(low, wa_ref[d]))
        vals[3 * d:3 * d + 3] = [w, k * (1.0 + (a - 1.0) * ka_ref[...]), kk * a]
    vals[6:8] = [r, kk]
    g = _bdot(_sigmoid(low), wg_ref[...])
    gb = _sel_dot(r * k * rk_ref[...], hsum_ref[...], EXACT_TERMS) * v * g
    for b in range(nb):
        g_ref[b] = g[b * tt:(b + 1) * tt]
        gb_ref[b] = gb[b * tt:(b + 1) * tt]

    n_pairs = len(RW_OPS) // 2
    low_half = _iota((nb * tt, LANES), 1) < RW_HD
    for p in range(n_pairs):
        for hp in range(2):
            first = vals[2 * p][:, hp * LANES:(hp + 1) * LANES]
            second = vals[2 * p + 1][:, hp * LANES:(hp + 1) * LANES]
            xs[p, 2 * hp] = jnp.where(low_half, first, pltpu.roll(second, RW_HD, 1))
            xs[p, 2 * hp + 1] = jnp.where(low_half, pltpu.roll(first, RW_HD, 1), second)

    def emit(p, t, tiles):
        both = jnp.concatenate(tiles, axis=0).T
        ops_ref[2 * p, t] = both[:RW_HD]
        ops_ref[2 * p + 1, t] = both[RW_HD:]

    if lat:
        def relayout(t, carry):
            for p in range(n_pairs):
                emit(p, t, [jnp.broadcast_to(xs[p, h, pl.ds(b * tt + t, 1), :], (SUBLANES, LANES))
                            for b in range(nb) for h in range(RW_HEADS)])
            return carry

        lax.fori_loop(0, tt, relayout, 0)
        grp = LANES // nb
        for vi in range(SUBLANES):
            vv_ref[:, vi, :] = jnp.concatenate(
                [v[b * tt:(b + 1) * tt, vi * grp:(vi + 1) * grp] for b in range(nb)], axis=1)
    else:
        xv[0] = v[:, :LANES]
        xv[1] = v[:, LANES:]
        for t in range(tt):
            at_t = pl.ds(t, nb, stride=tt)
            for p in range(n_pairs):
                emit(p, t, [xs[p, h, at_t, :] for h in range(RW_HEADS)])
            halves = [xv[hp, at_t, :] for hp in range(2)]
            ops_ref[2 * n_pairs, t] = jnp.concatenate(
                [halves[hp][:, hh * RW_HD:(hh + 1) * RW_HD] for hp in range(2) for hh in range(2)], axis=0).T


def _rw_prep(lat, u_rw, mu, wg, ww, wa, w0, a0, k_k, k_a, r_k, ones, hsum):
    nb, seq_len, _ = u_rw.shape
    tt = (LANES // nb) if lat else SUBLANES
    n_tiles = seq_len // tt
    n_ops = len(RW_OPS) + (0 if lat else 1)
    sub = tt // SUBLANES
    last = seq_len // SUBLANES - 1
    full = lambda a: pl.BlockSpec(a.shape, lambda i: (0,) * a.ndim)
    tok_spec = pl.BlockSpec((nb, tt, RW_W), lambda i: (0, i, 0))
    tok_sds = jax.ShapeDtypeStruct((nb, seq_len, RW_W), F32)
    out_specs = [pl.BlockSpec((n_ops, tt, RW_HD, LANES), lambda i: (0, i, 0, 0))]
    out_shape = [jax.ShapeDtypeStruct((n_ops, seq_len, RW_HD, LANES), F32)]
    scratch = [pltpu.VMEM((len(RW_OPS) // 2, RW_HEADS, nb * tt, LANES), F32)]
    if lat:
        out_specs.append(pl.BlockSpec((tt, SUBLANES, LANES), lambda i: (i, 0, 0)))
        out_shape.append(jax.ShapeDtypeStruct((seq_len, SUBLANES, LANES), F32))
    else:
        scratch.append(pltpu.VMEM((2, nb * tt, LANES), F32))
    return pl.pallas_call(
        functools.partial(_rw_prep_kernel, lat, nb, tt, n_tiles),
        grid=(n_tiles,),
        in_specs=[
            pl.BlockSpec((nb, tt, RW_COLS), lambda i: (0, i, 0)),
            pl.BlockSpec((nb, SUBLANES, RW_COLS), lambda i: (0, jnp.maximum(i * sub - 1, 0), 0)),
            pl.BlockSpec((nb, SUBLANES, RW_COLS), lambda i: (0, jnp.minimum((i + 1) * sub, last), 0)),
        ] + [full(a) for a in (mu, wg, ww, wa, w0, a0, k_k, k_a, r_k, ones, hsum)],
        out_specs=out_specs + [tok_spec, tok_spec],
        out_shape=out_shape + [tok_sds, tok_sds],
        scratch_shapes=scratch,
        compiler_params=_params(("arbitrary",)),
        name="rw_prep_lat" if lat else "rw_prep_ctx",
    )(u_rw, u_rw, u_rw, mu, wg, ww, wa, w0, a0, k_k, k_a, r_k, ones, hsum)


def _allsum_sublanes(x):
    x = x + pltpu.roll(x, 4, 0)
    x = x + pltpu.roll(x, 2, 0)
    return x + pltpu.roll(x, 1, 0)


def _rw_scan_kernel(lat, nv, n_chunks, *refs):
    if lat:
        af_ref, ab_ref, sf_ref, sb_ref, vf_ref, vb_ref, s0_ref, yf_ref, yb_ref, s_scr = refs
    else:
        af_ref, ab_ref, sf_ref, sb_ref, yf_ref, yb_ref, sfin_ref, s_scr = refs
        vf_ref = vb_ref = None
    c = pl.program_id(0)

    @pl.when(c == 0)
    def _():
        s_scr[...] = s0_ref[...] if lat else jnp.zeros(s_scr.shape, F32)

    n_part = SCAN_CHAINS // min(nv, SCAN_BLOCKS)

    def total(parts):
        while len(parts) > 1:
            parts = [parts[i] + parts[i + 1] for i in range(0, len(parts), 2)]
        return parts[0]

    def step(tt, carry):
        dirs = ((af_ref, sf_ref, vf_ref, yf_ref, tt), (ab_ref, sb_ref, vb_ref, yb_ref, SCAN_CHUNK - 1 - tt))
        for d, (a_ref, s_ref, v_ref, y_ref, tl) in enumerate(dirs):
            key_row = lambda ref, j, k: jnp.broadcast_to(ref[j, tl, pl.ds(k, 1), :], (SUBLANES, LANES))
            for first in range(0, nv, SCAN_BLOCKS):
                blocks = range(first, min(first + SCAN_BLOCKS, nv))
                sa = {vb: [None] * n_part for vb in blocks}
                for k in range(RW_HD):
                    kk = key_row(s_ref, 1, k)
                    for vb in blocks:
                        p = s_scr[d, vb, k] * kk
                        sa[vb][k % n_part] = p if k < n_part else sa[vb][k % n_part] + p
                sa = {vb: total(parts) for vb, parts in sa.items()}
                if lat:
                    vals = {0: v_ref[tl]}
                else:
                    vals = {vb: s_ref[2, tl, vb * SUBLANES:(vb + 1) * SUBLANES, :] for vb in blocks}
                y = {vb: [None] * n_part for vb in blocks}
                for k in range(RW_HD):
                    w, kd, bb = key_row(a_ref, 0, k), key_row(a_ref, 1, k), key_row(a_ref, 2, k)
                    r = key_row(s_ref, 0, k)
                    for vb in blocks:
                        sn = s_scr[d, vb, k] * w - sa[vb] * bb + vals[vb] * kd
                        s_scr[d, vb, k] = sn
                        y[vb][k % n_part] = sn * r if k < n_part else y[vb][k % n_part] + sn * r
                for vb in blocks:
                    if lat:
                        y_ref[tl] = total(y[vb])
                    else:
                        y_ref[tl, vb * SUBLANES:(vb + 1) * SUBLANES, :] = total(y[vb])
        return carry

    lax.fori_loop(0, SCAN_CHUNK, step, 0)

    if not lat:
        @pl.when(c == n_chunks - 1)
        def _():
            sfin_ref[...] = s_scr[...]


def _rw_scan(lat, ops, vv, s0):
    seq_len = ops.shape[1]
    nc = seq_len // SCAN_CHUNK
    nv = 1 if lat else RW_HD // SUBLANES
    n_shared = 2 if lat else 3
    s_shape = (2, nv, RW_HD, SUBLANES, LANES)
    state_spec = pl.BlockSpec(s_shape, lambda c: (0,) * 5)
    scratch = [pltpu.VMEM(s_shape, F32)]
    op_spec = lambda n, blk, rev: pl.BlockSpec(
        (n, SCAN_CHUNK, RW_HD, LANES), lambda c: (blk, (nc - 1 - c) if rev else c, 0, 0))
    in_specs = [op_spec(3, 0, False), op_spec(3, 1, True),
                op_spec(n_shared, 6 // n_shared, False), op_spec(n_shared, 6 // n_shared, True)]
    args = [ops] * 4
    if lat:
        y_spec = lambda rev: pl.BlockSpec((SCAN_CHUNK, SUBLANES, LANES), lambda c: ((nc - 1 - c) if rev else c, 0, 0))
        y_sds = jax.ShapeDtypeStruct((seq_len, SUBLANES, LANES), F32)
        in_specs += [y_spec(False), y_spec(True), state_spec]
        args += [vv, vv, s0]
        out_specs = [y_spec(False), y_spec(True)]
        out_shape = [y_sds, y_sds]
    else:
        y_spec = lambda rev: pl.BlockSpec((SCAN_CHUNK, RW_HD, LANES), lambda c: ((nc - 1 - c) if rev else c, 0, 0))
        y_sds = jax.ShapeDtypeStruct((seq_len, RW_HD, LANES), F32)
        out_specs = [y_spec(False), y_spec(True), state_spec]
        out_shape = [y_sds, y_sds, jax.ShapeDtypeStruct(s_shape, F32)]
    return pl.pallas_call(
        functools.partial(_rw_scan_kernel, lat, nv, nc),
        grid=(nc,),
        in_specs=in_specs,
        out_specs=out_specs,
        out_shape=out_shape,
        scratch_shapes=scratch,
        compiler_params=_params(("arbitrary",)),
        name="rw_scan_lat" if lat else "rw_scan_ctx",
    )(*args)


def _rw_post_ctx_kernel(nb, tt, yf_ref, yb_ref, g_ref, gb_ref, lnw_ref, lnb_ref, o_ref):
    for t in range(tt):
        y = yf_ref[t] + yb_ref[t]
        dev = y - jnp.mean(y, axis=0, keepdims=True)
        yn = (dev * lax.rsqrt(jnp.mean(dev * dev, axis=0, keepdims=True) + RW_GN_EPS)).T
        tok = jnp.concatenate([yn[h * nb:(h + 1) * nb, :] for h in range(RW_HEADS)], axis=1)
        o_ref[:, t, :] = (tok * lnw_ref[...] + lnb_ref[...]) * g_ref[:, t, :] + gb_ref[:, t, :]


def _rw_post_lat_kernel(nb, yf_ref, yb_ref, g_ref, gb_ref, lnw_ref, lnb_ref, hs_ref, o_ref):
    grp = LANES // nb
    ys = [yf_ref[:, vi, :] + yb_ref[:, vi, :] for vi in range(SUBLANES)]
    for b in range(nb):
        y = jnp.concatenate([yv[:, b * grp:(b + 1) * grp] for yv in ys], axis=1)
        dev = y - _sel_dot(y, hs_ref[...], EXACT_TERMS) * (1.0 / RW_HD)
        yn = dev * lax.rsqrt(_sel_dot(dev * dev, hs_ref[...], NORM_TERMS) * (1.0 / RW_HD) + RW_GN_EPS)
        o_ref[b] = (yn * lnw_ref[...] + lnb_ref[...]) * g_ref[b] + gb_ref[b]


def _rw_post(lat, yf, yb, g, gb, lnw, lnb, hs):
    nb, seq_len, _ = g.shape
    tt = LANES if lat else SUBLANES
    full = lambda a: pl.BlockSpec(a.shape, lambda i: (0,) * a.ndim)
    tok_spec = pl.BlockSpec((nb, tt, RW_W), lambda i: (0, i, 0))
    if lat:
        y_spec = pl.BlockSpec((tt, SUBLANES, LANES), lambda i: (i, 0, 0))
        body, extra = functools.partial(_rw_post_lat_kernel, nb), [hs]
    else:
        y_spec = pl.BlockSpec((tt, RW_HD, LANES), lambda i: (i, 0, 0))
        body, extra = functools.partial(_rw_post_ctx_kernel, nb, tt), []
    return pl.pallas_call(
        body,
        grid=(seq_len // tt,),
        in_specs=[y_spec, y_spec, tok_spec, tok_spec, full(lnw), full(lnb)] + [full(a) for a in extra],
        out_specs=tok_spec,
        out_shape=jax.ShapeDtypeStruct((nb, seq_len, RW_W), F32),
        compiler_params=_params(("parallel",)),
        name="rw_post_lat" if lat else "rw_post_ctx",
    )(yf, yb, g, gb, lnw, lnb, *extra)


def _rw_mixer(lat, u_rw, s0, prep_params, lnw, lnb, hs):
    nb = u_rw.shape[0]
    nvb = RW_HD // SUBLANES
    if lat:
        ops, vv, g, gb = _rw_prep(True, u_rw, *prep_params)
        grp = LANES // (nb * RW_HEADS)
        s0 = s0.reshape(nb, 2, RW_HEADS, grp, SUBLANES, RW_HD)
        s0 = jnp.transpose(s0, (1, 5, 4, 0, 2, 3)).reshape(2, 1, RW_HD, SUBLANES, LANES)
        yf, yb = _rw_scan(True, ops, vv, s0)
        return _rw_post(True, yf, yb, g, gb, lnw, lnb, hs), None
    ops, g, gb = _rw_prep(False, u_rw, *prep_params)
    yf, yb, sfin = _rw_scan(False, ops, None, None)
    sfin = sfin.reshape(2, nvb, RW_HD, SUBLANES, RW_HEADS, nb)
    sfin = jnp.transpose(sfin, (5, 0, 4, 1, 3, 2)).reshape(nb, 2, RW_HEADS, RW_HD, RW_HD)
    return _rw_post(False, yf, yb, g, gb, lnw, lnb, hs), sfin


def _softplus(x):
    return jnp.maximum(x, 0.0) + jnp.log1p(jnp.exp(-jnp.abs(x)))


def _ssd_kernel(reverse, n_blocks, u_ref, up_ref, un_ref, h0_ref, cw_ref, cb_ref, dtb_ref, a_ref,
                tri_ref, exp_ref, *rest):
    if reverse:
        yf_ref, nw_ref, out_ref, hfin_ref, h_scr = rest
    else:
        dsk_ref, out_ref, hfin_ref, h_scr = rest
    s = pl.program_id(1)
    blk = (n_blocks - 1 - s) if reverse else s

    @pl.when(s == 0)
    def _():
        h_scr[...] = h0_ref[...]

    q = SSD_CHUNK
    rows = SSD_CHUNKS_PER_STEP * q
    u = u_ref[...]
    xbc = u[:, SSD_W:SSD_W + 512]
    row = _iota(xbc.shape, 0)
    has_prev = (blk > 0).astype(F32)
    has_next = (blk < n_blocks - 1).astype(F32)
    prev = jnp.where(row == 0, up_ref[SUBLANES - 1:SUBLANES, SSD_W:SSD_W + 512] * has_prev, pltpu.roll(xbc, 1, 0))
    nxt = jnp.where(row == rows - 1, un_ref[0:1, SSD_W:SSD_W + 512] * has_next, pltpu.roll(xbc, rows - 1, 0))
    xc = _silu(cw_ref[0:1, :] * prev + cw_ref[1:2, :] * xbc + cw_ref[2:3, :] * nxt + cb_ref[...])
    dt_all = _softplus(u[:, SSD_W + 512:] + dtb_ref[...])
    dt_e_all = _sel_dot(dt_all, exp_ref[...], EXACT_TERMS)
    dta_all = dt_all * a_ref[...]

    ii = _iota((q, q), 0)
    jj = _iota((q, q), 1)
    tri = (jj >= ii) if reverse else (jj <= ii)
    lane = _iota((q, LANES), 1)
    rowc = _iota((LANES, 1), 0)
    state = [h_scr[g] for g in range(2)]
    order = range(SSD_CHUNKS_PER_STEP - 1, -1, -1) if reverse else range(SSD_CHUNKS_PER_STEP)
    for ci in order:
        rs = slice(ci * q, (ci + 1) * q)
        x = xc[rs, :SSD_W]
        b_all = xc[rs, SSD_W:SSD_W + LANES]
        c_all = xc[rs, SSD_W + LANES:]
        cs = _sel_dot(dta_all[rs], tri_ref[...], EXACT_TERMS, sel_first=True)
        cs_t = cs.T
        cs_e = _sel_dot(cs, exp_ref[...], EXACT_TERMS)
        tot = cs_e[0:1, :] if reverse else cs_e[q - 1:q, :]
        xdt = x * dt_e_all[rs]
        dec_in = jnp.exp(tot - cs_e)
        dec_out = jnp.exp(cs_e)
        cdec = jnp.exp(tot)
        ys = []
        for g in range(2):
            bg = b_all[:, g * 64:(g + 1) * 64]
            cg = c_all[:, g * 64:(g + 1) * 64]
            gm = _bdot_nt(cg, bg)
            sl = slice(g * LANES, (g + 1) * LANES)
            xdt_g = xdt[:, sl]
            yd = []
            for hh in range(2):
                h = 2 * g + hh
                diff = cs[:, h:h + 1] - cs_t[h:h + 1, :]
                lm = jnp.exp(jnp.where(tri, diff, -jnp.inf))
                yd.append(_bdot(gm * lm, xdt_g))
            hg = state[g]
            ys.append(jnp.where(lane < 64, yd[0], yd[1]) + _bdot_nt(cg, hg) * dec_out[:, sl])
            st = _bdot((xdt_g * dec_in[:, sl]).T, bg)
            cd = cdec[:, sl]
            state[g] = hg * jnp.where(rowc < 64, cd[:, 0:1], cd[:, 64:65]) + st
        y = jnp.concatenate(ys, axis=1)
        if reverse:
            val = (yf_ref[rs, :] + y) * _silu(u[rs, :SSD_W])
            out_ref[rs, :] = _rms(val, nw_ref[...])
        else:
            out_ref[rs, :] = y + dsk_ref[...] * x
    for g in range(2):
        h_scr[g] = state[g]
        hfin_ref[g] = state[g]


def _ssd_sweep(reverse, u_ssd, nb, seq_len, h0, cw, cb, dtb, a_neg, extra):
    rows = SSD_CHUNKS_PER_STEP * SSD_CHUNK
    nc = seq_len // rows
    sub_per_chunk = rows // SUBLANES
    last_sub = u_ssd.shape[0] // SUBLANES - 1
    chunk = (lambda s: nc - 1 - s) if reverse else (lambda s: s)
    rowblk = lambda b, s: b * nc + chunk(s)
    full = lambda a: pl.BlockSpec(a.shape, lambda b, s: (0,) * a.ndim)
    steps = np.arange(SSD_CHUNK)
    tri = jnp.asarray((steps[None, :] >= steps[:, None]) if reverse else (steps[None, :] <= steps[:, None]), BF16)
    expand = jnp.asarray(np.arange(LANES)[:, None] == np.arange(SSD_W)[None, :] // 64, BF16)
    in_specs = [
        pl.BlockSpec((rows, SSD_PAD), lambda b, s: (rowblk(b, s), 0)),
        pl.BlockSpec((SUBLANES, SSD_PAD), lambda b, s: (jnp.maximum(rowblk(b, s) * sub_per_chunk - 1, 0), 0)),
        pl.BlockSpec((SUBLANES, SSD_PAD),
                     lambda b, s: (jnp.minimum((rowblk(b, s) + 1) * sub_per_chunk, last_sub), 0)),
        pl.BlockSpec((None, 2, LANES, 64), lambda b, s: (b, 0, 0, 0)),
        full(cw), full(cb), full(dtb), full(a_neg), full(tri), full(expand),
    ]
    args = [u_ssd, u_ssd, u_ssd, h0, cw, cb, dtb, a_neg, tri, expand]
    if reverse:
        yf, nw = extra
        in_specs += [pl.BlockSpec((rows, SSD_W), lambda b, s: (rowblk(b, s), 0)), full(nw)]
        args += [yf, nw]
    else:
        in_specs += [full(extra)]
        args += [extra]
    return pl.pallas_call(
        functools.partial(_ssd_kernel, reverse, nc),
        grid=(nb, nc),
        in_specs=in_specs,
        out_specs=[pl.BlockSpec((rows, SSD_W), lambda b, s: (rowblk(b, s), 0)),
                   pl.BlockSpec((None, 2, LANES, 64), lambda b, s: (b, 0, 0, 0))],
        out_shape=[jax.ShapeDtypeStruct((nb * seq_len, SSD_W), F32),
                   jax.ShapeDtypeStruct((nb, 2, LANES, 64), F32)],
        scratch_shapes=[pltpu.VMEM((2, LANES, 64), F32)],
        compiler_params=_params(("parallel", "arbitrary")),
        name="ssd_bwd" if reverse else "ssd_fwd",
    )(*args)


def _ssd_mixer(u_ssd, nb, seq_len, h0, cw, cb, dtb, a_neg, dsk, nw):
    yf, hf = _ssd_sweep(False, u_ssd, nb, seq_len, h0[:, 0], cw, cb, dtb[0], a_neg[0], dsk)
    out, hb = _ssd_sweep(True, u_ssd, nb, seq_len, h0[:, 1], cw, cb, dtb[1], a_neg[1], (yf, nw))
    return out, jnp.stack([hf, hb], axis=1).reshape(nb, 2, 4, 64, 64)


def _route(scores, bias):
    n_grp = N_EXPERTS // EXP_PER_GROUP
    sel = scores + bias
    sc = [scores[e:e + 1, :] for e in range(N_EXPERTS)]
    sl = [sel[e:e + 1, :] for e in range(N_EXPERTS)]
    as_int = lambda cond: jnp.where(cond, 1, 0)
    top2, grp = [], []
    for g in range(n_grp):
        members = range(g * EXP_PER_GROUP, (g + 1) * EXP_PER_GROUP)
        total = None
        for e in members:
            ahead = [as_int(sl[o] >= sl[e]) if o < e else as_int(sl[o] > sl[e]) for o in members if o != e]
            top2.append(ahead[0] + ahead[1] + ahead[2] < 2)
            kept = jnp.where(top2[e], sl[e], 0.0)
            total = kept if total is None else total + kept
        grp.append(total)
    rows = []
    for g in range(n_grp):
        beaten = [as_int(grp[o] >= grp[g]) if o < g else as_int(grp[o] > grp[g]) for o in range(n_grp) if o != g]
        best = beaten[0] + beaten[1] + beaten[2] == 0
        members = range(g * EXP_PER_GROUP, (g + 1) * EXP_PER_GROUP)
        chosen = [jnp.where(best, as_int(top2[e]), 0) > 0 for e in members]
        cw = [jnp.where(c, sc[e], 0.0) for c, e in zip(chosen, members)]
        den = cw[0] + cw[1] + cw[2] + cw[3]
        rows += [jnp.where(c, w / jnp.where(c, den, 1.0), 0.0) for c, w in zip(chosen, cw)]
    return jnp.concatenate(rows, axis=0)


def _outproj_kernel(oatt_ref, orw_ref, ossd_ref, x_ref, mod_ref, wout_ref,
                    n2_ref, rw_ref, rb_ref, x1_ref, h2_ref, comb_ref):
    o = jnp.concatenate([oatt_ref[...], orw_ref[...], ossd_ref[...]], axis=1)
    x1 = x_ref[...] + mod_ref[2] * _bdot(o, wout_ref[...])
    h2 = _rms(x1, n2_ref[...]) * (1.0 + mod_ref[4]) + mod_ref[3]
    comb = _route(_sigmoid(_bdot_nt(rw_ref[...], h2)), rb_ref[...])
    x1_ref[...] = x1
    h2_ref[...] = h2.astype(BF16)
    comb_ref[...] = jnp.concatenate([comb, jnp.zeros((LANES - N_EXPERTS, comb.shape[1]), F32)], axis=0).T


def _outproj(o_att, o_rw, o_ssd, x, mod_l, row_fn, w_out_bf, n2, router_w, router_b):
    n_tok = x.shape[0]
    tile = lambda w: pl.BlockSpec((TOKEN_TILE, w), lambda i: (i, 0))
    full = lambda a: pl.BlockSpec(a.shape, lambda i: (0,) * a.ndim)
    return pl.pallas_call(
        _outproj_kernel,
        grid=(n_tok // TOKEN_TILE,),
        in_specs=[tile(ATT_W), tile(RW_W), tile(SSD_W), tile(D_MODEL),
                  _mod_spec(row_fn)] + [full(a) for a in (w_out_bf, n2, router_w, router_b)],
        out_specs=[tile(D_MODEL), tile(D_MODEL), tile(LANES)],
        out_shape=[jax.ShapeDtypeStruct((n_tok, D_MODEL), F32),
                   jax.ShapeDtypeStruct((n_tok, D_MODEL), BF16),
                   jax.ShapeDtypeStruct((n_tok, LANES), F32)],
        compiler_params=_params(("parallel",)),
        name="outproj",
    )(o_att, o_rw, o_ssd, x, mod_l, w_out_bf, n2, router_w, router_b)


def _moe_kernel(h2_ref, comb_ref, x1_ref, mod_ref, wg_ref, wu_ref, wd_ref, o_ref, acc_ref):
    s = pl.program_id(1)

    @pl.when(s == 0)
    def _():
        acc_ref[...] = jnp.zeros_like(acc_ref)

    h2 = h2_ref[...]
    comb = comb_ref[...]
    lane = _iota(comb.shape, 1)
    weighted = []
    for j in range(MOE_EXPERTS_PER_STEP):
        hid = _silu(jnp.dot(h2, wg_ref[j], preferred_element_type=F32)) * jnp.dot(
            h2, wu_ref[j], preferred_element_type=F32)
        wcol = jnp.sum(jnp.where(lane == s * MOE_EXPERTS_PER_STEP + j, comb, 0.0), axis=1, keepdims=True)
        weighted.append((hid * wcol).astype(BF16))
    acc_ref[...] += jnp.dot(jnp.concatenate(weighted, axis=1), wd_ref[...].reshape(-1, D_MODEL),
                            preferred_element_type=F32)

    @pl.when(s == N_EXPERTS // MOE_EXPERTS_PER_STEP - 1)
    def _():
        o_ref[...] = x1_ref[...] + mod_ref[5] * acc_ref[...]


def _moe(h2, comb, x1, mod_l, row_fn, wg_bf, wu_bf, wd_bf):
    n_tok = x1.shape[0]
    tile = lambda w: pl.BlockSpec((MOE_TILE, w), lambda i, e: (i, 0))
    return pl.pallas_call(
        _moe_kernel,
        grid=(n_tok // MOE_TILE, N_EXPERTS // MOE_EXPERTS_PER_STEP),
        in_specs=[tile(D_MODEL), tile(LANES), tile(D_MODEL), _mod_spec(row_fn),
                  pl.BlockSpec((MOE_EXPERTS_PER_STEP, D_MODEL, D_FF), lambda i, e: (e, 0, 0)),
                  pl.BlockSpec((MOE_EXPERTS_PER_STEP, D_MODEL, D_FF), lambda i, e: (e, 0, 0)),
                  pl.BlockSpec((MOE_EXPERTS_PER_STEP, D_FF, D_MODEL), lambda i, e: (e, 0, 0))],
        out_specs=tile(D_MODEL),
        out_shape=jax.ShapeDtypeStruct((n_tok, D_MODEL), F32),
        scratch_shapes=[pltpu.VMEM((MOE_TILE, D_MODEL), F32)],
        compiler_params=_params(("parallel", "arbitrary")),
        name="moe",
    )(h2, comb, x1, mod_l, wg_bf, wu_bf, wd_bf)


def _rope_tables(seq_len):
    t = np.arange(seq_len)
    pos = np.stack([t // GRID_W, t % GRID_W], 0).astype(np.float32)
    n_freq = HEAD_DIM // 4
    inv = jnp.asarray(ROPE_BASE, F32) ** (-jnp.arange(n_freq, dtype=F32) / n_freq)
    ang = jnp.asarray(pos)[:, :, None] * inv
    cos, sin = jnp.cos(ang), jnp.sin(ang)
    cos_h = jnp.concatenate([cos[0], cos[0], cos[1], cos[1]], -1)
    sin_h = jnp.concatenate([-sin[0], sin[0], -sin[1], sin[1]], -1)
    return jnp.tile(cos_h, (1, 2)), jnp.tile(sin_h, (1, 2))


def _pad_rows(w, start, total):
    return jnp.pad(w, ((0, 0),) * (w.ndim - 2) + ((start, total - start - w.shape[-2]), (0, 0)))


def _value_order(x, axis, start, grp):
    if grp is None:
        return x
    axis %= x.ndim
    sl = lambda a, b: lax.slice_in_dim(x, a, b, axis=axis)
    mid = sl(start, start + RW_W)
    shape = mid.shape
    mid = mid.reshape(shape[:axis] + (RW_HEADS, grp, RW_HD // grp) + shape[axis + 1:])
    order = tuple(range(axis)) + (axis + 2, axis, axis + 1) + tuple(range(axis + 3, mid.ndim))
    mid = jnp.transpose(mid, order).reshape(shape)
    return jnp.concatenate([sl(0, start), mid, sl(start + RW_W, x.shape[axis])], axis=axis)


def _rw_layer_params(l, P, grp):
    row = lambda a: a.reshape(1, -1)
    v0 = 2 * RW_W
    w_in = jnp.pad(P["w_in"][l], ((0, 0), (0, SSD_PAD - SSD_COLS))).astype(BF16)
    head = np.arange(RW_W) // RW_HD if grp is None else (np.arange(RW_W) // grp) % RW_HEADS
    rp = {}
    rp["w_in"] = _value_order(w_in, 1, ATT_COLS + v0, grp)
    rp["w_out"] = _value_order(P["w_out"][l].astype(BF16), 0, ATT_W, grp)
    rp["prep"] = (
        _value_order(P["rw_mu"][l], 1, v0, grp),
        _pad_rows(_value_order(P["rw_g_up"][l], 1, 0, grp), 64, LANES),
        _pad_rows(P["rw_w_up"][l], 0, LANES),
        _pad_rows(P["rw_a_up"][l], 32, LANES),
        P["rw_w0"][l], P["rw_a0"][l],
        row(P["rw_k_k"][l]), row(P["rw_k_a"][l]), row(P["rw_r_k"][l]),
        _head_selector(np.arange(RW_W) // RW_HD), _head_selector(np.arange(RW_W) // RW_HD, head),
    )
    rp["lnw"] = _value_order(row(P["rw_ln_w"][l]), 1, 0, grp)
    rp["lnb"] = _value_order(row(P["rw_ln_b"][l]), 1, 0, grp)
    rp["hs"] = _head_selector(head)
    return rp


def _layer_params(l, P):
    row = lambda a: a.reshape(1, -1)
    lp = {}
    lp["n1"] = row(P["norm1_w"][l])
    lp["n2"] = row(P["norm2_w"][l])
    lp["qw"] = row(jnp.tile(P["q_norm_w"][l], ATT_HEADS))
    lp["kw"] = row(jnp.tile(P["k_norm_w"][l], ATT_KV_HEADS))
    lp["sink"] = P["attn_sink"][l]
    pad_heads = lambda a: jnp.pad(a, ((0, 0), (0, LANES - a.shape[1]))).reshape(2, 1, LANES)
    lp["ssd"] = (
        P["ssd_conv_w"][l], row(P["ssd_conv_b"][l]),
        pad_heads(P["ssd_dt_bias"][l]), pad_heads(-jnp.exp(P["ssd_a_log"][l])),
        row(jnp.repeat(P["ssd_d"][l], 64)), row(P["ssd_norm_w"][l]),
    )
    lp["wg"] = P["exp_gate"][l].astype(BF16)
    lp["wu"] = P["exp_up"][l].astype(BF16)
    lp["wd"] = P["exp_down"][l].astype(BF16)
    return lp


def _trunk(x, nb, seq_len, is_ctx, mod, layers, rw_layers, router, cache_k, cache_v, state_rwkv, state_ssd):
    tiles_per_seq = seq_len // TOKEN_TILE
    moe_per_seq = max(seq_len // MOE_TILE, 1)
    if is_ctx:
        row_tok = lambda i: 0
        row_moe = lambda i: 0
        rope = None
    else:
        row_tok = lambda i: 1 + i // tiles_per_seq
        row_moe = lambda i: 1 + i // moe_per_seq
        rope = _rope_tables(seq_len)
    new_k, new_v, new_rw, new_ssd = [], [], [], []
    for l, (lp, rp) in enumerate(zip(layers, rw_layers)):
        q, k, v, u_rw, u_ssd = _inproj(x, mod[l], row_tok, lp["n1"], rp["w_in"], lp["qw"], lp["kw"], rope, seq_len)
        if is_ctx:
            o_att = _attn_ctx(lp["sink"], q, k, v, nb, seq_len)
            rw0 = None
            ssd0 = jnp.zeros((nb, 2, 2, LANES, 64), F32)
        else:
            kc = cache_k[:, l].reshape(nb, -1, KV_W)
            vc = cache_v[:, l].reshape(nb, -1, KV_W)
            o_att = _attn_lat(lp["sink"], q, k, v, kc, vc, nb, seq_len)
            rw0 = state_rwkv[:, l]
            ssd0 = state_ssd[:, l].reshape(nb, 2, 2, LANES, 64)
        o_rw, s_rw = _rw_mixer(not is_ctx, u_rw.reshape(nb, seq_len, RW_COLS), rw0, rp["prep"],
                               rp["lnw"], rp["lnb"], rp["hs"])
        o_ssd, s_ssd = _ssd_mixer(u_ssd, nb, seq_len, ssd0, *lp["ssd"])
        if is_ctx:
            new_k.append(k.reshape(nb, seq_len, ATT_KV_HEADS, HEAD_DIM))
            new_v.append(v.reshape(nb, seq_len, ATT_KV_HEADS, HEAD_DIM))
            new_rw.append(s_rw)
            new_ssd.append(s_ssd)
        x1, h2, comb = _outproj(o_att, o_rw.reshape(nb * seq_len, RW_W), o_ssd, x, mod[l], row_tok, rp["w_out"],
                                lp["n2"], *router)
        x = _moe(h2, comb, x1, mod[l], row_moe, lp["wg"], lp["wu"], lp["wd"])
    if is_ctx:
        return x, tuple(jnp.stack(t, 1) for t in (new_k, new_v, new_rw, new_ssd))
    return x, None


def kernel(x_prompt, x_sample, cache_k, cache_v, state_rwkv, state_ssd, c, c_ctx, w_mod, b_mod, norm1_w, norm2_w, w_in, w_out, q_norm_w, k_norm_w, attn_sink, rw_mu, rw_w0, rw_w_up, rw_a0, rw_a_up, rw_g_up, rw_k_k, rw_k_a, rw_r_k, rw_ln_w, rw_ln_b, ssd_conv_w, ssd_conv_b, ssd_dt_bias, ssd_a_log, ssd_d, ssd_norm_w, router_w, router_bias, exp_gate, exp_up, exp_down):
    P = dict(norm1_w=norm1_w, norm2_w=norm2_w, w_in=w_in, w_out=w_out, q_norm_w=q_norm_w, k_norm_w=k_norm_w,
             attn_sink=attn_sink, rw_mu=rw_mu, rw_w0=rw_w0, rw_w_up=rw_w_up, rw_a0=rw_a0, rw_a_up=rw_a_up,
             rw_g_up=rw_g_up, rw_k_k=rw_k_k, rw_k_a=rw_k_a, rw_r_k=rw_r_k, rw_ln_w=rw_ln_w, rw_ln_b=rw_ln_b,
             ssd_conv_w=ssd_conv_w, ssd_conv_b=ssd_conv_b, ssd_dt_bias=ssd_dt_bias, ssd_a_log=ssd_a_log,
             ssd_d=ssd_d, ssd_norm_w=ssd_norm_w, exp_gate=exp_gate, exp_up=exp_up, exp_down=exp_down)
    nb_ctx, seq_ctx, _ = x_prompt.shape
    nb_lat, seq_lat, _ = x_sample.shape
    assert nb_lat + 1 <= SUBLANES and seq_ctx % TOKEN_TILE == 0 and seq_lat % MOE_TILE == 0
    assert nb_ctx * RW_HEADS == LANES and LANES % (nb_lat * RW_HEADS) == 0
    cond8 = jnp.zeros((SUBLANES, D_MODEL), F32).at[0].set(c_ctx).at[1:1 + nb_lat].set(c)
    mod = _modulation(cond8, w_mod, b_mod)
    layers = [_layer_params(l, P) for l in range(DEPTH)]
    rw_ctx = [_rw_layer_params(l, P, None) for l in range(DEPTH)]
    rw_lat = [_rw_layer_params(l, P, LANES // (nb_lat * RW_HEADS)) for l in range(DEPTH)]
    router = (router_w.T.astype(BF16), router_bias.reshape(N_EXPERTS, 1))
    y_prompt, ctx_state = _trunk(x_prompt.reshape(-1, D_MODEL), nb_ctx, seq_ctx, True, mod, layers, rw_ctx, router,
                                 None, None, None, None)
    y_sample, _ = _trunk(x_sample.reshape(-1, D_MODEL), nb_lat, seq_lat, False, mod, layers, rw_lat, router,
                         cache_k, cache_v, state_rwkv, state_ssd)
    return (y_prompt.reshape(x_prompt.shape), y_sample.reshape(x_sample.shape)) + ctx_state
```

```python
import functools
import math

import jax
import jax.numpy as jnp
import numpy as np
from jax import lax
from jax.experimental import pallas as pl
from jax.experimental.pallas import tpu as pltpu

F32 = jnp.float32
BF16 = jnp.bfloat16
HIGHEST = lax.Precision.HIGHEST

D_MODEL = 1024
DEPTH = 2
GRID_W = 64
EPS = 1e-6
ATT_HEADS = 8
ATT_KV_HEADS = 2
HEAD_DIM = 64
ATT_W = 512
KV_W = 128
ATT_COLS = 768
WINDOW = 128
BLK = 128
ROPE_BASE = 10000.0
RW_HEADS = 4
RW_HD = 64
RW_W = 256
RW_COLS = 896
RW_DECAY_SCALE = 0.606531
RW_GN_EPS = 64e-5
SSD_W = 256
SSD_CHUNK = 128
SSD_CHUNKS_PER_STEP = 4
SSD_COLS = 772
SSD_PAD = 896
IN_PAD = ATT_COLS + RW_COLS + SSD_PAD
N_EXPERTS = 16
EXP_PER_GROUP = 4
D_FF = 256

LANES = 128
SUBLANES = 8
TOKEN_TILE = 256
MOE_TILE = 1024
MOE_EXPERTS_PER_STEP = 4
SCAN_CHUNK_CTX = 16
SCAN_CHUNK_LAT = 32
SCAN_BLOCKS = 4
SCAN_CHAINS = 8
VMEM_LIMIT = 48 * 1024 * 1024


def _params(sem):
    return pltpu.CompilerParams(dimension_semantics=sem, vmem_limit_bytes=VMEM_LIMIT)


def _bdot(a, b):
    return jnp.dot(a.astype(BF16), b.astype(BF16), preferred_element_type=F32)


def _bdot_nt(a, b):
    return lax.dot_general(a.astype(BF16), b.astype(BF16), (((1,), (1,)), ((), ())),
                           preferred_element_type=F32)


def _sel_dot(x, sel, terms, sel_first=False):
    acc = None
    rem = x
    for i in range(terms):
        piece = rem.astype(BF16)
        part = jnp.dot(sel, piece, preferred_element_type=F32) if sel_first else jnp.dot(
            piece, sel, preferred_element_type=F32)
        acc = part if acc is None else acc + part
        if i + 1 < terms:
            rem = rem - piece.astype(F32)
    return acc


def _iota(shape, axis):
    return lax.broadcasted_iota(jnp.int32, shape, axis)


def _head_selector(heads_in, heads_out=None):
    heads_out = heads_in if heads_out is None else heads_out
    return jnp.asarray(np.asarray(heads_in)[:, None] == np.asarray(heads_out)[None, :], BF16)


NORM_TERMS = 2
EXACT_TERMS = 3


def _sigmoid(x):
    return 1.0 / (1.0 + jnp.exp(-x))


def _silu(x):
    return x * _sigmoid(x)


def _rms(x, w):
    return x * lax.rsqrt(jnp.mean(x * x, -1, keepdims=True) + EPS) * w


def _mod_kernel(c_ref, w_ref, b_ref, o_ref):
    o_ref[...] = _bdot(_silu(c_ref[...]), w_ref[...]) + b_ref[...]


def _modulation(cond8, w_mod, b_mod):
    out = pl.pallas_call(
        _mod_kernel,
        grid=(DEPTH, 6),
        in_specs=[
            pl.BlockSpec((SUBLANES, D_MODEL), lambda l, j: (0, 0)),
            pl.BlockSpec((None, D_MODEL, D_MODEL), lambda l, j: (l, 0, j)),
            pl.BlockSpec((None, 1, D_MODEL), lambda l, j: (l, 0, j)),
        ],
        out_specs=pl.BlockSpec((None, None, SUBLANES, D_MODEL), lambda l, j: (l, j, 0, 0)),
        out_shape=jax.ShapeDtypeStruct((DEPTH, 6, SUBLANES, D_MODEL), F32),
        compiler_params=_params(("arbitrary", "arbitrary")),
        name="modulation",
    )(cond8, w_mod, b_mod.reshape(DEPTH, 1, 6 * D_MODEL))
    return out.reshape(DEPTH, 6, SUBLANES, 1, D_MODEL)


def _mod_spec(row_fn):
    return pl.BlockSpec((6, None, 1, D_MODEL), lambda *idx: (0, row_fn(idx[0]), 0, 0))


def _swap16(x):
    w = x.shape[1]
    first = (_iota(x.shape, 1) & 31) < 16
    return jnp.where(first, pltpu.roll(x, w - 16, 1), pltpu.roll(x, 16, 1))


def _inproj_kernel(use_rope, x_ref, mod_ref, n1_ref, w_ref, qw_ref, kw_ref, hq_ref, *rest):
    if use_rope:
        cos_ref, sin_ref, q_ref, k_ref, v_ref, urw_ref, ussd_ref = rest
    else:
        q_ref, k_ref, v_ref, urw_ref, ussd_ref = rest
    h = _rms(x_ref[...], n1_ref[...]) * (1.0 + mod_ref[1]) + mod_ref[0]
    u = _bdot(h, w_ref[...])
    q = u[:, :ATT_W]
    k = u[:, ATT_W:ATT_W + KV_W]
    q = q * lax.rsqrt(_sel_dot(q * q, hq_ref[...], NORM_TERMS) * (1.0 / HEAD_DIM) + EPS) * qw_ref[...]
    k = k * lax.rsqrt(_sel_dot(k * k, hq_ref[:KV_W, :KV_W], NORM_TERMS) * (1.0 / HEAD_DIM) + EPS) * kw_ref[...]
    if use_rope:
        cos = cos_ref[...]
        sin = sin_ref[...]
        k = k * cos + _swap16(k) * sin
        cos4 = jnp.concatenate([cos] * 4, axis=1)
        sin4 = jnp.concatenate([sin] * 4, axis=1)
        q = q * cos4 + _swap16(q) * sin4
    q_ref[...] = q
    k_ref[...] = k
    v_ref[...] = u[:, ATT_W + KV_W:ATT_COLS]
    urw_ref[...] = u[:, ATT_COLS:ATT_COLS + RW_COLS]
    ussd_ref[...] = u[:, ATT_COLS + RW_COLS:]


def _inproj(x, mod_l, row_fn, n1, w_in_bf, qw, kw, rope, seq_len):
    n_tok = x.shape[0]
    tiles_per_seq = seq_len // TOKEN_TILE
    in_specs = [
        pl.BlockSpec((TOKEN_TILE, D_MODEL), lambda i: (i, 0)),
        _mod_spec(row_fn),
        pl.BlockSpec((1, D_MODEL), lambda i: (0, 0)),
        pl.BlockSpec((D_MODEL, IN_PAD), lambda i: (0, 0)),
        pl.BlockSpec((1, ATT_W), lambda i: (0, 0)),
        pl.BlockSpec((1, KV_W), lambda i: (0, 0)),
        pl.BlockSpec((ATT_W, ATT_W), lambda i: (0, 0)),
    ]
    args = [x, mod_l, n1, w_in_bf, qw, kw, _head_selector(np.arange(ATT_W) // HEAD_DIM)]
    if rope is not None:
        in_specs += [pl.BlockSpec((TOKEN_TILE, LANES), lambda i: (i % tiles_per_seq, 0))] * 2
        args += list(rope)
    widths = (ATT_W, KV_W, KV_W, RW_COLS, SSD_PAD)
    return pl.pallas_call(
        functools.partial(_inproj_kernel, rope is not None),
        grid=(n_tok // TOKEN_TILE,),
        in_specs=in_specs,
        out_specs=[pl.BlockSpec((TOKEN_TILE, w), lambda i: (i, 0)) for w in widths],
        out_shape=[jax.ShapeDtypeStruct((n_tok, w), F32) for w in widths],
        compiler_params=_params(("parallel",)),
        name="inproj",
    )(*args)


LOG2E = 1.4426950408889634


def _attend_heads(sink_ref, q, parts):
    groups = ATT_HEADS // ATT_KV_HEADS
    outs = []
    for hk in range(ATT_KV_HEADS):
        cols = slice(hk * HEAD_DIM, (hk + 1) * HEAD_DIM)
        keys = [k[:, cols].astype(BF16) for k, _, _ in parts]
        vals = [v[:, cols].astype(BF16) for _, v, _ in parts]
        for g in range(groups):
            hd = hk * groups + g
            qh = q[:, hd * HEAD_DIM:(hd + 1) * HEAD_DIM].astype(BF16)
            sink = sink_ref[hd] * LOG2E
            scores = []
            for kh, (_, _, mask) in zip(keys, parts):
                s = lax.dot_general(qh, kh, (((1,), (1,)), ((), ())),
                                    preferred_element_type=F32) * (HEAD_DIM ** -0.5 * LOG2E)
                scores.append(s if mask is None else jnp.where(mask, s, -1e30))
            m = jnp.max(scores[0], -1, keepdims=True)
            for s in scores[1:]:
                m = jnp.maximum(m, jnp.max(s, -1, keepdims=True))
            m = jnp.maximum(m, sink)
            den = jnp.exp2(sink - m)
            o = None
            for s, vh in zip(scores, vals):
                p = jnp.exp2(s - m)
                den = den + jnp.sum(p, -1, keepdims=True)
                pv = jnp.dot(p.astype(BF16), vh, preferred_element_type=F32)
                o = pv if o is None else o + pv
            outs.append(o / den)
    return jnp.concatenate(outs, axis=1)


def _attn_ctx_kernel(sink_ref, q_ref, k_ref, v_ref, o_ref):
    o_ref[...] = _attend_heads(sink_ref, q_ref[...], [(k_ref[...], v_ref[...], None)])


def _attn_ctx(sink, q, k, v, nb, seq_len):
    return pl.pallas_call(
        _attn_ctx_kernel,
        grid=(nb,),
        in_specs=[
            pl.BlockSpec(memory_space=pltpu.SMEM),
            pl.BlockSpec((seq_len, ATT_W), lambda b: (b, 0)),
            pl.BlockSpec((seq_len, KV_W), lambda b: (b, 0)),
            pl.BlockSpec((seq_len, KV_W), lambda b: (b, 0)),
        ],
        out_specs=pl.BlockSpec((seq_len, ATT_W), lambda b: (b, 0)),
        out_shape=jax.ShapeDtypeStruct((nb * seq_len, ATT_W), F32),
        compiler_params=_params(("parallel",)),
        name="attn_ctx",
    )(sink, q, k, v)


def _attn_lat_kernel(seq_len, sink_ref, q_ref, k_ref, v_ref, kc_ref, vc_ref, o_ref):
    i = pl.program_id(1)
    band = 3 * BLK
    start = pl.multiple_of(jnp.clip((i - 1) * BLK, 0, seq_len - band), BLK)
    dist = jnp.abs(i * BLK + _iota((BLK, band), 0) - (start + _iota((BLK, band), 1)))
    parts = [(k_ref[pl.ds(start, band), :], v_ref[pl.ds(start, band), :], dist <= WINDOW),
             (kc_ref[...], vc_ref[...], None)]
    o_ref[...] = _attend_heads(sink_ref, q_ref[...], parts)


def _attn_lat(sink, q, k, v, kc, vc, nb, seq_len):
    nblk = seq_len // BLK
    past = kc.shape[1]
    return pl.pallas_call(
        functools.partial(_attn_lat_kernel, seq_len),
        grid=(nb, nblk),
        in_specs=[
            pl.BlockSpec(memory_space=pltpu.SMEM),
            pl.BlockSpec((BLK, ATT_W), lambda b, i: (b * nblk + i, 0)),
            pl.BlockSpec((seq_len, KV_W), lambda b, i: (b, 0)),
            pl.BlockSpec((seq_len, KV_W), lambda b, i: (b, 0)),
            pl.BlockSpec((None, past, KV_W), lambda b, i: (b, 0, 0)),
            pl.BlockSpec((None, past, KV_W), lambda b, i: (b, 0, 0)),
        ],
        out_specs=pl.BlockSpec((BLK, ATT_W), lambda b, i: (b * nblk + i, 0)),
        out_shape=jax.ShapeDtypeStruct((nb * seq_len, ATT_W), F32),
        compiler_params=_params(("parallel", "arbitrary")),
        name="attn_lat",
    )(sink, q, k, v, kc, vc)


RW_OPS = ("w_f", "kd_f", "b_f", "w_b", "kd_b", "b_b", "r", "kk")


def _rw_prep_kernel(lat, nb, tt, n_tiles, u_ref, up_ref, un_ref, mu_ref, wg_ref, ww_ref, wa_ref,
                    w0_ref, a0_ref, kk_ref, ka_ref, rk_ref, ones_ref, hsum_ref, *rest):
    if lat:
        ops_ref, vv_ref, g_ref, gb_ref, xs = rest
    else:
        ops_ref, g_ref, gb_ref, xs, xv = rest
    i = pl.program_id(0)
    has_prev = (i > 0).astype(F32)
    has_next = (i < n_tiles - 1).astype(F32)
    row = _iota((tt, RW_COLS), 0)
    mu0 = mu_ref[0:1, :]
    mu1 = mu_ref[1:2, :]
    mixed = []
    for b in range(nb):
        ub = u_ref[b]
        prev = jnp.where(row == 0, up_ref[b, SUBLANES - 1:SUBLANES, :] * has_prev, pltpu.roll(ub, 1, 0))
        nxt = jnp.where(row == tt - 1, un_ref[b, 0:1, :] * has_next, pltpu.roll(ub, tt - 1, 0))
        mixed.append(ub + mu0 * (prev - ub) + mu1 * (nxt - ub))
    u = jnp.concatenate(mixed, axis=0)
    r = u[:, :RW_W]
    k = u[:, RW_W:2 * RW_W]
    v = u[:, 2 * RW_W:3 * RW_W]
    low = u[:, 3 * RW_W:]
    kk = k * kk_ref[...]
    kk = kk * lax.rsqrt(_sel_dot(kk * kk, ones_ref[...], NORM_TERMS) + 1e-12)
    tanh_low = jnp.tanh(low)
    vals = [None] * len(RW_OPS)
    for d in range(2):
        w = jnp.exp(-RW_DECAY_SCALE * _sigmoid(w0_ref[d:d + 1, :] + _bdot(tanh_low, ww_ref[d])))
        a = _sigmoid(a0_ref[d:d + 1, :] + _bdot(low, wa_ref[d]))
        vals[3 * d:3 * d + 3] = [w, k * (1.0 + (a - 1.0) * ka_ref[...]), kk * a]
    vals[6:8] = [r, kk]
    g = _bdot(_sigmoid(low), wg_ref[...])
    gb = _sel_dot(r * k * rk_ref[...], hsum_ref[...], EXACT_TERMS) * v * g
    for b in range(nb):
        g_ref[b] = g[b * tt:(b + 1) * tt]
        gb_ref[b] = gb[b * tt:(b + 1) * tt]

    n_pairs = len(RW_OPS) // 2
    low_half = _iota((nb * tt, LANES), 1) < RW_HD
    for p in range(n_pairs):
        for hp in range(2):
            first = vals[2 * p][:, hp * LANES:(hp + 1) * LANES]
            second = vals[2 * p + 1][:, hp * LANES:(hp + 1) * LANES]
            xs[p, 2 * hp] = jnp.where(low_half, first, pltpu.roll(second, RW_HD, 1))
            xs[p, 2 * hp + 1] = jnp.where(low_half, pltpu.roll(first, RW_HD, 1), second)

    def emit(p, t, tiles):
        both = jnp.concatenate(tiles, axis=0).T
        ops_ref[2 * p, t] = both[:RW_HD]
        ops_ref[2 * p + 1, t] = both[RW_HD:]

    if lat:
        def relayout(t, carry):
            for p in range(n_pairs):
                emit(p, t, [jnp.broadcast_to(xs[p, h, pl.ds(b * tt + t, 1), :], (SUBLANES, LANES))
                            for b in range(nb) for h in range(RW_HEADS)])
            return carry

        lax.fori_loop(0, tt, relayout, 0)
        grp = LANES // nb
        for vi in range(SUBLANES):
            vv_ref[:, vi, :] = jnp.concatenate(
                [v[b * tt:(b + 1) * tt, vi * grp:(vi + 1) * grp] for b in range(nb)], axis=1)
    else:
        xv[0] = v[:, :LANES]
        xv[1] = v[:, LANES:]
        for t in range(tt):
            at_t = pl.ds(t, nb, stride=tt)
            for p in range(n_pairs):
                emit(p, t, [xs[p, h, at_t, :] for h in range(RW_HEADS)])
            halves = [xv[hp, at_t, :] for hp in range(2)]
            ops_ref[2 * n_pairs, t] = jnp.concatenate(
                [halves[hp][:, hh * RW_HD:(hh + 1) * RW_HD] for hp in range(2) for hh in range(2)], axis=0).T


def _rw_prep(lat, u_rw, mu, wg, ww, wa, w0, a0, k_k, k_a, r_k, ones, hsum):
    nb, seq_len, _ = u_rw.shape
    tt = (LANES // nb) if lat else SUBLANES
    n_tiles = seq_len // tt
    n_ops = len(RW_OPS) + (0 if lat else 1)
    sub = tt // SUBLANES
    last = seq_len // SUBLANES - 1
    full = lambda a: pl.BlockSpec(a.shape, lambda i: (0,) * a.ndim)
    tok_spec = pl.BlockSpec((nb, tt, RW_W), lambda i: (0, i, 0))
    tok_sds = jax.ShapeDtypeStruct((nb, seq_len, RW_W), F32)
    out_specs = [pl.BlockSpec((n_ops, tt, RW_HD, LANES), lambda i: (0, i, 0, 0))]
    out_shape = [jax.ShapeDtypeStruct((n_ops, seq_len, RW_HD, LANES), F32)]
    scratch = [pltpu.VMEM((len(RW_OPS) // 2, RW_HEADS, nb * tt, LANES), F32)]
    if lat:
        out_specs.append(pl.BlockSpec((tt, SUBLANES, LANES), lambda i: (i, 0, 0)))
        out_shape.append(jax.ShapeDtypeStruct((seq_len, SUBLANES, LANES), F32))
    else:
        scratch.append(pltpu.VMEM((2, nb * tt, LANES), F32))
    return pl.pallas_call(
        functools.partial(_rw_prep_kernel, lat, nb, tt, n_tiles),
        grid=(n_tiles,),
        in_specs=[
            pl.BlockSpec((nb, tt, RW_COLS), lambda i: (0, i, 0)),
            pl.BlockSpec((nb, SUBLANES, RW_COLS), lambda i: (0, jnp.maximum(i * sub - 1, 0), 0)),
            pl.BlockSpec((nb, SUBLANES, RW_COLS), lambda i: (0, jnp.minimum((i + 1) * sub, last), 0)),
        ] + [full(a) for a in (mu, wg, ww, wa, w0, a0, k_k, k_a, r_k, ones, hsum)],
        out_specs=out_specs + [tok_spec, tok_spec],
        out_shape=out_shape + [tok_sds, tok_sds],
        scratch_shapes=scratch,
        compiler_params=_params(("arbitrary",)),
        name="rw_prep_lat" if lat else "rw_prep_ctx",
    )(u_rw, u_rw, u_rw, mu, wg, ww, wa, w0, a0, k_k, k_a, r_k, ones, hsum)


def _allsum_sublanes(x):
    x = x + pltpu.roll(x, 4, 0)
    x = x + pltpu.roll(x, 2, 0)
    return x + pltpu.roll(x, 1, 0)


def _rw_scan_kernel(lat, nv, n_chunks, *refs):
    if lat:
        af_ref, ab_ref, sf_ref, sb_ref, kf_ref, kb_ref, vf_ref, vb_ref, s0_ref, yf_ref, yb_ref, s_scr, sa_scr = refs
    else:
        af_ref, ab_ref, sf_ref, sb_ref, kf_ref, kb_ref, yf_ref, yb_ref, sfin_ref, s_scr, sa_scr = refs
        vf_ref = vb_ref = None
    c = pl.program_id(0)
    chunk = af_ref.shape[1]
    n_part = SCAN_CHAINS // min(nv, SCAN_BLOCKS)
    bcast = lambda row: jnp.broadcast_to(row, (SUBLANES, LANES))

    def total(parts):
        while len(parts) > 1:
            parts = [parts[i] + parts[i + 1] for i in range(0, len(parts), 2)]
        return parts[0]

    @pl.when(c == 0)
    def _():
        if lat:
            s_scr[...] = s0_ref[...]
            for d, (s_ref, first) in enumerate(((sf_ref, 0), (sb_ref, chunk - 1))):
                parts = [None] * SCAN_CHAINS
                for k in range(RW_HD):
                    p = s_scr[d, 0, k] * bcast(s_ref[1, first, pl.ds(k, 1), :])
                    parts[k % SCAN_CHAINS] = p if k < SCAN_CHAINS else parts[k % SCAN_CHAINS] + p
                sa_scr[d, 0] = total(parts)
        else:
            s_scr[...] = jnp.zeros(s_scr.shape, F32)
            sa_scr[...] = jnp.zeros(sa_scr.shape, F32)

    def step(last, tt, carry):
        dirs = ((af_ref, sf_ref, vf_ref, yf_ref, tt, tt + 1, kf_ref),
                (ab_ref, sb_ref, vb_ref, yb_ref, chunk - 1 - tt, chunk - 2 - tt, kb_ref))
        for d, (a_ref, s_ref, v_ref, y_ref, tl, nxt, k_ref) in enumerate(dirs):
            key_row = lambda ref, j, k: bcast(ref[j, tl, pl.ds(k, 1), :])
            if last:
                next_key = lambda k: bcast(k_ref[0, 0, pl.ds(k, 1), :])
            else:
                next_key = lambda k: bcast(s_ref[1, nxt, pl.ds(k, 1), :])
            for first in range(0, nv, SCAN_BLOCKS):
                blocks = range(first, min(first + SCAN_BLOCKS, nv))
                sa = {vb: sa_scr[d, vb] for vb in blocks}
                if lat:
                    vals = {0: v_ref[tl]}
                else:
                    vals = {vb: s_ref[2, tl, vb * SUBLANES:(vb + 1) * SUBLANES, :] for vb in blocks}
                y = {vb: [None] * n_part for vb in blocks}
                san = {vb: [None] * n_part for vb in blocks}
                for k in range(RW_HD):
                    w, kd, bb = key_row(a_ref, 0, k), key_row(a_ref, 1, k), key_row(a_ref, 2, k)
                    r = key_row(s_ref, 0, k)
                    kkn = next_key(k)
                    for vb in blocks:
                        sn = s_scr[d, vb, k] * w - sa[vb] * bb + vals[vb] * kd
                        s_scr[d, vb, k] = sn
                        y[vb][k % n_part] = sn * r if k < n_part else y[vb][k % n_part] + sn * r
                        san[vb][k % n_part] = sn * kkn if k < n_part else san[vb][k % n_part] + sn * kkn
                for vb in blocks:
                    sa_scr[d, vb] = total(san[vb])
                    if lat:
                        y_ref[tl] = total(y[vb])
                    else:
                        y_ref[tl, vb * SUBLANES:(vb + 1) * SUBLANES, :] = total(y[vb])
        return carry

    lax.fori_loop(0, chunk - 1, functools.partial(step, False), 0)
    step(True, chunk - 1, 0)

    if not lat:
        @pl.when(c == n_chunks - 1)
        def _():
            sfin_ref[...] = s_scr[...]


def _rw_scan(lat, ops, vv, s0):
    seq_len = ops.shape[1]
    chunk = SCAN_CHUNK_LAT if lat else SCAN_CHUNK_CTX
    nc = seq_len // chunk
    nv = 1 if lat else RW_HD // SUBLANES
    n_shared = 2 if lat else 3
    s_shape = (2, nv, RW_HD, SUBLANES, LANES)
    state_spec = pl.BlockSpec(s_shape, lambda c: (0,) * 5)
    scratch = [pltpu.VMEM(s_shape, F32), pltpu.VMEM((2, nv, SUBLANES, LANES), F32)]
    op_spec = lambda n, blk, rev: pl.BlockSpec(
        (n, chunk, RW_HD, LANES), lambda c: (blk, (nc - 1 - c) if rev else c, 0, 0))
    last = seq_len - 1
    kk_next = lambda t_of: pl.BlockSpec((1, 1, RW_HD, LANES), lambda c: (RW_OPS.index("kk"), t_of(c), 0, 0))
    in_specs = [op_spec(3, 0, False), op_spec(3, 1, True),
                op_spec(n_shared, 6 // n_shared, False), op_spec(n_shared, 6 // n_shared, True),
                kk_next(lambda c: jnp.minimum((c + 1) * chunk, last)),
                kk_next(lambda c: jnp.maximum((nc - 1 - c) * chunk - 1, 0))]
    args = [ops] * 6
    if lat:
        y_spec = lambda rev: pl.BlockSpec((chunk, SUBLANES, LANES), lambda c: ((nc - 1 - c) if rev else c, 0, 0))
        y_sds = jax.ShapeDtypeStruct((seq_len, SUBLANES, LANES), F32)
        in_specs += [y_spec(False), y_spec(True), state_spec]
        args += [vv, vv, s0]
        out_specs = [y_spec(False), y_spec(True)]
        out_shape = [y_sds, y_sds]
    else:
        y_spec = lambda rev: pl.BlockSpec((chunk, RW_HD, LANES), lambda c: ((nc - 1 - c) if rev else c, 0, 0))
        y_sds = jax.ShapeDtypeStruct((seq_len, RW_HD, LANES), F32)
        out_specs = [y_spec(False), y_spec(True), state_spec]
        out_shape = [y_sds, y_sds, jax.ShapeDtypeStruct(s_shape, F32)]
    return pl.pallas_call(
        functools.partial(_rw_scan_kernel, lat, nv, nc),
        grid=(nc,),
        in_specs=in_specs,
        out_specs=out_specs,
        out_shape=out_shape,
        scratch_shapes=scratch,
        compiler_params=_params(("arbitrary",)),
        name="rw_scan_lat" if lat else "rw_scan_ctx",
    )(*args)


def _rw_post_ctx_kernel(nb, tt, yf_ref, yb_ref, g_ref, gb_ref, lnw_ref, lnb_ref, o_ref):
    for t in range(tt):
        y = yf_ref[t] + yb_ref[t]
        dev = y - jnp.mean(y, axis=0, keepdims=True)
        yn = (dev * lax.rsqrt(jnp.mean(dev * dev, axis=0, keepdims=True) + RW_GN_EPS)).T
        tok = jnp.concatenate([yn[h * nb:(h + 1) * nb, :] for h in range(RW_HEADS)], axis=1)
        o_ref[:, t, :] = (tok * lnw_ref[...] + lnb_ref[...]) * g_ref[:, t, :] + gb_ref[:, t, :]


def _rw_post_lat_kernel(nb, yf_ref, yb_ref, g_ref, gb_ref, lnw_ref, lnb_ref, hs_ref, o_ref):
    grp = LANES // nb
    ys = [yf_ref[:, vi, :] + yb_ref[:, vi, :] for vi in range(SUBLANES)]
    for b in range(nb):
        y = jnp.concatenate([yv[:, b * grp:(b + 1) * grp] for yv in ys], axis=1)
        dev = y - _sel_dot(y, hs_ref[...], EXACT_TERMS) * (1.0 / RW_HD)
        yn = dev * lax.rsqrt(_sel_dot(dev * dev, hs_ref[...], NORM_TERMS) * (1.0 / RW_HD) + RW_GN_EPS)
        o_ref[b] = (yn * lnw_ref[...] + lnb_ref[...]) * g_ref[b] + gb_ref[b]


def _rw_post(lat, yf, yb, g, gb, lnw, lnb, hs):
    nb, seq_len, _ = g.shape
    tt = LANES if lat else SUBLANES
    full = lambda a: pl.BlockSpec(a.shape, lambda i: (0,) * a.ndim)
    tok_spec = pl.BlockSpec((nb, tt, RW_W), lambda i: (0, i, 0))
    if lat:
        y_spec = pl.BlockSpec((tt, SUBLANES, LANES), lambda i: (i, 0, 0))
        body, extra = functools.partial(_rw_post_lat_kernel, nb), [hs]
    else:
        y_spec = pl.BlockSpec((tt, RW_HD, LANES), lambda i: (i, 0, 0))
        body, extra = functools.partial(_rw_post_ctx_kernel, nb, tt), []
    return pl.pallas_call(
        body,
        grid=(seq_len // tt,),
        in_specs=[y_spec, y_spec, tok_spec, tok_spec, full(lnw), full(lnb)] + [full(a) for a in extra],
        out_specs=tok_spec,
        out_shape=jax.ShapeDtypeStruct((nb, seq_len, RW_W), F32),
        compiler_params=_params(("parallel",)),
        name="rw_post_lat" if lat else "rw_post_ctx",
    )(yf, yb, g, gb, lnw, lnb, *extra)


def _rw_mixer(lat, u_rw, s0, prep_params, lnw, lnb, hs):
    nb = u_rw.shape[0]
    nvb = RW_HD // SUBLANES
    if lat:
        ops, vv, g, gb = _rw_prep(True, u_rw, *prep_params)
        grp = LANES // (nb * RW_HEADS)
        s0 = s0.reshape(nb, 2, RW_HEADS, grp, SUBLANES, RW_HD)
        s0 = jnp.transpose(s0, (1, 5, 4, 0, 2, 3)).reshape(2, 1, RW_HD, SUBLANES, LANES)
        yf, yb = _rw_scan(True, ops, vv, s0)
        return _rw_post(True, yf, yb, g, gb, lnw, lnb, hs), None
    ops, g, gb = _rw_prep(False, u_rw, *prep_params)
    yf, yb, sfin = _rw_scan(False, ops, None, None)
    sfin = sfin.reshape(2, nvb, RW_HD, SUBLANES, RW_HEADS, nb)
    sfin = jnp.transpose(sfin, (5, 0, 4, 1, 3, 2)).reshape(nb, 2, RW_HEADS, RW_HD, RW_HD)
    return _rw_post(False, yf, yb, g, gb, lnw, lnb, hs), sfin


def _softplus(x):
    return jnp.maximum(x, 0.0) + jnp.log1p(jnp.exp(-jnp.abs(x)))


def _ssd_kernel(reverse, n_blocks, n_sub, u_ref, up_ref, un_ref, h0_ref, cw_ref, cb_ref, dtb_ref, a_ref,
                tri_ref, exp_ref, *rest):
    if reverse:
        yf_ref, nw_ref, out_ref, hfin_ref, h_scr = rest
    else:
        dsk_ref, out_ref, hfin_ref, h_scr = rest
    s = pl.program_id(1)
    blk = (n_blocks - 1 - s) if reverse else s

    @pl.when(s == 0)
    def _():
        h_scr[...] = h0_ref[...]

    q = SSD_CHUNK
    rows = n_sub * q
    u = u_ref[...]
    xbc = u[:, SSD_W:SSD_W + 512]
    row = _iota(xbc.shape, 0)
    has_prev = (blk > 0).astype(F32)
    has_next = (blk < n_blocks - 1).astype(F32)
    prev = jnp.where(row == 0, up_ref[SUBLANES - 1:SUBLANES, SSD_W:SSD_W + 512] * has_prev, pltpu.roll(xbc, 1, 0))
    nxt = jnp.where(row == rows - 1, un_ref[0:1, SSD_W:SSD_W + 512] * has_next, pltpu.roll(xbc, rows - 1, 0))
    xc = _silu(cw_ref[0:1, :] * prev + cw_ref[1:2, :] * xbc + cw_ref[2:3, :] * nxt + cb_ref[...])
    dt_all = _softplus(u[:, SSD_W + 512:] + dtb_ref[...])
    dt_e_all = _sel_dot(dt_all, exp_ref[...], EXACT_TERMS)
    dta_all = dt_all * a_ref[...]

    ii = _iota((q, q), 0)
    jj = _iota((q, q), 1)
    tri = (jj >= ii) if reverse else (jj <= ii)
    lane = _iota((q, LANES), 1)
    rowc = _iota((LANES, 1), 0)
    state = [h_scr[g] for g in range(2)]
    order = range(n_sub - 1, -1, -1) if reverse else range(n_sub)
    for ci in order:
        rs = slice(ci * q, (ci + 1) * q)
        x = xc[rs, :SSD_W]
        b_all = xc[rs, SSD_W:SSD_W + LANES]
        c_all = xc[rs, SSD_W + LANES:]
        cs = _sel_dot(dta_all[rs], tri_ref[...], EXACT_TERMS, sel_first=True)
        cs_t = cs.T
        cs_e = _sel_dot(cs, exp_ref[...], EXACT_TERMS)
        tot = cs_e[0:1, :] if reverse else cs_e[q - 1:q, :]
        xdt = x * dt_e_all[rs]
        dec_in = jnp.exp(tot - cs_e)
        dec_out = jnp.exp(cs_e)
        cdec = jnp.exp(tot)
        ys = []
        for g in range(2):
            bg = b_all[:, g * 64:(g + 1) * 64]
            cg = c_all[:, g * 64:(g + 1) * 64]
            gm = _bdot_nt(cg, bg)
            sl = slice(g * LANES, (g + 1) * LANES)
            xdt_g = xdt[:, sl]
            yd = []
            for hh in range(2):
                h = 2 * g + hh
                diff = cs[:, h:h + 1] - cs_t[h:h + 1, :]
                lm = jnp.exp(jnp.where(tri, diff, -jnp.inf))
                yd.append(_bdot(gm * lm, xdt_g))
            hg = state[g]
            ys.append(jnp.where(lane < 64, yd[0], yd[1]) + _bdot_nt(cg, hg) * dec_out[:, sl])
            st = _bdot((xdt_g * dec_in[:, sl]).T, bg)
            cd = cdec[:, sl]
            state[g] = hg * jnp.where(rowc < 64, cd[:, 0:1], cd[:, 64:65]) + st
        y = jnp.concatenate(ys, axis=1)
        if reverse:
            val = (yf_ref[rs, :] + y) * _silu(u[rs, :SSD_W])
            out_ref[rs, :] = _rms(val, nw_ref[...])
        else:
            out_ref[rs, :] = y + dsk_ref[...] * x
    for g in range(2):
        h_scr[g] = state[g]
        hfin_ref[g] = state[g]


def _ssd_sweep(reverse, u_ssd, nb, seq_len, h0, cw, cb, dtb, a_neg, extra):
    n_sub = min(SSD_CHUNKS_PER_STEP, seq_len // SSD_CHUNK)
    rows = n_sub * SSD_CHUNK
    nc = seq_len // rows
    sub_per_chunk = rows // SUBLANES
    last_sub = u_ssd.shape[0] // SUBLANES - 1
    chunk = (lambda s: nc - 1 - s) if reverse else (lambda s: s)
    rowblk = lambda b, s: b * nc + chunk(s)
    full = lambda a: pl.BlockSpec(a.shape, lambda b, s: (0,) * a.ndim)
    steps = np.arange(SSD_CHUNK)
    tri = jnp.asarray((steps[None, :] >= steps[:, None]) if reverse else (steps[None, :] <= steps[:, None]), BF16)
    expand = jnp.asarray(np.arange(LANES)[:, None] == np.arange(SSD_W)[None, :] // 64, BF16)
    in_specs = [
        pl.BlockSpec((rows, SSD_PAD), lambda b, s: (rowblk(b, s), 0)),
        pl.BlockSpec((SUBLANES, SSD_PAD), lambda b, s: (jnp.maximum(rowblk(b, s) * sub_per_chunk - 1, 0), 0)),
        pl.BlockSpec((SUBLANES, SSD_PAD),
                     lambda b, s: (jnp.minimum((rowblk(b, s) + 1) * sub_per_chunk, last_sub), 0)),
        pl.BlockSpec((None, 2, LANES, 64), lambda b, s: (b, 0, 0, 0)),
        full(cw), full(cb), full(dtb), full(a_neg), full(tri), full(expand),
    ]
    args = [u_ssd, u_ssd, u_ssd, h0, cw, cb, dtb, a_neg, tri, expand]
    if reverse:
        yf, nw = extra
        in_specs += [pl.BlockSpec((rows, SSD_W), lambda b, s: (rowblk(b, s), 0)), full(nw)]
        args += [yf, nw]
    else:
        in_specs += [full(extra)]
        args += [extra]
    return pl.pallas_call(
        functools.partial(_ssd_kernel, reverse, nc, n_sub),
        grid=(nb, nc),
        in_specs=in_specs,
        out_specs=[pl.BlockSpec((rows, SSD_W), lambda b, s: (rowblk(b, s), 0)),
                   pl.BlockSpec((None, 2, LANES, 64), lambda b, s: (b, 0, 0, 0))],
        out_shape=[jax.ShapeDtypeStruct((nb * seq_len, SSD_W), F32),
                   jax.ShapeDtypeStruct((nb, 2, LANES, 64), F32)],
        scratch_shapes=[pltpu.VMEM((2, LANES, 64), F32)],
        compiler_params=_params(("parallel", "arbitrary")),
        name="ssd_bwd" if reverse else "ssd_fwd",
    )(*args)


def _ssd_mixer(u_ssd, nb, seq_len, h0, cw, cb, dtb, a_neg, dsk, nw):
    yf, hf = _ssd_sweep(False, u_ssd, nb, seq_len, h0[:, 0], cw, cb, dtb[0], a_neg[0], dsk)
    out, hb = _ssd_sweep(True, u_ssd, nb, seq_len, h0[:, 1], cw, cb, dtb[1], a_neg[1], (yf, nw))
    return out, jnp.stack([hf, hb], axis=1).reshape(nb, 2, 4, 64, 64)


def _route(scores, bias):
    n_grp = N_EXPERTS // EXP_PER_GROUP
    sel = scores + bias
    sc = [scores[e:e + 1, :] for e in range(N_EXPERTS)]
    sl = [sel[e:e + 1, :] for e in range(N_EXPERTS)]
    as_int = lambda cond: jnp.where(cond, 1, 0)
    top2, grp = [], []
    for g in range(n_grp):
        members = range(g * EXP_PER_GROUP, (g + 1) * EXP_PER_GROUP)
        total = None
        for e in members:
            ahead = [as_int(sl[o] >= sl[e]) if o < e else as_int(sl[o] > sl[e]) for o in members if o != e]
            top2.append(ahead[0] + ahead[1] + ahead[2] < 2)
            kept = jnp.where(top2[e], sl[e], 0.0)
            total = kept if total is None else total + kept
        grp.append(total)
    rows = []
    for g in range(n_grp):
        beaten = [as_int(grp[o] >= grp[g]) if o < g else as_int(grp[o] > grp[g]) for o in range(n_grp) if o != g]
        best = beaten[0] + beaten[1] + beaten[2] == 0
        members = range(g * EXP_PER_GROUP, (g + 1) * EXP_PER_GROUP)
        chosen = [jnp.where(best, as_int(top2[e]), 0) > 0 for e in members]
        cw = [jnp.where(c, sc[e], 0.0) for c, e in zip(chosen, members)]
        den = cw[0] + cw[1] + cw[2] + cw[3]
        rows += [jnp.where(c, w / jnp.where(c, den, 1.0), 0.0) for c, w in zip(chosen, cw)]
    return jnp.concatenate(rows, axis=0)


def _outproj_kernel(oatt_ref, orw_ref, ossd_ref, x_ref, mod_ref, wout_ref,
                    n2_ref, rw_ref, rb_ref, x1_ref, h2_ref, comb_ref):
    o = jnp.concatenate([oatt_ref[...], orw_ref[...], ossd_ref[...]], axis=1)
    x1 = x_ref[...] + mod_ref[2] * _bdot(o, wout_ref[...])
    h2 = _rms(x1, n2_ref[...]) * (1.0 + mod_ref[4]) + mod_ref[3]
    comb = _route(_sigmoid(_bdot_nt(rw_ref[...], h2)), rb_ref[...])
    x1_ref[...] = x1
    h2_ref[...] = h2.astype(BF16)
    comb_ref[...] = jnp.concatenate([comb, jnp.zeros((LANES - N_EXPERTS, comb.shape[1]), F32)], axis=0).T


def _outproj(o_att, o_rw, o_ssd, x, mod_l, row_fn, w_out_bf, n2, router_w, router_b):
    n_tok = x.shape[0]
    tile = lambda w: pl.BlockSpec((TOKEN_TILE, w), lambda i: (i, 0))
    full = lambda a: pl.BlockSpec(a.shape, lambda i: (0,) * a.ndim)
    return pl.pallas_call(
        _outproj_kernel,
        grid=(n_tok // TOKEN_TILE,),
        in_specs=[tile(ATT_W), tile(RW_W), tile(SSD_W), tile(D_MODEL),
                  _mod_spec(row_fn)] + [full(a) for a in (w_out_bf, n2, router_w, router_b)],
        out_specs=[tile(D_MODEL), tile(D_MODEL), tile(LANES)],
        out_shape=[jax.ShapeDtypeStruct((n_tok, D_MODEL), F32),
                   jax.ShapeDtypeStruct((n_tok, D_MODEL), BF16),
                   jax.ShapeDtypeStruct((n_tok, LANES), F32)],
        compiler_params=_params(("parallel",)),
        name="outproj",
    )(o_att, o_rw, o_ssd, x, mod_l, w_out_bf, n2, router_w, router_b)


def _moe_kernel(h2_ref, comb_ref, x1_ref, mod_ref, wg_ref, wu_ref, wd_ref, o_ref, acc_ref):
    s = pl.program_id(1)

    @pl.when(s == 0)
    def _():
        acc_ref[...] = jnp.zeros_like(acc_ref)

    h2 = h2_ref[...]
    comb = comb_ref[...]
    lane = _iota(comb.shape, 1)
    weighted = []
    for j in range(MOE_EXPERTS_PER_STEP):
        hid = _silu(jnp.dot(h2, wg_ref[j], preferred_element_type=F32)) * jnp.dot(
            h2, wu_ref[j], preferred_element_type=F32)
        wcol = jnp.sum(jnp.where(lane == s * MOE_EXPERTS_PER_STEP + j, comb, 0.0), axis=1, keepdims=True)
        weighted.append((hid * wcol).astype(BF16))
    acc_ref[...] += jnp.dot(jnp.concatenate(weighted, axis=1), wd_ref[...].reshape(-1, D_MODEL),
                            preferred_element_type=F32)

    @pl.when(s == N_EXPERTS // MOE_EXPERTS_PER_STEP - 1)
    def _():
        o_ref[...] = x1_ref[...] + mod_ref[5] * acc_ref[...]


def _moe(h2, comb, x1, mod_l, row_fn, wg_bf, wu_bf, wd_bf):
    n_tok = x1.shape[0]
    tile = lambda w: pl.BlockSpec((MOE_TILE, w), lambda i, e: (i, 0))
    return pl.pallas_call(
        _moe_kernel,
        grid=(n_tok // MOE_TILE, N_EXPERTS // MOE_EXPERTS_PER_STEP),
        in_specs=[tile(D_MODEL), tile(LANES), tile(D_MODEL), _mod_spec(row_fn),
                  pl.BlockSpec((MOE_EXPERTS_PER_STEP, D_MODEL, D_FF), lambda i, e: (e, 0, 0)),
                  pl.BlockSpec((MOE_EXPERTS_PER_STEP, D_MODEL, D_FF), lambda i, e: (e, 0, 0)),
                  pl.BlockSpec((MOE_EXPERTS_PER_STEP, D_FF, D_MODEL), lambda i, e: (e, 0, 0))],
        out_specs=tile(D_MODEL),
        out_shape=jax.ShapeDtypeStruct((n_tok, D_MODEL), F32),
        scratch_shapes=[pltpu.VMEM((MOE_TILE, D_MODEL), F32)],
        compiler_params=_params(("parallel", "arbitrary")),
        name="moe",
    )(h2, comb, x1, mod_l, wg_bf, wu_bf, wd_bf)


def _rope_tables(seq_len):
    t = np.arange(seq_len)
    pos = np.stack([t // GRID_W, t % GRID_W], 0).astype(np.float32)
    n_freq = HEAD_DIM // 4
    inv = jnp.asarray(ROPE_BASE, F32) ** (-jnp.arange(n_freq, dtype=F32) / n_freq)
    ang = jnp.asarray(pos)[:, :, None] * inv
    cos, sin = jnp.cos(ang), jnp.sin(ang)
    cos_h = jnp.concatenate([cos[0], cos[0], cos[1], cos[1]], -1)
    sin_h = jnp.concatenate([-sin[0], sin[0], -sin[1], sin[1]], -1)
    return jnp.tile(cos_h, (1, 2)), jnp.tile(sin_h, (1, 2))


def _pad_rows(w, start, total):
    return jnp.pad(w, ((0, 0),) * (w.ndim - 2) + ((start, total - start - w.shape[-2]), (0, 0)))


def _value_order(x, axis, start, grp):
    if grp is None:
        return x
    axis %= x.ndim
    sl = lambda a, b: lax.slice_in_dim(x, a, b, axis=axis)
    mid = sl(start, start + RW_W)
    shape = mid.shape
    mid = mid.reshape(shape[:axis] + (RW_HEADS, grp, RW_HD // grp) + shape[axis + 1:])
    order = tuple(range(axis)) + (axis + 2, axis, axis + 1) + tuple(range(axis + 3, mid.ndim))
    mid = jnp.transpose(mid, order).reshape(shape)
    return jnp.concatenate([sl(0, start), mid, sl(start + RW_W, x.shape[axis])], axis=axis)


def _rw_layer_params(l, P, grp):
    row = lambda a: a.reshape(1, -1)
    v0 = 2 * RW_W
    w_in = jnp.pad(P["w_in"][l], ((0, 0), (0, SSD_PAD - SSD_COLS))).astype(BF16)
    head = np.arange(RW_W) // RW_HD if grp is None else (np.arange(RW_W) // grp) % RW_HEADS
    rp = {}
    rp["w_in"] = _value_order(w_in, 1, ATT_COLS + v0, grp)
    rp["w_out"] = _value_order(P["w_out"][l].astype(BF16), 0, ATT_W, grp)
    rp["prep"] = (
        _value_order(P["rw_mu"][l], 1, v0, grp),
        _pad_rows(_value_order(P["rw_g_up"][l], 1, 0, grp), 64, LANES),
        _pad_rows(P["rw_w_up"][l], 0, LANES),
        _pad_rows(P["rw_a_up"][l], 32, LANES),
        P["rw_w0"][l], P["rw_a0"][l],
        row(P["rw_k_k"][l]), row(P["rw_k_a"][l]), row(P["rw_r_k"][l]),
        _head_selector(np.arange(RW_W) // RW_HD), _head_selector(np.arange(RW_W) // RW_HD, head),
    )
    rp["lnw"] = _value_order(row(P["rw_ln_w"][l]), 1, 0, grp)
    rp["lnb"] = _value_order(row(P["rw_ln_b"][l]), 1, 0, grp)
    rp["hs"] = _head_selector(head)
    return rp


def _layer_params(l, P):
    row = lambda a: a.reshape(1, -1)
    lp = {}
    lp["n1"] = row(P["norm1_w"][l])
    lp["n2"] = row(P["norm2_w"][l])
    lp["qw"] = row(jnp.tile(P["q_norm_w"][l], ATT_HEADS))
    lp["kw"] = row(jnp.tile(P["k_norm_w"][l], ATT_KV_HEADS))
    lp["sink"] = P["attn_sink"][l]
    pad_heads = lambda a: jnp.pad(a, ((0, 0), (0, LANES - a.shape[1]))).reshape(2, 1, LANES)
    lp["ssd"] = (
        P["ssd_conv_w"][l], row(P["ssd_conv_b"][l]),
        pad_heads(P["ssd_dt_bias"][l]), pad_heads(-jnp.exp(P["ssd_a_log"][l])),
        row(jnp.repeat(P["ssd_d"][l], 64)), row(P["ssd_norm_w"][l]),
    )
    lp["wg"] = P["exp_gate"][l].astype(BF16)
    lp["wu"] = P["exp_up"][l].astype(BF16)
    lp["wd"] = P["exp_down"][l].astype(BF16)
    return lp


def _trunk(x, nb, seq_len, is_ctx, mod, layers, rw_layers, router, cache_k, cache_v, state_rwkv, state_ssd):
    tiles_per_seq = seq_len // TOKEN_TILE
    moe_per_seq = max(seq_len // MOE_TILE, 1)
    if is_ctx:
        row_tok = lambda i: 0
        row_moe = lambda i: 0
        rope = None
    else:
        row_tok = lambda i: 1 + i // tiles_per_seq
        row_moe = lambda i: 1 + i // moe_per_seq
        rope = _rope_tables(seq_len)
    new_k, new_v, new_rw, new_ssd = [], [], [], []
    for l, (lp, rp) in enumerate(zip(layers, rw_layers)):
        q, k, v, u_rw, u_ssd = _inproj(x, mod[l], row_tok, lp["n1"], rp["w_in"], lp["qw"], lp["kw"], rope, seq_len)
        if is_ctx:
            o_att = _attn_ctx(lp["sink"], q, k, v, nb, seq_len)
            rw0 = None
            ssd0 = jnp.zeros((nb, 2, 2, LANES, 64), F32)
        else:
            kc = cache_k[:, l].reshape(nb, -1, KV_W)
            vc = cache_v[:, l].reshape(nb, -1, KV_W)
            o_att = _attn_lat(lp["sink"], q, k, v, kc, vc, nb, seq_len)
            rw0 = state_rwkv[:, l]
            ssd0 = state_ssd[:, l].reshape(nb, 2, 2, LANES, 64)
        o_rw, s_rw = _rw_mixer(not is_ctx, u_rw.reshape(nb, seq_len, RW_COLS), rw0, rp["prep"],
                               rp["lnw"], rp["lnb"], rp["hs"])
        o_ssd, s_ssd = _ssd_mixer(u_ssd, nb, seq_len, ssd0, *lp["ssd"])
        if is_ctx:
            new_k.append(k.reshape(nb, seq_len, ATT_KV_HEADS, HEAD_DIM))
            new_v.append(v.reshape(nb, seq_len, ATT_KV_HEADS, HEAD_DIM))
            new_rw.append(s_rw)
            new_ssd.append(s_ssd)
        x1, h2, comb = _outproj(o_att, o_rw.reshape(nb * seq_len, RW_W), o_ssd, x, mod[l], row_tok, rp["w_out"],
                                lp["n2"], *router)
        x = _moe(h2, comb, x1, mod[l], row_moe, lp["wg"], lp["wu"], lp["wd"])
    if is_ctx:
        return x, tuple(jnp.stack(t, 1) for t in (new_k, new_v, new_rw, new_ssd))
    return x, None


def kernel(x_prompt, x_sample, cache_k, cache_v, state_rwkv, state_ssd, c, c_ctx, w_mod, b_mod, norm1_w, norm2_w, w_in, w_out, q_norm_w, k_norm_w, attn_sink, rw_mu, rw_w0, rw_w_up, rw_a0, rw_a_up, rw_g_up, rw_k_k, rw_k_a, rw_r_k, rw_ln_w, rw_ln_b, ssd_conv_w, ssd_conv_b, ssd_dt_bias, ssd_a_log, ssd_d, ssd_norm_w, router_w, router_bias, exp_gate, exp_up, exp_down):
    P = dict(norm1_w=norm1_w, norm2_w=norm2_w, w_in=w_in, w_out=w_out, q_norm_w=q_norm_w, k_norm_w=k_norm_w,
             attn_sink=attn_sink, rw_mu=rw_mu, rw_w0=rw_w0, rw_w_up=rw_w_up, rw_a0=rw_a0, rw_a_up=rw_a_up,
             rw_g_up=rw_g_up, rw_k_k=rw_k_k, rw_k_a=rw_k_a, rw_r_k=rw_r_k, rw_ln_w=rw_ln_w, rw_ln_b=rw_ln_b,
             ssd_conv_w=ssd_conv_w, ssd_conv_b=ssd_conv_b, ssd_dt_bias=ssd_dt_bias, ssd_a_log=ssd_a_log,
             ssd_d=ssd_d, ssd_norm_w=ssd_norm_w, exp_gate=exp_gate, exp_up=exp_up, exp_down=exp_down)
    nb_ctx, seq_ctx, _ = x_prompt.shape
    nb_lat, seq_lat, _ = x_sample.shape
    assert nb_lat + 1 <= SUBLANES and seq_ctx % TOKEN_TILE == 0 and seq_lat % MOE_TILE == 0
    assert nb_ctx * RW_HEADS == LANES and LANES % (nb_lat * RW_HEADS) == 0
    cond8 = jnp.zeros((SUBLANES, D_MODEL), F32).at[0].set(c_ctx).at[1:1 + nb_lat].set(c)
    mod = _modulation(cond8, w_mod, b_mod)
    layers = [_layer_params(l, P) for l in range(DEPTH)]
    rw_ctx = [_rw_layer_params(l, P, None) for l in range(DEPTH)]
    rw_lat = [_rw_layer_params(l, P, LANES // (nb_lat * RW_HEADS)) for l in range(DEPTH)]
    router = (router_w.T.astype(BF16), router_bias.reshape(N_EXPERTS, 1))
    y_prompt, ctx_state = _trunk(x_prompt.reshape(-1, D_MODEL), nb_ctx, seq_ctx, True, mod, layers, rw_ctx, router,
                                 None, None, None, None)
    y_sample, _ = _trunk(x_sample.reshape(-1, D_MODEL), nb_lat, seq_lat, False, mod, layers, rw_lat, router,
                         cache_k, cache_v, state_rwkv, state_ssd)
    return (y_prompt.reshape(x_prompt.shape), y_sample.reshape(x_sample.shape)) + ctx_state
```

```python
import functools
import math

import jax
import jax.numpy as jnp
import numpy as np
from jax import lax
from jax.experimental import pallas as pl
from jax.experimental.pallas import tpu as pltpu

F32 = jnp.float32
BF16 = jnp.bfloat16
HIGHEST = lax.Precision.HIGHEST

D_MODEL = 1024
DEPTH = 2
GRID_W = 64
EPS = 1e-6
ATT_HEADS = 8
ATT_KV_HEADS = 2
HEAD_DIM = 64
ATT_W = 512
KV_W = 128
ATT_COLS = 768
WINDOW = 128
BLK = 128
ATT_QBLK = 256
ROPE_BASE = 10000.0
RW_HEADS = 4
RW_HD = 64
RW_W = 256
RW_COLS = 896
RW_DECAY_SCALE = 0.606531
RW_GN_EPS = 64e-5
SSD_W = 256
SSD_CHUNK = 128
SSD_CHUNKS_PER_STEP = 2
SSD_COLS = 772
SSD_PAD = 896
IN_PAD = ATT_COLS + RW_COLS + SSD_PAD
N_EXPERTS = 16
EXP_PER_GROUP = 4
D_FF = 256

LANES = 128
SUBLANES = 8
TOKEN_TILE = 512
MOE_TILE = 1024
MOE_EXPERTS_PER_STEP = 4
SCAN_CHUNK_CTX = 16
SCAN_CHUNK_LAT = 16
SCAN_BLOCKS = 4
SCAN_CHAINS = 8
VMEM_LIMIT = 48 * 1024 * 1024


def _params(sem):
    return pltpu.CompilerParams(dimension_semantics=sem, vmem_limit_bytes=VMEM_LIMIT)


def _bdot(a, b):
    return jnp.dot(a.astype(BF16), b.astype(BF16), preferred_element_type=F32)


def _bdot_nt(a, b):
    return lax.dot_general(a.astype(BF16), b.astype(BF16), (((1,), (1,)), ((), ())),
                           preferred_element_type=F32)


def _sel_dot(x, sel, terms, sel_first=False):
    acc = None
    rem = x
    for i in range(terms):
        piece = rem.astype(BF16)
        part = jnp.dot(sel, piece, preferred_element_type=F32) if sel_first else jnp.dot(
            piece, sel, preferred_element_type=F32)
        acc = part if acc is None else acc + part
        if i + 1 < terms:
            rem = rem - piece.astype(F32)
    return acc


def _iota(shape, axis):
    return lax.broadcasted_iota(jnp.int32, shape, axis)


def _head_selector(heads_in, heads_out=None):
    heads_out = heads_in if heads_out is None else heads_out
    return jnp.asarray(np.asarray(heads_in)[:, None] == np.asarray(heads_out)[None, :], BF16)


NORM_TERMS = 2
EXACT_TERMS = 3


def _sigmoid(x):
    return 1.0 / (1.0 + jnp.exp(-x))


def _silu(x):
    return x * _sigmoid(x)


def _rms(x, w):
    return x * lax.rsqrt(jnp.mean(x * x, -1, keepdims=True) + EPS) * w


def _mod_kernel(c_ref, w_ref, b_ref, o_ref):
    o_ref[...] = _bdot(_silu(c_ref[...]), w_ref[...]) + b_ref[...]


def _modulation(cond8, w_mod, b_mod):
    out = pl.pallas_call(
        _mod_kernel,
        grid=(DEPTH, 6),
        in_specs=[
            pl.BlockSpec((SUBLANES, D_MODEL), lambda l, j: (0, 0)),
            pl.BlockSpec((None, D_MODEL, D_MODEL), lambda l, j: (l, 0, j)),
            pl.BlockSpec((None, 1, D_MODEL), lambda l, j: (l, 0, j)),
        ],
        out_specs=pl.BlockSpec((None, None, SUBLANES, D_MODEL), lambda l, j: (l, j, 0, 0)),
        out_shape=jax.ShapeDtypeStruct((DEPTH, 6, SUBLANES, D_MODEL), F32),
        compiler_params=_params(("arbitrary", "arbitrary")),
        name="modulation",
    )(cond8, w_mod, b_mod.reshape(DEPTH, 1, 6 * D_MODEL))
    return out.reshape(DEPTH, 6, SUBLANES, 1, D_MODEL)


def _mod_spec(row_fn):
    return pl.BlockSpec((6, None, 1, D_MODEL), lambda *idx: (0, row_fn(idx[0]), 0, 0))


def _swap16(x):
    w = x.shape[1]
    first = (_iota(x.shape, 1) & 31) < 16
    return jnp.where(first, pltpu.roll(x, w - 16, 1), pltpu.roll(x, 16, 1))


def _inproj_kernel(use_rope, x_ref, mod_ref, n1_ref, w_ref, qw_ref, kw_ref, hq_ref, *rest):
    if use_rope:
        cos_ref, sin_ref, q_ref, k_ref, v_ref, urw_ref, ussd_ref = rest
    else:
        q_ref, k_ref, v_ref, urw_ref, ussd_ref = rest
    h = _rms(x_ref[...], n1_ref[...]) * (1.0 + mod_ref[1]) + mod_ref[0]
    u = _bdot(h, w_ref[...])
    q = u[:, :ATT_W]
    k = u[:, ATT_W:ATT_W + KV_W]
    q = q * lax.rsqrt(_sel_dot(q * q, hq_ref[...], NORM_TERMS) * (1.0 / HEAD_DIM) + EPS) * qw_ref[...]
    k = k * lax.rsqrt(_sel_dot(k * k, hq_ref[:KV_W, :KV_W], NORM_TERMS) * (1.0 / HEAD_DIM) + EPS) * kw_ref[...]
    if use_rope:
        cos = cos_ref[...]
        sin = sin_ref[...]
        k = k * cos + _swap16(k) * sin
        cos4 = jnp.concatenate([cos] * 4, axis=1)
        sin4 = jnp.concatenate([sin] * 4, axis=1)
        q = q * cos4 + _swap16(q) * sin4
    q_ref[...] = q
    k_ref[...] = k
    v_ref[...] = u[:, ATT_W + KV_W:ATT_COLS]
    urw_ref[...] = u[:, ATT_COLS:ATT_COLS + RW_COLS]
    ussd_ref[...] = u[:, ATT_COLS + RW_COLS:]


def _inproj(x, mod_l, row_fn, n1, w_in_bf, qw, kw, rope, seq_len):
    n_tok = x.shape[0]
    tiles_per_seq = seq_len // TOKEN_TILE
    in_specs = [
        pl.BlockSpec((TOKEN_TILE, D_MODEL), lambda i: (i, 0)),
        _mod_spec(row_fn),
        pl.BlockSpec((1, D_MODEL), lambda i: (0, 0)),
        pl.BlockSpec((D_MODEL, IN_PAD), lambda i: (0, 0)),
        pl.BlockSpec((1, ATT_W), lambda i: (0, 0)),
        pl.BlockSpec((1, KV_W), lambda i: (0, 0)),
        pl.BlockSpec((ATT_W, ATT_W), lambda i: (0, 0)),
    ]
    args = [x, mod_l, n1, w_in_bf, qw, kw, _head_selector(np.arange(ATT_W) // HEAD_DIM)]
    if rope is not None:
        in_specs += [pl.BlockSpec((TOKEN_TILE, LANES), lambda i: (i % tiles_per_seq, 0))] * 2
        args += list(rope)
    widths = (ATT_W, KV_W, KV_W, RW_COLS, SSD_PAD)
    return pl.pallas_call(
        functools.partial(_inproj_kernel, rope is not None),
        grid=(n_tok // TOKEN_TILE,),
        in_specs=in_specs,
        out_specs=[pl.BlockSpec((TOKEN_TILE, w), lambda i: (i, 0)) for w in widths],
        out_shape=[jax.ShapeDtypeStruct((n_tok, w), F32) for w in widths],
        compiler_params=_params(("parallel",)),
        name="inproj",
    )(*args)


LOG2E = 1.4426950408889634


def _attend_heads(sink_ref, q, parts):
    groups = ATT_HEADS // ATT_KV_HEADS
    outs = []
    for hk in range(ATT_KV_HEADS):
        cols = slice(hk * HEAD_DIM, (hk + 1) * HEAD_DIM)
        keys = [k[:, cols].astype(BF16) for k, _, _ in parts]
        vals = [v[:, cols].astype(BF16) for _, v, _ in parts]
        for g in range(groups):
            hd = hk * groups + g
            qh = q[:, hd * HEAD_DIM:(hd + 1) * HEAD_DIM].astype(BF16)
            sink = sink_ref[hd] * LOG2E
            scores = []
            for kh, (_, _, mask) in zip(keys, parts):
                s = lax.dot_general(qh, kh, (((1,), (1,)), ((), ())),
                                    preferred_element_type=F32) * (HEAD_DIM ** -0.5 * LOG2E)
                scores.append(s if mask is None else jnp.where(mask, s, -1e30))
            m = jnp.max(scores[0], -1, keepdims=True)
            for s in scores[1:]:
                m = jnp.maximum(m, jnp.max(s, -1, keepdims=True))
            m = jnp.maximum(m, sink)
            den = jnp.exp2(sink - m)
            o = None
            for s, vh in zip(scores, vals):
                p = jnp.exp2(s - m)
                den = den + jnp.sum(p, -1, keepdims=True)
                pv = jnp.dot(p.astype(BF16), vh, preferred_element_type=F32)
                o = pv if o is None else o + pv
            outs.append(o / den)
    return jnp.concatenate(outs, axis=1)


def _attn_ctx_kernel(sink_ref, q_ref, k_ref, v_ref, o_ref):
    o_ref[...] = _attend_heads(sink_ref, q_ref[...], [(k_ref[...], v_ref[...], None)])


def _attn_ctx(sink, q, k, v, nb, seq_len):
    return pl.pallas_call(
        _attn_ctx_kernel,
        grid=(nb,),
        in_specs=[
            pl.BlockSpec(memory_space=pltpu.SMEM),
            pl.BlockSpec((seq_len, ATT_W), lambda b: (b, 0)),
            pl.BlockSpec((seq_len, KV_W), lambda b: (b, 0)),
            pl.BlockSpec((seq_len, KV_W), lambda b: (b, 0)),
        ],
        out_specs=pl.BlockSpec((seq_len, ATT_W), lambda b: (b, 0)),
        out_shape=jax.ShapeDtypeStruct((nb * seq_len, ATT_W), F32),
        compiler_params=_params(("parallel",)),
        name="attn_ctx",
    )(sink, q, k, v)


def _attn_lat_kernel(seq_len, sink_ref, q_ref, k_ref, v_ref, kc_ref, vc_ref, o_ref):
    i = pl.program_id(1)
    band = ATT_QBLK + 2 * BLK
    start = pl.multiple_of(jnp.clip(i * ATT_QBLK - BLK, 0, seq_len - band), BLK)
    dist = jnp.abs(i * ATT_QBLK + _iota((ATT_QBLK, band), 0) - (start + _iota((ATT_QBLK, band), 1)))
    parts = [(k_ref[pl.ds(start, band), :], v_ref[pl.ds(start, band), :], dist <= WINDOW),
             (kc_ref[...], vc_ref[...], None)]
    o_ref[...] = _attend_heads(sink_ref, q_ref[...], parts)


def _attn_lat(sink, q, k, v, kc, vc, nb, seq_len):
    nblk = seq_len // ATT_QBLK
    past = kc.shape[1]
    return pl.pallas_call(
        functools.partial(_attn_lat_kernel, seq_len),
        grid=(nb, nblk),
        in_specs=[
            pl.BlockSpec(memory_space=pltpu.SMEM),
            pl.BlockSpec((ATT_QBLK, ATT_W), lambda b, i: (b * nblk + i, 0)),
            pl.BlockSpec((seq_len, KV_W), lambda b, i: (b, 0)),
            pl.BlockSpec((seq_len, KV_W), lambda b, i: (b, 0)),
            pl.BlockSpec((None, past, KV_W), lambda b, i: (b, 0, 0)),
            pl.BlockSpec((None, past, KV_W), lambda b, i: (b, 0, 0)),
        ],
        out_specs=pl.BlockSpec((ATT_QBLK, ATT_W), lambda b, i: (b * nblk + i, 0)),
        out_shape=jax.ShapeDtypeStruct((nb * seq_len, ATT_W), F32),
        compiler_params=_params(("parallel", "arbitrary")),
        name="attn_lat",
    )(sink, q, k, v, kc, vc)


RW_OPS = ("w_f", "kd_f", "b_f", "w_b", "kd_b", "b_b", "r", "kk")


def _rw_prep_kernel(lat, nb, tt, n_tiles, u_ref, up_ref, un_ref, mu_ref, wg_ref, ww_ref, wa_ref,
                    w0_ref, a0_ref, kk_ref, ka_ref, rk_ref, ones_ref, hsum_ref, *rest):
    if lat:
        ops_ref, vv_ref, g_ref, gb_ref, xs = rest
    else:
        ops_ref, g_ref, gb_ref, xs, xv = rest
    i = pl.program_id(0)
    has_prev = (i > 0).astype(F32)
    has_next = (i < n_tiles - 1).astype(F32)
    row = _iota((tt, RW_COLS), 0)
    mu0 = mu_ref[0:1, :]
    mu1 = mu_ref[1:2, :]
    mixed = []
    for b in range(nb):
        ub = u_ref[b]
        prev = jnp.where(row == 0, up_ref[b, SUBLANES - 1:SUBLANES, :] * has_prev, pltpu.roll(ub, 1, 0))
        nxt = jnp.where(row == tt - 1, un_ref[b, 0:1, :] * has_next, pltpu.roll(ub, tt - 1, 0))
        mixed.append(ub + mu0 * (prev - ub) + mu1 * (nxt - ub))
    u = jnp.concatenate(mixed, axis=0)
    r = u[:, :RW_W]
    k = u[:, RW_W:2 * RW_W]
    v = u[:, 2 * RW_W:3 * RW_W]
    low = u[:, 3 * RW_W:]
    kk = k * kk_ref[...]
    kk = kk * lax.rsqrt(_sel_dot(kk * kk, ones_ref[...], NORM_TERMS) + 1e-12)
    tanh_low = jnp.tanh(low)
    vals = [None] * len(RW_OPS)
    for d in range(2):
        w = jnp.exp(-RW_DECAY_SCALE * _sigmoid(w0_ref[d:d + 1, :] + _bdot(tanh_low, ww_ref[d])))
        a = _sigmoid(a0_ref[d:d + 1, :] + _bdot(low, wa_ref[d]))
        vals[3 * d:3 * d + 3] = [w, k * (1.0 + (a - 1.0) * ka_ref[...]), kk * a]
    vals[6:8] = [r, kk]
    g = _bdot(_sigmoid(low), wg_ref[...])
    gb = _sel_dot(r * k * rk_ref[...], hsum_ref[...], EXACT_TERMS) * v * g
    for b in range(nb):
        g_ref[b] = g[b * tt:(b + 1) * tt]
        gb_ref[b] = gb[b * tt:(b + 1) * tt]

    n_pairs = len(RW_OPS) // 2
    low_half = _iota((nb * tt, LANES), 1) < RW_HD
    for p in range(n_pairs):
        for hp in range(2):
            first = vals[2 * p][:, hp * LANES:(hp + 1) * LANES]
            second = vals[2 * p + 1][:, hp * LANES:(hp + 1) * LANES]
            xs[p, 2 * hp] = jnp.where(low_half, first, pltpu.roll(second, RW_HD, 1))
            xs[p, 2 * hp + 1] = jnp.where(low_half, pltpu.roll(first, RW_HD, 1), second)

    def emit(p, t, tiles):
        both = jnp.concatenate(tiles, axis=0).T
        ops_ref[2 * p, t] = both[:RW_HD]
        ops_ref[2 * p + 1, t] = both[RW_HD:]

    if lat:
        def relayout(t, carry):
            for p in range(n_pairs):
                emit(p, t, [jnp.broadcast_to(xs[p, h, pl.ds(b * tt + t, 1), :], (SUBLANES, LANES))
                            for b in range(nb) for h in range(RW_HEADS)])
            return carry

        lax.fori_loop(0, tt, relayout, 0)
        grp = LANES // nb
        for vi in range(SUBLANES):
            vv_ref[:, vi, :] = jnp.concatenate(
                [v[b * tt:(b + 1) * tt, vi * grp:(vi + 1) * grp] for b in range(nb)], axis=1)
    else:
        xv[0] = v[:, :LANES]
        xv[1] = v[:, LANES:]
        for t in range(tt):
            at_t = pl.ds(t, nb, stride=tt)
            for p in range(n_pairs):
                emit(p, t, [xs[p, h, at_t, :] for h in range(RW_HEADS)])
            halves = [xv[hp, at_t, :] for hp in range(2)]
            ops_ref[2 * n_pairs, t] = jnp.concatenate(
                [halves[hp][:, hh * RW_HD:(hh + 1) * RW_HD] for hp in range(2) for hh in range(2)], axis=0).T


def _rw_prep(lat, u_rw, mu, wg, ww, wa, w0, a0, k_k, k_a, r_k, ones, hsum):
    nb, seq_len, _ = u_rw.shape
    tt = (LANES // nb) if lat else SUBLANES
    n_tiles = seq_len // tt
    n_ops = len(RW_OPS) + (0 if lat else 1)
    sub = tt // SUBLANES
    last = seq_len // SUBLANES - 1
    full = lambda a: pl.BlockSpec(a.shape, lambda i: (0,) * a.ndim)
    tok_spec = pl.BlockSpec((nb, tt, RW_W), lambda i: (0, i, 0))
    tok_sds = jax.ShapeDtypeStruct((nb, seq_len, RW_W), F32)
    out_specs = [pl.BlockSpec((n_ops, tt, RW_HD, LANES), lambda i: (0, i, 0, 0))]
    out_shape = [jax.ShapeDtypeStruct((n_ops, seq_len, RW_HD, LANES), F32)]
    scratch = [pltpu.VMEM((len(RW_OPS) // 2, RW_HEADS, nb * tt, LANES), F32)]
    if lat:
        out_specs.append(pl.BlockSpec((tt, SUBLANES, LANES), lambda i: (i, 0, 0)))
        out_shape.append(jax.ShapeDtypeStruct((seq_len, SUBLANES, LANES), F32))
    else:
        scratch.append(pltpu.VMEM((2, nb * tt, LANES), F32))
    return pl.pallas_call(
        functools.partial(_rw_prep_kernel, lat, nb, tt, n_tiles),
        grid=(n_tiles,),
        in_specs=[
            pl.BlockSpec((nb, tt, RW_COLS), lambda i: (0, i, 0)),
            pl.BlockSpec((nb, SUBLANES, RW_COLS), lambda i: (0, jnp.maximum(i * sub - 1, 0), 0)),
            pl.BlockSpec((nb, SUBLANES, RW_COLS), lambda i: (0, jnp.minimum((i + 1) * sub, last), 0)),
        ] + [full(a) for a in (mu, wg, ww, wa, w0, a0, k_k, k_a, r_k, ones, hsum)],
        out_specs=out_specs + [tok_spec, tok_spec],
        out_shape=out_shape + [tok_sds, tok_sds],
        scratch_shapes=scratch,
        compiler_params=_params(("arbitrary",)),
        name="rw_prep_lat" if lat else "rw_prep_ctx",
    )(u_rw, u_rw, u_rw, mu, wg, ww, wa, w0, a0, k_k, k_a, r_k, ones, hsum)


def _allsum_sublanes(x):
    x = x + pltpu.roll(x, 4, 0)
    x = x + pltpu.roll(x, 2, 0)
    return x + pltpu.roll(x, 1, 0)


def _rw_scan_kernel(lat, nv, n_chunks, *refs):
    if lat:
        af_ref, ab_ref, sf_ref, sb_ref, kf_ref, kb_ref, vf_ref, vb_ref, s0_ref, yf_ref, yb_ref, s_scr, sa_scr = refs
    else:
        af_ref, ab_ref, sf_ref, sb_ref, kf_ref, kb_ref, yf_ref, yb_ref, sfin_ref, s_scr, sa_scr = refs
        vf_ref = vb_ref = None
    c = pl.program_id(0)
    chunk = af_ref.shape[1]
    n_part = SCAN_CHAINS // min(nv, SCAN_BLOCKS)
    bcast = lambda row: jnp.broadcast_to(row, (SUBLANES, LANES))

    def total(parts):
        while len(parts) > 1:
            parts = [parts[i] + parts[i + 1] for i in range(0, len(parts), 2)]
        return parts[0]

    @pl.when(c == 0)
    def _():
        if lat:
            s_scr[...] = s0_ref[...]
            for d, (s_ref, first) in enumerate(((sf_ref, 0), (sb_ref, chunk - 1))):
                parts = [None] * SCAN_CHAINS
                for k in range(RW_HD):
                    p = s_scr[d, 0, k] * bcast(s_ref[1, first, pl.ds(k, 1), :])
                    parts[k % SCAN_CHAINS] = p if k < SCAN_CHAINS else parts[k % SCAN_CHAINS] + p
                sa_scr[d, 0] = total(parts)
        else:
            s_scr[...] = jnp.zeros(s_scr.shape, F32)
            sa_scr[...] = jnp.zeros(sa_scr.shape, F32)

    def step(last, tt, carry):
        dirs = ((af_ref, sf_ref, vf_ref, yf_ref, tt, tt + 1, kf_ref),
                (ab_ref, sb_ref, vb_ref, yb_ref, chunk - 1 - tt, chunk - 2 - tt, kb_ref))
        for d, (a_ref, s_ref, v_ref, y_ref, tl, nxt, k_ref) in enumerate(dirs):
            key_row = lambda ref, j, k: bcast(ref[j, tl, pl.ds(k, 1), :])
            if last:
                next_key = lambda k: bcast(k_ref[0, 0, pl.ds(k, 1), :])
            else:
                next_key = lambda k: bcast(s_ref[1, nxt, pl.ds(k, 1), :])
            for first in range(0, nv, SCAN_BLOCKS):
                blocks = range(first, min(first + SCAN_BLOCKS, nv))
                sa = {vb: sa_scr[d, vb] for vb in blocks}
                if lat:
                    vals = {0: v_ref[tl]}
                else:
                    vals = {vb: s_ref[2, tl, vb * SUBLANES:(vb + 1) * SUBLANES, :] for vb in blocks}
                y = {vb: [None] * n_part for vb in blocks}
                san = {vb: [None] * n_part for vb in blocks}
                for k in range(RW_HD):
                    w, kd, bb = key_row(a_ref, 0, k), key_row(a_ref, 1, k), key_row(a_ref, 2, k)
                    r = key_row(s_ref, 0, k)
                    kkn = next_key(k)
                    for vb in blocks:
                        sn = s_scr[d, vb, k] * w - sa[vb] * bb + vals[vb] * kd
                        s_scr[d, vb, k] = sn
                        y[vb][k % n_part] = sn * r if k < n_part else y[vb][k % n_part] + sn * r
                        san[vb][k % n_part] = sn * kkn if k < n_part else san[vb][k % n_part] + sn * kkn
                for vb in blocks:
                    sa_scr[d, vb] = total(san[vb])
                    if lat:
                        y_ref[tl] = total(y[vb])
                    else:
                        y_ref[tl, vb * SUBLANES:(vb + 1) * SUBLANES, :] = total(y[vb])
        return carry

    lax.fori_loop(0, chunk - 1, functools.partial(step, False), 0)
    step(True, chunk - 1, 0)

    if not lat:
        @pl.when(c == n_chunks - 1)
        def _():
            sfin_ref[...] = s_scr[...]


def _rw_scan(lat, ops, vv, s0):
    seq_len = ops.shape[1]
    chunk = SCAN_CHUNK_LAT if lat else SCAN_CHUNK_CTX
    nc = seq_len // chunk
    nv = 1 if lat else RW_HD // SUBLANES
    n_shared = 2 if lat else 3
    s_shape = (2, nv, RW_HD, SUBLANES, LANES)
    state_spec = pl.BlockSpec(s_shape, lambda c: (0,) * 5)
    scratch = [pltpu.VMEM(s_shape, F32), pltpu.VMEM((2, nv, SUBLANES, LANES), F32)]
    op_spec = lambda n, blk, rev: pl.BlockSpec(
        (n, chunk, RW_HD, LANES), lambda c: (blk, (nc - 1 - c) if rev else c, 0, 0))
    last = seq_len - 1
    kk_next = lambda t_of: pl.BlockSpec((1, 1, RW_HD, LANES), lambda c: (RW_OPS.index("kk"), t_of(c), 0, 0))
    in_specs = [op_spec(3, 0, False), op_spec(3, 1, True),
                op_spec(n_shared, 6 // n_shared, False), op_spec(n_shared, 6 // n_shared, True),
                kk_next(lambda c: jnp.minimum((c + 1) * chunk, last)),
                kk_next(lambda c: jnp.maximum((nc - 1 - c) * chunk - 1, 0))]
    args = [ops] * 6
    if lat:
        y_spec = lambda rev: pl.BlockSpec((chunk, SUBLANES, LANES), lambda c: ((nc - 1 - c) if rev else c, 0, 0))
        y_sds = jax.ShapeDtypeStruct((seq_len, SUBLANES, LANES), F32)
        in_specs += [y_spec(False), y_spec(True), state_spec]
        args += [vv, vv, s0]
        out_specs = [y_spec(False), y_spec(True)]
        out_shape = [y_sds, y_sds]
    else:
        y_spec = lambda rev: pl.BlockSpec((chunk, RW_HD, LANES), lambda c: ((nc - 1 - c) if rev else c, 0, 0))
        y_sds = jax.ShapeDtypeStruct((seq_len, RW_HD, LANES), F32)
        out_specs = [y_spec(False), y_spec(True), state_spec]
        out_shape = [y_sds, y_sds, jax.ShapeDtypeStruct(s_shape, F32)]
    return pl.pallas_call(
        functools.partial(_rw_scan_kernel, lat, nv, nc),
        grid=(nc,),
        in_specs=in_specs,
        out_specs=out_specs,
        out_shape=out_shape,
        scratch_shapes=scratch,
        compiler_params=_params(("arbitrary",)),
        name="rw_scan_lat" if lat else "rw_scan_ctx",
    )(*args)


def _rw_post_ctx_kernel(nb, tt, yf_ref, yb_ref, g_ref, gb_ref, lnw_ref, lnb_ref, o_ref):
    for t in range(tt):
        y = yf_ref[t] + yb_ref[t]
        dev = y - jnp.mean(y, axis=0, keepdims=True)
        yn = (dev * lax.rsqrt(jnp.mean(dev * dev, axis=0, keepdims=True) + RW_GN_EPS)).T
        tok = jnp.concatenate([yn[h * nb:(h + 1) * nb, :] for h in range(RW_HEADS)], axis=1)
        o_ref[:, t, :] = (tok * lnw_ref[...] + lnb_ref[...]) * g_ref[:, t, :] + gb_ref[:, t, :]


def _rw_post_lat_kernel(nb, yf_ref, yb_ref, g_ref, gb_ref, lnw_ref, lnb_ref, hs_ref, o_ref):
    grp = LANES // nb
    ys = [yf_ref[:, vi, :] + yb_ref[:, vi, :] for vi in range(SUBLANES)]
    for b in range(nb):
        y = jnp.concatenate([yv[:, b * grp:(b + 1) * grp] for yv in ys], axis=1)
        dev = y - _sel_dot(y, hs_ref[...], EXACT_TERMS) * (1.0 / RW_HD)
        yn = dev * lax.rsqrt(_sel_dot(dev * dev, hs_ref[...], NORM_TERMS) * (1.0 / RW_HD) + RW_GN_EPS)
        o_ref[b] = (yn * lnw_ref[...] + lnb_ref[...]) * g_ref[b] + gb_ref[b]


def _rw_post(lat, yf, yb, g, gb, lnw, lnb, hs):
    nb, seq_len, _ = g.shape
    tt = LANES if lat else SUBLANES
    full = lambda a: pl.BlockSpec(a.shape, lambda i: (0,) * a.ndim)
    tok_spec = pl.BlockSpec((nb, tt, RW_W), lambda i: (0, i, 0))
    if lat:
        y_spec = pl.BlockSpec((tt, SUBLANES, LANES), lambda i: (i, 0, 0))
        body, extra = functools.partial(_rw_post_lat_kernel, nb), [hs]
    else:
        y_spec = pl.BlockSpec((tt, RW_HD, LANES), lambda i: (i, 0, 0))
        body, extra = functools.partial(_rw_post_ctx_kernel, nb, tt), []
    return pl.pallas_call(
        body,
        grid=(seq_len // tt,),
        in_specs=[y_spec, y_spec, tok_spec, tok_spec, full(lnw), full(lnb)] + [full(a) for a in extra],
        out_specs=tok_spec,
        out_shape=jax.ShapeDtypeStruct((nb, seq_len, RW_W), F32),
        compiler_params=_params(("parallel",)),
        name="rw_post_lat" if lat else "rw_post_ctx",
    )(yf, yb, g, gb, lnw, lnb, *extra)


def _rw_mixer(lat, u_rw, s0, prep_params, lnw, lnb, hs):
    nb = u_rw.shape[0]
    nvb = RW_HD // SUBLANES
    if lat:
        ops, vv, g, gb = _rw_prep(True, u_rw, *prep_params)
        grp = LANES // (nb * RW_HEADS)
        s0 = s0.reshape(nb, 2, RW_HEADS, grp, SUBLANES, RW_HD)
        s0 = jnp.transpose(s0, (1, 5, 4, 0, 2, 3)).reshape(2, 1, RW_HD, SUBLANES, LANES)
        yf, yb = _rw_scan(True, ops, vv, s0)
        return _rw_post(True, yf, yb, g, gb, lnw, lnb, hs), None
    ops, g, gb = _rw_prep(False, u_rw, *prep_params)
    yf, yb, sfin = _rw_scan(False, ops, None, None)
    sfin = sfin.reshape(2, nvb, RW_HD, SUBLANES, RW_HEADS, nb)
    sfin = jnp.transpose(sfin, (5, 0, 4, 1, 3, 2)).reshape(nb, 2, RW_HEADS, RW_HD, RW_HD)
    return _rw_post(False, yf, yb, g, gb, lnw, lnb, hs), sfin


def _softplus(x):
    return jnp.maximum(x, 0.0) + jnp.log1p(jnp.exp(-jnp.abs(x)))


def _ssd_kernel(reverse, n_blocks, n_sub, u_ref, up_ref, un_ref, h0_ref, cw_ref, cb_ref, dtb_ref, a_ref,
                tri_ref, exp_ref, *rest):
    if reverse:
        yf_ref, nw_ref, out_ref, hfin_ref, h_scr = rest
    else:
        dsk_ref, out_ref, hfin_ref, h_scr = rest
    s = pl.program_id(1)
    blk = (n_blocks - 1 - s) if reverse else s

    @pl.when(s == 0)
    def _():
        h_scr[...] = h0_ref[...]

    q = SSD_CHUNK
    rows = n_sub * q
    u = u_ref[...]
    xbc = u[:, SSD_W:SSD_W + 512]
    row = _iota(xbc.shape, 0)
    has_prev = (blk > 0).astype(F32)
    has_next = (blk < n_blocks - 1).astype(F32)
    prev = jnp.where(row == 0, up_ref[SUBLANES - 1:SUBLANES, SSD_W:SSD_W + 512] * has_prev, pltpu.roll(xbc, 1, 0))
    nxt = jnp.where(row == rows - 1, un_ref[0:1, SSD_W:SSD_W + 512] * has_next, pltpu.roll(xbc, rows - 1, 0))
    xc = _silu(cw_ref[0:1, :] * prev + cw_ref[1:2, :] * xbc + cw_ref[2:3, :] * nxt + cb_ref[...])
    dt_all = _softplus(u[:, SSD_W + 512:] + dtb_ref[...])
    dt_e_all = _sel_dot(dt_all, exp_ref[...], EXACT_TERMS)
    dta_all = dt_all * a_ref[...]

    ii = _iota((q, q), 0)
    jj = _iota((q, q), 1)
    tri = (jj >= ii) if reverse else (jj <= ii)
    lane = _iota((q, LANES), 1)
    rowc = _iota((LANES, 1), 0)
    state = [h_scr[g] for g in range(2)]
    order = range(n_sub - 1, -1, -1) if reverse else range(n_sub)
    for ci in order:
        rs = slice(ci * q, (ci + 1) * q)
        x = xc[rs, :SSD_W]
        b_all = xc[rs, SSD_W:SSD_W + LANES]
        c_all = xc[rs, SSD_W + LANES:]
        cs = _sel_dot(dta_all[rs], tri_ref[...], EXACT_TERMS, sel_first=True)
        cs_t = cs.T
        cs_e = _sel_dot(cs, exp_ref[...], EXACT_TERMS)
        tot = cs_e[0:1, :] if reverse else cs_e[q - 1:q, :]
        xdt = x * dt_e_all[rs]
        dec_in = jnp.exp(tot - cs_e)
        dec_out = jnp.exp(cs_e)
        cdec = jnp.exp(tot)
        ys = []
        for g in range(2):
            bg = b_all[:, g * 64:(g + 1) * 64]
            cg = c_all[:, g * 64:(g + 1) * 64]
            gm = _bdot_nt(cg, bg)
            sl = slice(g * LANES, (g + 1) * LANES)
            xdt_g = xdt[:, sl]
            yd = []
            for hh in range(2):
                h = 2 * g + hh
                diff = cs[:, h:h + 1] - cs_t[h:h + 1, :]
                lm = jnp.exp(jnp.where(tri, diff, -jnp.inf))
                yd.append(_bdot(gm * lm, xdt_g))
            hg = state[g]
            ys.append(jnp.where(lane < 64, yd[0], yd[1]) + _bdot_nt(cg, hg) * dec_out[:, sl])
            st = _bdot((xdt_g * dec_in[:, sl]).T, bg)
            cd = cdec[:, sl]
            state[g] = hg * jnp.where(rowc < 64, cd[:, 0:1], cd[:, 64:65]) + st
        y = jnp.concatenate(ys, axis=1)
        if reverse:
            val = (yf_ref[rs, :] + y) * _silu(u[rs, :SSD_W])
            out_ref[rs, :] = _rms(val, nw_ref[...])
        else:
            out_ref[rs, :] = y + dsk_ref[...] * x
    for g in range(2):
        h_scr[g] = state[g]
        hfin_ref[g] = state[g]


def _ssd_sweep(reverse, u_ssd, nb, seq_len, h0, cw, cb, dtb, a_neg, extra):
    n_sub = min(SSD_CHUNKS_PER_STEP, seq_len // SSD_CHUNK)
    rows = n_sub * SSD_CHUNK
    nc = seq_len // rows
    sub_per_chunk = rows // SUBLANES
    last_sub = u_ssd.shape[0] // SUBLANES - 1
    chunk = (lambda s: nc - 1 - s) if reverse else (lambda s: s)
    rowblk = lambda b, s: b * nc + chunk(s)
    full = lambda a: pl.BlockSpec(a.shape, lambda b, s: (0,) * a.ndim)
    steps = np.arange(SSD_CHUNK)
    tri = jnp.asarray((steps[None, :] >= steps[:, None]) if reverse else (steps[None, :] <= steps[:, None]), BF16)
    expand = jnp.asarray(np.arange(LANES)[:, None] == np.arange(SSD_W)[None, :] // 64, BF16)
    in_specs = [
        pl.BlockSpec((rows, SSD_PAD), lambda b, s: (rowblk(b, s), 0)),
        pl.BlockSpec((SUBLANES, SSD_PAD), lambda b, s: (jnp.maximum(rowblk(b, s) * sub_per_chunk - 1, 0), 0)),
        pl.BlockSpec((SUBLANES, SSD_PAD),
                     lambda b, s: (jnp.minimum((rowblk(b, s) + 1) * sub_per_chunk, last_sub), 0)),
        pl.BlockSpec((None, 2, LANES, 64), lambda b, s: (b, 0, 0, 0)),
        full(cw), full(cb), full(dtb), full(a_neg), full(tri), full(expand),
    ]
    args = [u_ssd, u_ssd, u_ssd, h0, cw, cb, dtb, a_neg, tri, expand]
    if reverse:
        yf, nw = extra
        in_specs += [pl.BlockSpec((rows, SSD_W), lambda b, s: (rowblk(b, s), 0)), full(nw)]
        args += [yf, nw]
    else:
        in_specs += [full(extra)]
        args += [extra]
    return pl.pallas_call(
        functools.partial(_ssd_kernel, reverse, nc, n_sub),
        grid=(nb, nc),
        in_specs=in_specs,
        out_specs=[pl.BlockSpec((rows, SSD_W), lambda b, s: (rowblk(b, s), 0)),
                   pl.BlockSpec((None, 2, LANES, 64), lambda b, s: (b, 0, 0, 0))],
        out_shape=[jax.ShapeDtypeStruct((nb * seq_len, SSD_W), F32),
                   jax.ShapeDtypeStruct((nb, 2, LANES, 64), F32)],
        scratch_shapes=[pltpu.VMEM((2, LANES, 64), F32)],
        compiler_params=_params(("parallel", "arbitrary")),
        name="ssd_bwd" if reverse else "ssd_fwd",
    )(*args)


def _ssd_mixer(u_ssd, nb, seq_len, h0, cw, cb, dtb, a_neg, dsk, nw):
    yf, hf = _ssd_sweep(False, u_ssd, nb, seq_len, h0[:, 0], cw, cb, dtb[0], a_neg[0], dsk)
    out, hb = _ssd_sweep(True, u_ssd, nb, seq_len, h0[:, 1], cw, cb, dtb[1], a_neg[1], (yf, nw))
    return out, jnp.stack([hf, hb], axis=1).reshape(nb, 2, 4, 64, 64)


def _route(scores, bias):
    n_grp = N_EXPERTS // EXP_PER_GROUP
    sel = scores + bias
    sc = [scores[e:e + 1, :] for e in range(N_EXPERTS)]
    sl = [sel[e:e + 1, :] for e in range(N_EXPERTS)]
    as_int = lambda cond: jnp.where(cond, 1, 0)
    top2, grp = [], []
    for g in range(n_grp):
        members = range(g * EXP_PER_GROUP, (g + 1) * EXP_PER_GROUP)
        total = None
        for e in members:
            ahead = [as_int(sl[o] >= sl[e]) if o < e else as_int(sl[o] > sl[e]) for o in members if o != e]
            top2.append(ahead[0] + ahead[1] + ahead[2] < 2)
            kept = jnp.where(top2[e], sl[e], 0.0)
            total = kept if total is None else total + kept
        grp.append(total)
    rows = []
    for g in range(n_grp):
        beaten = [as_int(grp[o] >= grp[g]) if o < g else as_int(grp[o] > grp[g]) for o in range(n_grp) if o != g]
        best = beaten[0] + beaten[1] + beaten[2] == 0
        members = range(g * EXP_PER_GROUP, (g + 1) * EXP_PER_GROUP)
        chosen = [jnp.where(best, as_int(top2[e]), 0) > 0 for e in members]
        cw = [jnp.where(c, sc[e], 0.0) for c, e in zip(chosen, members)]
        den = cw[0] + cw[1] + cw[2] + cw[3]
        rows += [jnp.where(c, w / jnp.where(c, den, 1.0), 0.0) for c, w in zip(chosen, cw)]
    return jnp.concatenate(rows, axis=0)


def _outproj_kernel(oatt_ref, orw_ref, ossd_ref, x_ref, mod_ref, wout_ref,
                    n2_ref, rw_ref, rb_ref, x1_ref, h2_ref, comb_ref):
    o = jnp.concatenate([oatt_ref[...], orw_ref[...], ossd_ref[...]], axis=1)
    x1 = x_ref[...] + mod_ref[2] * _bdot(o, wout_ref[...])
    h2 = _rms(x1, n2_ref[...]) * (1.0 + mod_ref[4]) + mod_ref[3]
    comb = _route(_sigmoid(_bdot_nt(rw_ref[...], h2)), rb_ref[...])
    x1_ref[...] = x1
    h2_ref[...] = h2.astype(BF16)
    comb_ref[...] = jnp.concatenate([comb, jnp.zeros((LANES - N_EXPERTS, comb.shape[1]), F32)], axis=0).T


def _outproj(o_att, o_rw, o_ssd, x, mod_l, row_fn, w_out_bf, n2, router_w, router_b):
    n_tok = x.shape[0]
    tile = lambda w: pl.BlockSpec((TOKEN_TILE, w), lambda i: (i, 0))
    full = lambda a: pl.BlockSpec(a.shape, lambda i: (0,) * a.ndim)
    return pl.pallas_call(
        _outproj_kernel,
        grid=(n_tok // TOKEN_TILE,),
        in_specs=[tile(ATT_W), tile(RW_W), tile(SSD_W), tile(D_MODEL),
                  _mod_spec(row_fn)] + [full(a) for a in (w_out_bf, n2, router_w, router_b)],
        out_specs=[tile(D_MODEL), tile(D_MODEL), tile(LANES)],
        out_shape=[jax.ShapeDtypeStruct((n_tok, D_MODEL), F32),
                   jax.ShapeDtypeStruct((n_tok, D_MODEL), BF16),
                   jax.ShapeDtypeStruct((n_tok, LANES), F32)],
        compiler_params=_params(("parallel",)),
        name="outproj",
    )(o_att, o_rw, o_ssd, x, mod_l, w_out_bf, n2, router_w, router_b)


def _moe_kernel(h2_ref, comb_ref, x1_ref, mod_ref, wg_ref, wu_ref, wd_ref, o_ref, acc_ref):
    s = pl.program_id(1)

    @pl.when(s == 0)
    def _():
        acc_ref[...] = jnp.zeros_like(acc_ref)

    h2 = h2_ref[...]
    comb = comb_ref[...]
    lane = _iota(comb.shape, 1)
    weighted = []
    for j in range(MOE_EXPERTS_PER_STEP):
        hid = _silu(jnp.dot(h2, wg_ref[j], preferred_element_type=F32)) * jnp.dot(
            h2, wu_ref[j], preferred_element_type=F32)
        wcol = jnp.sum(jnp.where(lane == s * MOE_EXPERTS_PER_STEP + j, comb, 0.0), axis=1, keepdims=True)
        weighted.append((hid * wcol).astype(BF16))
    acc_ref[...] += jnp.dot(jnp.concatenate(weighted, axis=1), wd_ref[...].reshape(-1, D_MODEL),
                            preferred_element_type=F32)

    @pl.when(s == N_EXPERTS // MOE_EXPERTS_PER_STEP - 1)
    def _():
        o_ref[...] = x1_ref[...] + mod_ref[5] * acc_ref[...]


def _moe(h2, comb, x1, mod_l, row_fn, wg_bf, wu_bf, wd_bf):
    n_tok = x1.shape[0]
    tile = lambda w: pl.BlockSpec((MOE_TILE, w), lambda i, e: (i, 0))
    return pl.pallas_call(
        _moe_kernel,
        grid=(n_tok // MOE_TILE, N_EXPERTS // MOE_EXPERTS_PER_STEP),
        in_specs=[tile(D_MODEL), tile(LANES), tile(D_MODEL), _mod_spec(row_fn),
                  pl.BlockSpec((MOE_EXPERTS_PER_STEP, D_MODEL, D_FF), lambda i, e: (e, 0, 0)),
                  pl.BlockSpec((MOE_EXPERTS_PER_STEP, D_MODEL, D_FF), lambda i, e: (e, 0, 0)),
                  pl.BlockSpec((MOE_EXPERTS_PER_STEP, D_FF, D_MODEL), lambda i, e: (e, 0, 0))],
        out_specs=tile(D_MODEL),
        out_shape=jax.ShapeDtypeStruct((n_tok, D_MODEL), F32),
        scratch_shapes=[pltpu.VMEM((MOE_TILE, D_MODEL), F32)],
        compiler_params=_params(("parallel", "arbitrary")),
        name="moe",
    )(h2, comb, x1, mod_l, wg_bf, wu_bf, wd_bf)


def _rope_tables(seq_len):
    t = np.arange(seq_len)
    pos = np.stack([t // GRID_W, t % GRID_W], 0).astype(np.float32)
    n_freq = HEAD_DIM // 4
    inv = jnp.asarray(ROPE_BASE, F32) ** (-jnp.arange(n_freq, dtype=F32) / n_freq)
    ang = jnp.asarray(pos)[:, :, None] * inv
    cos, sin = jnp.cos(ang), jnp.sin(ang)
    cos_h = jnp.concatenate([cos[0], cos[0], cos[1], cos[1]], -1)
    sin_h = jnp.concatenate([-sin[0], sin[0], -sin[1], sin[1]], -1)
    return jnp.tile(cos_h, (1, 2)), jnp.tile(sin_h, (1, 2))


def _pad_rows(w, start, total):
    return jnp.pad(w, ((0, 0),) * (w.ndim - 2) + ((start, total - start - w.shape[-2]), (0, 0)))


def _value_order(x, axis, start, grp):
    if grp is None:
        return x
    axis %= x.ndim
    sl = lambda a, b: lax.slice_in_dim(x, a, b, axis=axis)
    mid = sl(start, start + RW_W)
    shape = mid.shape
    mid = mid.reshape(shape[:axis] + (RW_HEADS, grp, RW_HD // grp) + shape[axis + 1:])
    order = tuple(range(axis)) + (axis + 2, axis, axis + 1) + tuple(range(axis + 3, mid.ndim))
    mid = jnp.transpose(mid, order).reshape(shape)
    return jnp.concatenate([sl(0, start), mid, sl(start + RW_W, x.shape[axis])], axis=axis)


def _rw_layer_params(l, P, grp):
    row = lambda a: a.reshape(1, -1)
    v0 = 2 * RW_W
    w_in = jnp.pad(P["w_in"][l], ((0, 0), (0, SSD_PAD - SSD_COLS))).astype(BF16)
    head = np.arange(RW_W) // RW_HD if grp is None else (np.arange(RW_W) // grp) % RW_HEADS
    rp = {}
    rp["w_in"] = _value_order(w_in, 1, ATT_COLS + v0, grp)
    rp["w_out"] = _value_order(P["w_out"][l].astype(BF16), 0, ATT_W, grp)
    rp["prep"] = (
        _value_order(P["rw_mu"][l], 1, v0, grp),
        _pad_rows(_value_order(P["rw_g_up"][l], 1, 0, grp), 64, LANES),
        _pad_rows(P["rw_w_up"][l], 0, LANES),
        _pad_rows(P["rw_a_up"][l], 32, LANES),
        P["rw_w0"][l], P["rw_a0"][l],
        row(P["rw_k_k"][l]), row(P["rw_k_a"][l]), row(P["rw_r_k"][l]),
        _head_selector(np.arange(RW_W) // RW_HD), _head_selector(np.arange(RW_W) // RW_HD, head),
    )
    rp["lnw"] = _value_order(row(P["rw_ln_w"][l]), 1, 0, grp)
    rp["lnb"] = _value_order(row(P["rw_ln_b"][l]), 1, 0, grp)
    rp["hs"] = _head_selector(head)
    return rp


def _layer_params(l, P):
    row = lambda a: a.reshape(1, -1)
    lp = {}
    lp["n1"] = row(P["norm1_w"][l])
    lp["n2"] = row(P["norm2_w"][l])
    lp["qw"] = row(jnp.tile(P["q_norm_w"][l], ATT_HEADS))
    lp["kw"] = row(jnp.tile(P["k_norm_w"][l], ATT_KV_HEADS))
    lp["sink"] = P["attn_sink"][l]
    pad_heads = lambda a: jnp.pad(a, ((0, 0), (0, LANES - a.shape[1]))).reshape(2, 1, LANES)
    lp["ssd"] = (
        P["ssd_conv_w"][l], row(P["ssd_conv_b"][l]),
        pad_heads(P["ssd_dt_bias"][l]), pad_heads(-jnp.exp(P["ssd_a_log"][l])),
        row(jnp.repeat(P["ssd_d"][l], 64)), row(P["ssd_norm_w"][l]),
    )
    lp["wg"] = P["exp_gate"][l].astype(BF16)
    lp["wu"] = P["exp_up"][l].astype(BF16)
    lp["wd"] = P["exp_down"][l].astype(BF16)
    return lp


def _trunk(x, nb, seq_len, is_ctx, mod, layers, rw_layers, router, cache_k, cache_v, state_rwkv, state_ssd):
    tiles_per_seq = seq_len // TOKEN_TILE
    moe_per_seq = max(seq_len // MOE_TILE, 1)
    if is_ctx:
        row_tok = lambda i: 0
        row_moe = lambda i: 0
        rope = None
    else:
        row_tok = lambda i: 1 + i // tiles_per_seq
        row_moe = lambda i: 1 + i // moe_per_seq
        rope = _rope_tables(seq_len)
    new_k, new_v, new_rw, new_ssd = [], [], [], []
    for l, (lp, rp) in enumerate(zip(layers, rw_layers)):
        q, k, v, u_rw, u_ssd = _inproj(x, mod[l], row_tok, lp["n1"], rp["w_in"], lp["qw"], lp["kw"], rope, seq_len)
        if is_ctx:
            o_att = _attn_ctx(lp["sink"], q, k, v, nb, seq_len)
            rw0 = None
            ssd0 = jnp.zeros((nb, 2, 2, LANES, 64), F32)
        else:
            kc = cache_k[:, l].reshape(nb, -1, KV_W)
            vc = cache_v[:, l].reshape(nb, -1, KV_W)
            o_att = _attn_lat(lp["sink"], q, k, v, kc, vc, nb, seq_len)
            rw0 = state_rwkv[:, l]
            ssd0 = state_ssd[:, l].reshape(nb, 2, 2, LANES, 64)
        o_rw, s_rw = _rw_mixer(not is_ctx, u_rw.reshape(nb, seq_len, RW_COLS), rw0, rp["prep"],
                               rp["lnw"], rp["lnb"], rp["hs"])
        o_ssd, s_ssd = _ssd_mixer(u_ssd, nb, seq_len, ssd0, *lp["ssd"])
        if is_ctx:
            new_k.append(k.reshape(nb, seq_len, ATT_KV_HEADS, HEAD_DIM))
            new_v.append(v.reshape(nb, seq_len, ATT_KV_HEADS, HEAD_DIM))
            new_rw.append(s_rw)
            new_ssd.append(s_ssd)
        x1, h2, comb = _outproj(o_att, o_rw.reshape(nb * seq_len, RW_W), o_ssd, x, mod[l], row_tok, rp["w_out"],
                                lp["n2"], *router)
        x = _moe(h2, comb, x1, mod[l], row_moe, lp["wg"], lp["wu"], lp["wd"])
    if is_ctx:
        return x, tuple(jnp.stack(t, 1) for t in (new_k, new_v, new_rw, new_ssd))
    return x, None


def kernel(x_prompt, x_sample, cache_k, cache_v, state_rwkv, state_ssd, c, c_ctx, w_mod, b_mod, norm1_w, norm2_w, w_in, w_out, q_norm_w, k_norm_w, attn_sink, rw_mu, rw_w0, rw_w_up, rw_a0, rw_a_up, rw_g_up, rw_k_k, rw_k_a, rw_r_k, rw_ln_w, rw_ln_b, ssd_conv_w, ssd_conv_b, ssd_dt_bias, ssd_a_log, ssd_d, ssd_norm_w, router_w, router_bias, exp_gate, exp_up, exp_down):
    P = dict(norm1_w=norm1_w, norm2_w=norm2_w, w_in=w_in, w_out=w_out, q_norm_w=q_norm_w, k_norm_w=k_norm_w,
             attn_sink=attn_sink, rw_mu=rw_mu, rw_w0=rw_w0, rw_w_up=rw_w_up, rw_a0=rw_a0, rw_a_up=rw_a_up,
             rw_g_up=rw_g_up, rw_k_k=rw_k_k, rw_k_a=rw_k_a, rw_r_k=rw_r_k, rw_ln_w=rw_ln_w, rw_ln_b=rw_ln_b,
             ssd_conv_w=ssd_conv_w, ssd_conv_b=ssd_conv_b, ssd_dt_bias=ssd_dt_bias, ssd_a_log=ssd_a_log,
             ssd_d=ssd_d, ssd_norm_w=ssd_norm_w, exp_gate=exp_gate, exp_up=exp_up, exp_down=exp_down)
    nb_ctx, seq_ctx, _ = x_prompt.shape
    nb_lat, seq_lat, _ = x_sample.shape
    assert nb_lat + 1 <= SUBLANES and (nb_ctx * seq_ctx) % MOE_TILE == 0 and seq_lat % MOE_TILE == 0
    assert nb_ctx * RW_HEADS == LANES and LANES % (nb_lat * RW_HEADS) == 0
    cond8 = jnp.zeros((SUBLANES, D_MODEL), F32).at[0].set(c_ctx).at[1:1 + nb_lat].set(c)
    mod = _modulation(cond8, w_mod, b_mod)
    layers = [_layer_params(l, P) for l in range(DEPTH)]
    rw_ctx = [_rw_layer_params(l, P, None) for l in range(DEPTH)]
    rw_lat = [_rw_layer_params(l, P, LANES // (nb_lat * RW_HEADS)) for l in range(DEPTH)]
    router = (router_w.T.astype(BF16), router_bias.reshape(N_EXPERTS, 1))
    y_prompt, ctx_state = _trunk(x_prompt.reshape(-1, D_MODEL), nb_ctx, seq_ctx, True, mod, layers, rw_ctx, router,
                                 None, None, None, None)
    y_sample, _ = _trunk(x_sample.reshape(-1, D_MODEL), nb_lat, seq_lat, False, mod, layers, rw_lat, router,
                         cache_k, cache_v, state_rwkv, state_ssd)
    return (y_prompt.reshape(x_prompt.shape), y_sample.reshape(x_sample.shape)) + ctx_state
```

```python
import functools

import jax
import jax.numpy as jnp
import numpy as np
from jax import lax
from jax.experimental import pallas as pl
from jax.experimental.pallas import tpu as pltpu

F32 = jnp.float32
BF16 = jnp.bfloat16

D_MODEL = 1024
DEPTH = 2
GRID_W = 64
EPS = 1e-6
ATT_HEADS = 8
ATT_KV_HEADS = 2
HEAD_DIM = 64
ATT_W = 512
KV_W = 128
ATT_COLS = 768
WINDOW = 128
BLK = 128
ATT_QBLK = 256
ROPE_BASE = 10000.0
RW_HEADS = 4
RW_HD = 64
RW_W = 256
RW_COLS = 896
RW_DECAY_SCALE = 0.606531
RW_GN_EPS = 64e-5
SSD_W = 256
SSD_CHUNK = 128
SSD_CHUNKS_PER_STEP = 2
SSD_COLS = 772
SSD_PAD = 896
IN_PAD = ATT_COLS + RW_COLS + SSD_PAD
N_EXPERTS = 16
EXP_PER_GROUP = 4
D_FF = 256

LANES = 128
SUBLANES = 8
TOKEN_TILE = 512
MOE_TILE = 1024
MOE_EXPERTS_PER_STEP = 4
SCAN_CHUNK_CTX = 16
SCAN_CHUNK_LAT = 16
SCAN_BLOCKS = 4
SCAN_CHAINS = 8
VMEM_LIMIT = 48 * 1024 * 1024


def _params(sem):
    return pltpu.CompilerParams(dimension_semantics=sem, vmem_limit_bytes=VMEM_LIMIT)


def _bdot(a, b):
    return jnp.dot(a.astype(BF16), b.astype(BF16), preferred_element_type=F32)


def _bdot_nt(a, b):
    return lax.dot_general(a.astype(BF16), b.astype(BF16), (((1,), (1,)), ((), ())),
                           preferred_element_type=F32)


def _sel_dot(x, sel, terms, sel_first=False):
    acc = None
    rem = x
    for i in range(terms):
        piece = rem.astype(BF16)
        part = jnp.dot(sel, piece, preferred_element_type=F32) if sel_first else jnp.dot(
            piece, sel, preferred_element_type=F32)
        acc = part if acc is None else acc + part
        if i + 1 < terms:
            rem = rem - piece.astype(F32)
    return acc


def _iota(shape, axis):
    return lax.broadcasted_iota(jnp.int32, shape, axis)


def _head_selector(heads_in, heads_out=None):
    heads_out = heads_in if heads_out is None else heads_out
    return jnp.asarray(np.asarray(heads_in)[:, None] == np.asarray(heads_out)[None, :], BF16)


NORM_TERMS = 2
EXACT_TERMS = 3


def _sigmoid(x):
    return 1.0 / (1.0 + jnp.exp(-x))


def _silu(x):
    return x * _sigmoid(x)


def _rms(x, w):
    return x * lax.rsqrt(jnp.mean(x * x, -1, keepdims=True) + EPS) * w


def _mod_kernel(c_ref, w_ref, b_ref, o_ref):
    o_ref[...] = _bdot(_silu(c_ref[...]), w_ref[...]) + b_ref[...]


def _modulation(cond8, w_mod, b_mod):
    out = pl.pallas_call(
        _mod_kernel,
        grid=(DEPTH, 6),
        in_specs=[
            pl.BlockSpec((SUBLANES, D_MODEL), lambda l, j: (0, 0)),
            pl.BlockSpec((None, D_MODEL, D_MODEL), lambda l, j: (l, 0, j)),
            pl.BlockSpec((None, 1, D_MODEL), lambda l, j: (l, 0, j)),
        ],
        out_specs=pl.BlockSpec((None, None, SUBLANES, D_MODEL), lambda l, j: (l, j, 0, 0)),
        out_shape=jax.ShapeDtypeStruct((DEPTH, 6, SUBLANES, D_MODEL), F32),
        compiler_params=_params(("arbitrary", "arbitrary")),
        name="modulation",
    )(cond8, w_mod, b_mod.reshape(DEPTH, 1, 6 * D_MODEL))
    return out.reshape(DEPTH, 6, SUBLANES, 1, D_MODEL)


def _mod_spec(row_fn):
    return pl.BlockSpec((6, None, 1, D_MODEL), lambda *idx: (0, row_fn(idx[0]), 0, 0))


def _swap16(x):
    w = x.shape[1]
    first = (_iota(x.shape, 1) & 31) < 16
    return jnp.where(first, pltpu.roll(x, w - 16, 1), pltpu.roll(x, 16, 1))


def _inproj_kernel(use_rope, x_ref, mod_ref, n1_ref, w_ref, qw_ref, kw_ref, hq_ref, *rest):
    if use_rope:
        cos_ref, sin_ref, q_ref, k_ref, v_ref, urw_ref, ussd_ref = rest
    else:
        q_ref, k_ref, v_ref, urw_ref, ussd_ref = rest
    h = _rms(x_ref[...], n1_ref[...]) * (1.0 + mod_ref[1]) + mod_ref[0]
    u = _bdot(h, w_ref[...])
    q = u[:, :ATT_W]
    k = u[:, ATT_W:ATT_W + KV_W]
    q = q * lax.rsqrt(_sel_dot(q * q, hq_ref[...], NORM_TERMS) * (1.0 / HEAD_DIM) + EPS) * qw_ref[...]
    k = k * lax.rsqrt(_sel_dot(k * k, hq_ref[:KV_W, :KV_W], NORM_TERMS) * (1.0 / HEAD_DIM) + EPS) * kw_ref[...]
    if use_rope:
        cos = cos_ref[...]
        sin = sin_ref[...]
        k = k * cos + _swap16(k) * sin
        cos4 = jnp.concatenate([cos] * 4, axis=1)
        sin4 = jnp.concatenate([sin] * 4, axis=1)
        q = q * cos4 + _swap16(q) * sin4
    q_ref[...] = q
    k_ref[...] = k
    v_ref[...] = u[:, ATT_W + KV_W:ATT_COLS]
    urw_ref[...] = u[:, ATT_COLS:ATT_COLS + RW_COLS]
    ussd_ref[...] = u[:, ATT_COLS + RW_COLS:]


def _inproj(x, mod_l, row_fn, n1, w_in_bf, qw, kw, rope, seq_len):
    n_tok = x.shape[0]
    tiles_per_seq = seq_len // TOKEN_TILE
    in_specs = [
        pl.BlockSpec((TOKEN_TILE, D_MODEL), lambda i: (i, 0)),
        _mod_spec(row_fn),
        pl.BlockSpec((1, D_MODEL), lambda i: (0, 0)),
        pl.BlockSpec((D_MODEL, IN_PAD), lambda i: (0, 0)),
        pl.BlockSpec((1, ATT_W), lambda i: (0, 0)),
        pl.BlockSpec((1, KV_W), lambda i: (0, 0)),
        pl.BlockSpec((ATT_W, ATT_W), lambda i: (0, 0)),
    ]
    args = [x, mod_l, n1, w_in_bf, qw, kw, _head_selector(np.arange(ATT_W) // HEAD_DIM)]
    if rope is not None:
        in_specs += [pl.BlockSpec((TOKEN_TILE, LANES), lambda i: (i % tiles_per_seq, 0))] * 2
        args += list(rope)
    widths = (ATT_W, KV_W, KV_W, RW_COLS, SSD_PAD)
    return pl.pallas_call(
        functools.partial(_inproj_kernel, rope is not None),
        grid=(n_tok // TOKEN_TILE,),
        in_specs=in_specs,
        out_specs=[pl.BlockSpec((TOKEN_TILE, w), lambda i: (i, 0)) for w in widths],
        out_shape=[jax.ShapeDtypeStruct((n_tok, w), F32) for w in widths],
        compiler_params=_params(("parallel",)),
        name="inproj",
    )(*args)


LOG2E = 1.4426950408889634


def _attend_heads(sink_ref, q, parts):
    groups = ATT_HEADS // ATT_KV_HEADS
    outs = []
    for hk in range(ATT_KV_HEADS):
        cols = slice(hk * HEAD_DIM, (hk + 1) * HEAD_DIM)
        keys = [k[:, cols].astype(BF16) for k, _, _ in parts]
        vals = [v[:, cols].astype(BF16) for _, v, _ in parts]
        for g in range(groups):
            hd = hk * groups + g
            qh = q[:, hd * HEAD_DIM:(hd + 1) * HEAD_DIM].astype(BF16)
            sink = sink_ref[hd] * LOG2E
            scores = []
            for kh, (_, _, mask) in zip(keys, parts):
                s = lax.dot_general(qh, kh, (((1,), (1,)), ((), ())),
                                    preferred_element_type=F32) * (HEAD_DIM ** -0.5 * LOG2E)
                scores.append(s if mask is None else jnp.where(mask, s, -1e30))
            m = jnp.max(scores[0], -1, keepdims=True)
            for s in scores[1:]:
                m = jnp.maximum(m, jnp.max(s, -1, keepdims=True))
            m = jnp.maximum(m, sink)
            den = jnp.exp2(sink - m)
            o = None
            for s, vh in zip(scores, vals):
                p = jnp.exp2(s - m)
                den = den + jnp.sum(p, -1, keepdims=True)
                pv = jnp.dot(p.astype(BF16), vh, preferred_element_type=F32)
                o = pv if o is None else o + pv
            outs.append(o / den)
    return jnp.concatenate(outs, axis=1)


def _attn_ctx_kernel(sink_ref, q_ref, k_ref, v_ref, o_ref):
    o_ref[...] = _attend_heads(sink_ref, q_ref[...], [(k_ref[...], v_ref[...], None)])


def _attn_ctx(sink, q, k, v, nb, seq_len):
    return pl.pallas_call(
        _attn_ctx_kernel,
        grid=(nb,),
        in_specs=[
            pl.BlockSpec(memory_space=pltpu.SMEM),
            pl.BlockSpec((seq_len, ATT_W), lambda b: (b, 0)),
            pl.BlockSpec((seq_len, KV_W), lambda b: (b, 0)),
            pl.BlockSpec((seq_len, KV_W), lambda b: (b, 0)),
        ],
        out_specs=pl.BlockSpec((seq_len, ATT_W), lambda b: (b, 0)),
        out_shape=jax.ShapeDtypeStruct((nb * seq_len, ATT_W), F32),
        compiler_params=_params(("parallel",)),
        name="attn_ctx",
    )(sink, q, k, v)


def _attn_lat_kernel(seq_len, sink_ref, q_ref, k_ref, v_ref, kc_ref, vc_ref, o_ref):
    i = pl.program_id(1)
    band = ATT_QBLK + 2 * BLK
    start = pl.multiple_of(jnp.clip(i * ATT_QBLK - BLK, 0, seq_len - band), BLK)
    dist = jnp.abs(i * ATT_QBLK + _iota((ATT_QBLK, band), 0) - (start + _iota((ATT_QBLK, band), 1)))
    parts = [(k_ref[pl.ds(start, band), :], v_ref[pl.ds(start, band), :], dist <= WINDOW),
             (kc_ref[...], vc_ref[...], None)]
    o_ref[...] = _attend_heads(sink_ref, q_ref[...], parts)


def _attn_lat(sink, q, k, v, kc, vc, nb, seq_len):
    nblk = seq_len // ATT_QBLK
    past = kc.shape[1]
    return pl.pallas_call(
        functools.partial(_attn_lat_kernel, seq_len),
        grid=(nb, nblk),
        in_specs=[
            pl.BlockSpec(memory_space=pltpu.SMEM),
            pl.BlockSpec((ATT_QBLK, ATT_W), lambda b, i: (b * nblk + i, 0)),
            pl.BlockSpec((seq_len, KV_W), lambda b, i: (b, 0)),
            pl.BlockSpec((seq_len, KV_W), lambda b, i: (b, 0)),
            pl.BlockSpec((None, past, KV_W), lambda b, i: (b, 0, 0)),
            pl.BlockSpec((None, past, KV_W), lambda b, i: (b, 0, 0)),
        ],
        out_specs=pl.BlockSpec((ATT_QBLK, ATT_W), lambda b, i: (b * nblk + i, 0)),
        out_shape=jax.ShapeDtypeStruct((nb * seq_len, ATT_W), F32),
        compiler_params=_params(("parallel", "arbitrary")),
        name="attn_lat",
    )(sink, q, k, v, kc, vc)


RW_OPS = ("w_f", "kd_f", "b_f", "w_b", "kd_b", "b_b", "r", "kk")


def _rw_prep_kernel(lat, nb, tt, n_tiles, u_ref, up_ref, un_ref, mu_ref, wg_ref, ww_ref, wa_ref,
                    w0_ref, a0_ref, kk_ref, ka_ref, rk_ref, ones_ref, hsum_ref, *rest):
    if lat:
        ops_ref, vv_ref, g_ref, gb_ref, xs = rest
    else:
        ops_ref, g_ref, gb_ref, xs, xv = rest
    i = pl.program_id(0)
    has_prev = (i > 0).astype(F32)
    has_next = (i < n_tiles - 1).astype(F32)
    row = _iota((tt, RW_COLS), 0)
    mu0 = mu_ref[0:1, :]
    mu1 = mu_ref[1:2, :]
    mixed = []
    for b in range(nb):
        ub = u_ref[b]
        prev = jnp.where(row == 0, up_ref[b, SUBLANES - 1:SUBLANES, :] * has_prev, pltpu.roll(ub, 1, 0))
        nxt = jnp.where(row == tt - 1, un_ref[b, 0:1, :] * has_next, pltpu.roll(ub, tt - 1, 0))
        mixed.append(ub + mu0 * (prev - ub) + mu1 * (nxt - ub))
    u = jnp.concatenate(mixed, axis=0)
    r = u[:, :RW_W]
    k = u[:, RW_W:2 * RW_W]
    v = u[:, 2 * RW_W:3 * RW_W]
    low = u[:, 3 * RW_W:]
    kk = k * kk_ref[...]
    kk = kk * lax.rsqrt(_sel_dot(kk * kk, ones_ref[...], NORM_TERMS) + 1e-12)
    tanh_low = jnp.tanh(low)
    vals = [None] * len(RW_OPS)
    for d in range(2):
        w = jnp.exp(-RW_DECAY_SCALE * _sigmoid(w0_ref[d:d + 1, :] + _bdot(tanh_low, ww_ref[d])))
        a = _sigmoid(a0_ref[d:d + 1, :] + _bdot(low, wa_ref[d]))
        vals[3 * d:3 * d + 3] = [w, k * (1.0 + (a - 1.0) * ka_ref[...]), kk * a]
    vals[6:8] = [r, kk]
    g = _bdot(_sigmoid(low), wg_ref[...])
    gb = _sel_dot(r * k * rk_ref[...], hsum_ref[...], EXACT_TERMS) * v * g
    for b in range(nb):
        g_ref[b] = g[b * tt:(b + 1) * tt]
        gb_ref[b] = gb[b * tt:(b + 1) * tt]

    n_pairs = len(RW_OPS) // 2
    low_half = _iota((nb * tt, LANES), 1) < RW_HD
    for p in range(n_pairs):
        for hp in range(2):
            first = vals[2 * p][:, hp * LANES:(hp + 1) * LANES]
            second = vals[2 * p + 1][:, hp * LANES:(hp + 1) * LANES]
            xs[p, 2 * hp] = jnp.where(low_half, first, pltpu.roll(second, RW_HD, 1))
            xs[p, 2 * hp + 1] = jnp.where(low_half, pltpu.roll(first, RW_HD, 1), second)

    def emit(p, t, tiles):
        both = jnp.concatenate(tiles, axis=0).T
        ops_ref[2 * p, t] = both[:RW_HD]
        ops_ref[2 * p + 1, t] = both[RW_HD:]

    if lat:
        def relayout(t, carry):
            for p in range(n_pairs):
                emit(p, t, [jnp.broadcast_to(xs[p, h, pl.ds(b * tt + t, 1), :], (SUBLANES, LANES))
                            for b in range(nb) for h in range(RW_HEADS)])
            return carry

        lax.fori_loop(0, tt, relayout, 0)
        grp = LANES // nb
        for vi in range(SUBLANES):
            vv_ref[:, vi, :] = jnp.concatenate(
                [v[b * tt:(b + 1) * tt, vi * grp:(vi + 1) * grp] for b in range(nb)], axis=1)
    else:
        xv[0] = v[:, :LANES]
        xv[1] = v[:, LANES:]
        for t in range(tt):
            at_t = pl.ds(t, nb, stride=tt)
            for p in range(n_pairs):
                emit(p, t, [xs[p, h, at_t, :] for h in range(RW_HEADS)])
            halves = [xv[hp, at_t, :] for hp in range(2)]
            ops_ref[2 * n_pairs, t] = jnp.concatenate(
                [halves[hp][:, hh * RW_HD:(hh + 1) * RW_HD] for hp in range(2) for hh in range(2)], axis=0).T


def _rw_prep(lat, u_rw, mu, wg, ww, wa, w0, a0, k_k, k_a, r_k, ones, hsum):
    nb, seq_len, _ = u_rw.shape
    tt = (LANES // nb) if lat else SUBLANES
    n_tiles = seq_len // tt
    n_ops = len(RW_OPS) + (0 if lat else 1)
    sub = tt // SUBLANES
    last = seq_len // SUBLANES - 1
    full = lambda a: pl.BlockSpec(a.shape, lambda i: (0,) * a.ndim)
    tok_spec = pl.BlockSpec((nb, tt, RW_W), lambda i: (0, i, 0))
    tok_sds = jax.ShapeDtypeStruct((nb, seq_len, RW_W), F32)
    out_specs = [pl.BlockSpec((n_ops, tt, RW_HD, LANES), lambda i: (0, i, 0, 0))]
    out_shape = [jax.ShapeDtypeStruct((n_ops, seq_len, RW_HD, LANES), F32)]
    scratch = [pltpu.VMEM((len(RW_OPS) // 2, RW_HEADS, nb * tt, LANES), F32)]
    if lat:
        out_specs.append(pl.BlockSpec((tt, SUBLANES, LANES), lambda i: (i, 0, 0)))
        out_shape.append(jax.ShapeDtypeStruct((seq_len, SUBLANES, LANES), F32))
    else:
        scratch.append(pltpu.VMEM((2, nb * tt, LANES), F32))
    return pl.pallas_call(
        functools.partial(_rw_prep_kernel, lat, nb, tt, n_tiles),
        grid=(n_tiles,),
        in_specs=[
            pl.BlockSpec((nb, tt, RW_COLS), lambda i: (0, i, 0)),
            pl.BlockSpec((nb, SUBLANES, RW_COLS), lambda i: (0, jnp.maximum(i * sub - 1, 0), 0)),
            pl.BlockSpec((nb, SUBLANES, RW_COLS), lambda i: (0, jnp.minimum((i + 1) * sub, last), 0)),
        ] + [full(a) for a in (mu, wg, ww, wa, w0, a0, k_k, k_a, r_k, ones, hsum)],
        out_specs=out_specs + [tok_spec, tok_spec],
        out_shape=out_shape + [tok_sds, tok_sds],
        scratch_shapes=scratch,
        compiler_params=_params(("arbitrary",)),
        name="rw_prep_lat" if lat else "rw_prep_ctx",
    )(u_rw, u_rw, u_rw, mu, wg, ww, wa, w0, a0, k_k, k_a, r_k, ones, hsum)


def _rw_scan_kernel(lat, nv, n_chunks, *refs):
    if lat:
        af_ref, ab_ref, sf_ref, sb_ref, kf_ref, kb_ref, vf_ref, vb_ref, s0_ref, yf_ref, yb_ref, s_scr, sa_scr = refs
    else:
        af_ref, ab_ref, sf_ref, sb_ref, kf_ref, kb_ref, yf_ref, yb_ref, sfin_ref, s_scr, sa_scr = refs
        vf_ref = vb_ref = None
    c = pl.program_id(0)
    chunk = af_ref.shape[1]
    n_part = SCAN_CHAINS // min(nv, SCAN_BLOCKS)
    bcast = lambda row: jnp.broadcast_to(row, (SUBLANES, LANES))

    def total(parts):
        while len(parts) > 1:
            parts = [parts[i] + parts[i + 1] for i in range(0, len(parts), 2)]
        return parts[0]

    @pl.when(c == 0)
    def _():
        if lat:
            s_scr[...] = s0_ref[...]
            for d, (s_ref, first) in enumerate(((sf_ref, 0), (sb_ref, chunk - 1))):
                parts = [None] * SCAN_CHAINS
                for k in range(RW_HD):
                    p = s_scr[d, 0, k] * bcast(s_ref[1, first, pl.ds(k, 1), :])
                    parts[k % SCAN_CHAINS] = p if k < SCAN_CHAINS else parts[k % SCAN_CHAINS] + p
                sa_scr[d, 0] = total(parts)
        else:
            s_scr[...] = jnp.zeros(s_scr.shape, F32)
            sa_scr[...] = jnp.zeros(sa_scr.shape, F32)

    def step(last, tt, carry):
        dirs = ((af_ref, sf_ref, vf_ref, yf_ref, tt, tt + 1, kf_ref),
                (ab_ref, sb_ref, vb_ref, yb_ref, chunk - 1 - tt, chunk - 2 - tt, kb_ref))
        for d, (a_ref, s_ref, v_ref, y_ref, tl, nxt, k_ref) in enumerate(dirs):
            key_row = lambda ref, j, k: bcast(ref[j, tl, pl.ds(k, 1), :])
            if last:
                next_key = lambda k: bcast(k_ref[0, 0, pl.ds(k, 1), :])
            else:
                next_key = lambda k: bcast(s_ref[1, nxt, pl.ds(k, 1), :])
            for first in range(0, nv, SCAN_BLOCKS):
                blocks = range(first, min(first + SCAN_BLOCKS, nv))
                sa = {vb: sa_scr[d, vb] for vb in blocks}
                if lat:
                    vals = {0: v_ref[tl]}
                else:
                    vals = {vb: s_ref[2, tl, vb * SUBLANES:(vb + 1) * SUBLANES, :] for vb in blocks}
                y = {vb: [None] * n_part for vb in blocks}
                san = {vb: [None] * n_part for vb in blocks}
                for k in range(RW_HD):
                    w, kd, bb = key_row(a_ref, 0, k), key_row(a_ref, 1, k), key_row(a_ref, 2, k)
                    r = key_row(s_ref, 0, k)
                    kkn = next_key(k)
                    for vb in blocks:
                        sn = s_scr[d, vb, k] * w - sa[vb] * bb + vals[vb] * kd
                        s_scr[d, vb, k] = sn
                        y[vb][k % n_part] = sn * r if k < n_part else y[vb][k % n_part] + sn * r
                        san[vb][k % n_part] = sn * kkn if k < n_part else san[vb][k % n_part] + sn * kkn
                for vb in blocks:
                    sa_scr[d, vb] = total(san[vb])
                    if lat:
                        y_ref[tl] = total(y[vb])
                    else:
                        y_ref[tl, vb * SUBLANES:(vb + 1) * SUBLANES, :] = total(y[vb])
        return carry

    lax.fori_loop(0, chunk - 1, functools.partial(step, False), 0)
    step(True, chunk - 1, 0)

    if not lat:
        @pl.when(c == n_chunks - 1)
        def _():
            sfin_ref[...] = s_scr[...]


def _rw_scan(lat, ops, vv, s0):
    seq_len = ops.shape[1]
    chunk = SCAN_CHUNK_LAT if lat else SCAN_CHUNK_CTX
    nc = seq_len // chunk
    nv = 1 if lat else RW_HD // SUBLANES
    n_shared = 2 if lat else 3
    s_shape = (2, nv, RW_HD, SUBLANES, LANES)
    state_spec = pl.BlockSpec(s_shape, lambda c: (0,) * 5)
    scratch = [pltpu.VMEM(s_shape, F32), pltpu.VMEM((2, nv, SUBLANES, LANES), F32)]
    op_spec = lambda n, blk, rev: pl.BlockSpec(
        (n, chunk, RW_HD, LANES), lambda c: (blk, (nc - 1 - c) if rev else c, 0, 0))
    last = seq_len - 1
    kk_next = lambda t_of: pl.BlockSpec((1, 1, RW_HD, LANES), lambda c: (RW_OPS.index("kk"), t_of(c), 0, 0))
    in_specs = [op_spec(3, 0, False), op_spec(3, 1, True),
                op_spec(n_shared, 6 // n_shared, False), op_spec(n_shared, 6 // n_shared, True),
                kk_next(lambda c: jnp.minimum((c + 1) * chunk, last)),
                kk_next(lambda c: jnp.maximum((nc - 1 - c) * chunk - 1, 0))]
    args = [ops] * 6
    if lat:
        y_spec = lambda rev: pl.BlockSpec((chunk, SUBLANES, LANES), lambda c: ((nc - 1 - c) if rev else c, 0, 0))
        y_sds = jax.ShapeDtypeStruct((seq_len, SUBLANES, LANES), F32)
        in_specs += [y_spec(False), y_spec(True), state_spec]
        args += [vv, vv, s0]
        out_specs = [y_spec(False), y_spec(True)]
        out_shape = [y_sds, y_sds]
    else:
        y_spec = lambda rev: pl.BlockSpec((chunk, RW_HD, LANES), lambda c: ((nc - 1 - c) if rev else c, 0, 0))
        y_sds = jax.ShapeDtypeStruct((seq_len, RW_HD, LANES), F32)
        out_specs = [y_spec(False), y_spec(True), state_spec]
        out_shape = [y_sds, y_sds, jax.ShapeDtypeStruct(s_shape, F32)]
    return pl.pallas_call(
        functools.partial(_rw_scan_kernel, lat, nv, nc),
        grid=(nc,),
        in_specs=in_specs,
        out_specs=out_specs,
        out_shape=out_shape,
        scratch_shapes=scratch,
        compiler_params=_params(("arbitrary",)),
        name="rw_scan_lat" if lat else "rw_scan_ctx",
    )(*args)


def _rw_post_ctx_kernel(nb, tt, yf_ref, yb_ref, g_ref, gb_ref, lnw_ref, lnb_ref, o_ref):
    for t in range(tt):
        y = yf_ref[t] + yb_ref[t]
        dev = y - jnp.mean(y, axis=0, keepdims=True)
        yn = (dev * lax.rsqrt(jnp.mean(dev * dev, axis=0, keepdims=True) + RW_GN_EPS)).T
        tok = jnp.concatenate([yn[h * nb:(h + 1) * nb, :] for h in range(RW_HEADS)], axis=1)
        o_ref[:, t, :] = (tok * lnw_ref[...] + lnb_ref[...]) * g_ref[:, t, :] + gb_ref[:, t, :]


def _rw_post_lat_kernel(nb, yf_ref, yb_ref, g_ref, gb_ref, lnw_ref, lnb_ref, hs_ref, o_ref):
    grp = LANES // nb
    ys = [yf_ref[:, vi, :] + yb_ref[:, vi, :] for vi in range(SUBLANES)]
    for b in range(nb):
        y = jnp.concatenate([yv[:, b * grp:(b + 1) * grp] for yv in ys], axis=1)
        dev = y - _sel_dot(y, hs_ref[...], EXACT_TERMS) * (1.0 / RW_HD)
        yn = dev * lax.rsqrt(_sel_dot(dev * dev, hs_ref[...], NORM_TERMS) * (1.0 / RW_HD) + RW_GN_EPS)
        o_ref[b] = (yn * lnw_ref[...] + lnb_ref[...]) * g_ref[b] + gb_ref[b]


def _rw_post(lat, yf, yb, g, gb, lnw, lnb, hs):
    nb, seq_len, _ = g.shape
    tt = LANES if lat else SUBLANES
    full = lambda a: pl.BlockSpec(a.shape, lambda i: (0,) * a.ndim)
    tok_spec = pl.BlockSpec((nb, tt, RW_W), lambda i: (0, i, 0))
    if lat:
        y_spec = pl.BlockSpec((tt, SUBLANES, LANES), lambda i: (i, 0, 0))
        body, extra = functools.partial(_rw_post_lat_kernel, nb), [hs]
    else:
        y_spec = pl.BlockSpec((tt, RW_HD, LANES), lambda i: (i, 0, 0))
        body, extra = functools.partial(_rw_post_ctx_kernel, nb, tt), []
    return pl.pallas_call(
        body,
        grid=(seq_len // tt,),
        in_specs=[y_spec, y_spec, tok_spec, tok_spec, full(lnw), full(lnb)] + [full(a) for a in extra],
        out_specs=tok_spec,
        out_shape=jax.ShapeDtypeStruct((nb, seq_len, RW_W), F32),
        compiler_params=_params(("parallel",)),
        name="rw_post_lat" if lat else "rw_post_ctx",
    )(yf, yb, g, gb, lnw, lnb, *extra)


def _rw_mixer(lat, u_rw, s0, prep_params, lnw, lnb, hs):
    nb = u_rw.shape[0]
    nvb = RW_HD // SUBLANES
    if lat:
        ops, vv, g, gb = _rw_prep(True, u_rw, *prep_params)
        grp = LANES // (nb * RW_HEADS)
        s0 = s0.reshape(nb, 2, RW_HEADS, grp, SUBLANES, RW_HD)
        s0 = jnp.transpose(s0, (1, 5, 4, 0, 2, 3)).reshape(2, 1, RW_HD, SUBLANES, LANES)
        yf, yb = _rw_scan(True, ops, vv, s0)
        return _rw_post(True, yf, yb, g, gb, lnw, lnb, hs), None
    ops, g, gb = _rw_prep(False, u_rw, *prep_params)
    yf, yb, sfin = _rw_scan(False, ops, None, None)
    sfin = sfin.reshape(2, nvb, RW_HD, SUBLANES, RW_HEADS, nb)
    sfin = jnp.transpose(sfin, (5, 0, 4, 1, 3, 2)).reshape(nb, 2, RW_HEADS, RW_HD, RW_HD)
    return _rw_post(False, yf, yb, g, gb, lnw, lnb, hs), sfin


def _softplus(x):
    return jnp.maximum(x, 0.0) + jnp.log1p(jnp.exp(-jnp.abs(x)))


def _ssd_kernel(reverse, n_blocks, n_sub, u_ref, up_ref, un_ref, h0_ref, cw_ref, cb_ref, dtb_ref, a_ref,
                tri_ref, exp_ref, *rest):
    if reverse:
        yf_ref, nw_ref, out_ref, hfin_ref, h_scr = rest
    else:
        dsk_ref, out_ref, hfin_ref, h_scr = rest
    s = pl.program_id(1)
    blk = (n_blocks - 1 - s) if reverse else s

    @pl.when(s == 0)
    def _():
        h_scr[...] = h0_ref[...]

    q = SSD_CHUNK
    rows = n_sub * q
    u = u_ref[...]
    xbc = u[:, SSD_W:SSD_W + 512]
    row = _iota(xbc.shape, 0)
    has_prev = (blk > 0).astype(F32)
    has_next = (blk < n_blocks - 1).astype(F32)
    prev = jnp.where(row == 0, up_ref[SUBLANES - 1:SUBLANES, SSD_W:SSD_W + 512] * has_prev, pltpu.roll(xbc, 1, 0))
    nxt = jnp.where(row == rows - 1, un_ref[0:1, SSD_W:SSD_W + 512] * has_next, pltpu.roll(xbc, rows - 1, 0))
    xc = _silu(cw_ref[0:1, :] * prev + cw_ref[1:2, :] * xbc + cw_ref[2:3, :] * nxt + cb_ref[...])
    dt_all = _softplus(u[:, SSD_W + 512:] + dtb_ref[...])
    dt_e_all = _sel_dot(dt_all, exp_ref[...], EXACT_TERMS)
    dta_all = dt_all * a_ref[...]

    ii = _iota((q, q), 0)
    jj = _iota((q, q), 1)
    tri = (jj >= ii) if reverse else (jj <= ii)
    lane = _iota((q, LANES), 1)
    rowc = _iota((LANES, 1), 0)
    state = [h_scr[g] for g in range(2)]
    order = range(n_sub - 1, -1, -1) if reverse else range(n_sub)
    for ci in order:
        rs = slice(ci * q, (ci + 1) * q)
        x = xc[rs, :SSD_W]
        b_all = xc[rs, SSD_W:SSD_W + LANES]
        c_all = xc[rs, SSD_W + LANES:]
        cs = _sel_dot(dta_all[rs], tri_ref[...], EXACT_TERMS, sel_first=True)
        cs_t = cs.T
        cs_e = _sel_dot(cs, exp_ref[...], EXACT_TERMS)
        tot = cs_e[0:1, :] if reverse else cs_e[q - 1:q, :]
        xdt = x * dt_e_all[rs]
        dec_in = jnp.exp(tot - cs_e)
        dec_out = jnp.exp(cs_e)
        cdec = jnp.exp(tot)
        ys = []
        for g in range(2):
            bg = b_all[:, g * 64:(g + 1) * 64]
            cg = c_all[:, g * 64:(g + 1) * 64]
            gm = _bdot_nt(cg, bg)
            sl = slice(g * LANES, (g + 1) * LANES)
            xdt_g = xdt[:, sl]
            yd = []
            for hh in range(2):
                h = 2 * g + hh
                diff = cs[:, h:h + 1] - cs_t[h:h + 1, :]
                lm = jnp.exp(jnp.where(tri, diff, -jnp.inf))
                yd.append(_bdot(gm * lm, xdt_g))
            hg = state[g]
            ys.append(jnp.where(lane < 64, yd[0], yd[1]) + _bdot_nt(cg, hg) * dec_out[:, sl])
            st = _bdot((xdt_g * dec_in[:, sl]).T, bg)
            cd = cdec[:, sl]
            state[g] = hg * jnp.where(rowc < 64, cd[:, 0:1], cd[:, 64:65]) + st
        y = jnp.concatenate(ys, axis=1)
        if reverse:
            val = (yf_ref[rs, :] + y) * _silu(u[rs, :SSD_W])
            out_ref[rs, :] = _rms(val, nw_ref[...])
        else:
            out_ref[rs, :] = y + dsk_ref[...] * x
    for g in range(2):
        h_scr[g] = state[g]
        hfin_ref[g] = state[g]


def _ssd_sweep(reverse, u_ssd, nb, seq_len, h0, cw, cb, dtb, a_neg, extra):
    n_sub = min(SSD_CHUNKS_PER_STEP, seq_len // SSD_CHUNK)
    rows = n_sub * SSD_CHUNK
    nc = seq_len // rows
    sub_per_chunk = rows // SUBLANES
    last_sub = u_ssd.shape[0] // SUBLANES - 1
    chunk = (lambda s: nc - 1 - s) if reverse else (lambda s: s)
    rowblk = lambda b, s: b * nc + chunk(s)
    full = lambda a: pl.BlockSpec(a.shape, lambda b, s: (0,) * a.ndim)
    steps = np.arange(SSD_CHUNK)
    tri = jnp.asarray((steps[None, :] >= steps[:, None]) if reverse else (steps[None, :] <= steps[:, None]), BF16)
    expand = jnp.asarray(np.arange(LANES)[:, None] == np.arange(SSD_W)[None, :] // 64, BF16)
    in_specs = [
        pl.BlockSpec((rows, SSD_PAD), lambda b, s: (rowblk(b, s), 0)),
        pl.BlockSpec((SUBLANES, SSD_PAD), lambda b, s: (jnp.maximum(rowblk(b, s) * sub_per_chunk - 1, 0), 0)),
        pl.BlockSpec((SUBLANES, SSD_PAD),
                     lambda b, s: (jnp.minimum((rowblk(b, s) + 1) * sub_per_chunk, last_sub), 0)),
        pl.BlockSpec((None, 2, LANES, 64), lambda b, s: (b, 0, 0, 0)),
        full(cw), full(cb), full(dtb), full(a_neg), full(tri), full(expand),
    ]
    args = [u_ssd, u_ssd, u_ssd, h0, cw, cb, dtb, a_neg, tri, expand]
    if reverse:
        yf, nw = extra
        in_specs += [pl.BlockSpec((rows, SSD_W), lambda b, s: (rowblk(b, s), 0)), full(nw)]
        args += [yf, nw]
    else:
        in_specs += [full(extra)]
        args += [extra]
    return pl.pallas_call(
        functools.partial(_ssd_kernel, reverse, nc, n_sub),
        grid=(nb, nc),
        in_specs=in_specs,
        out_specs=[pl.BlockSpec((rows, SSD_W), lambda b, s: (rowblk(b, s), 0)),
                   pl.BlockSpec((None, 2, LANES, 64), lambda b, s: (b, 0, 0, 0))],
        out_shape=[jax.ShapeDtypeStruct((nb * seq_len, SSD_W), F32),
                   jax.ShapeDtypeStruct((nb, 2, LANES, 64), F32)],
        scratch_shapes=[pltpu.VMEM((2, LANES, 64), F32)],
        compiler_params=_params(("parallel", "arbitrary")),
        name="ssd_bwd" if reverse else "ssd_fwd",
    )(*args)


def _ssd_mixer(u_ssd, nb, seq_len, h0, cw, cb, dtb, a_neg, dsk, nw):
    yf, hf = _ssd_sweep(False, u_ssd, nb, seq_len, h0[:, 0], cw, cb, dtb[0], a_neg[0], dsk)
    out, hb = _ssd_sweep(True, u_ssd, nb, seq_len, h0[:, 1], cw, cb, dtb[1], a_neg[1], (yf, nw))
    return out, jnp.stack([hf, hb], axis=1).reshape(nb, 2, 4, 64, 64)


def _route(scores, bias):
    n_grp = N_EXPERTS // EXP_PER_GROUP
    sel = scores + bias
    sc = [scores[e:e + 1, :] for e in range(N_EXPERTS)]
    sl = [sel[e:e + 1, :] for e in range(N_EXPERTS)]
    as_int = lambda cond: jnp.where(cond, 1, 0)
    top2, grp = [], []
    for g in range(n_grp):
        members = range(g * EXP_PER_GROUP, (g + 1) * EXP_PER_GROUP)
        total = None
        for e in members:
            ahead = [as_int(sl[o] >= sl[e]) if o < e else as_int(sl[o] > sl[e]) for o in members if o != e]
            top2.append(ahead[0] + ahead[1] + ahead[2] < 2)
            kept = jnp.where(top2[e], sl[e], 0.0)
            total = kept if total is None else total + kept
        grp.append(total)
    rows = []
    for g in range(n_grp):
        beaten = [as_int(grp[o] >= grp[g]) if o < g else as_int(grp[o] > grp[g]) for o in range(n_grp) if o != g]
        best = beaten[0] + beaten[1] + beaten[2] == 0
        members = range(g * EXP_PER_GROUP, (g + 1) * EXP_PER_GROUP)
        chosen = [jnp.where(best, as_int(top2[e]), 0) > 0 for e in members]
        cw = [jnp.where(c, sc[e], 0.0) for c, e in zip(chosen, members)]
        den = cw[0] + cw[1] + cw[2] + cw[3]
        rows += [jnp.where(c, w / jnp.where(c, den, 1.0), 0.0) for c, w in zip(chosen, cw)]
    return jnp.concatenate(rows, axis=0)


def _outproj_kernel(oatt_ref, orw_ref, ossd_ref, x_ref, mod_ref, wout_ref,
                    n2_ref, rw_ref, rb_ref, x1_ref, h2_ref, comb_ref):
    o = jnp.concatenate([oatt_ref[...], orw_ref[...], ossd_ref[...]], axis=1)
    x1 = x_ref[...] + mod_ref[2] * _bdot(o, wout_ref[...])
    h2 = _rms(x1, n2_ref[...]) * (1.0 + mod_ref[4]) + mod_ref[3]
    comb = _route(_sigmoid(_bdot_nt(rw_ref[...], h2)), rb_ref[...])
    x1_ref[...] = x1
    h2_ref[...] = h2.astype(BF16)
    comb_ref[...] = jnp.concatenate([comb, jnp.zeros((LANES - N_EXPERTS, comb.shape[1]), F32)], axis=0).T


def _outproj(o_att, o_rw, o_ssd, x, mod_l, row_fn, w_out_bf, n2, router_w, router_b):
    n_tok = x.shape[0]
    tile = lambda w: pl.BlockSpec((TOKEN_TILE, w), lambda i: (i, 0))
    full = lambda a: pl.BlockSpec(a.shape, lambda i: (0,) * a.ndim)
    return pl.pallas_call(
        _outproj_kernel,
        grid=(n_tok // TOKEN_TILE,),
        in_specs=[tile(ATT_W), tile(RW_W), tile(SSD_W), tile(D_MODEL),
                  _mod_spec(row_fn)] + [full(a) for a in (w_out_bf, n2, router_w, router_b)],
        out_specs=[tile(D_MODEL), tile(D_MODEL), tile(LANES)],
        out_shape=[jax.ShapeDtypeStruct((n_tok, D_MODEL), F32),
                   jax.ShapeDtypeStruct((n_tok, D_MODEL), BF16),
                   jax.ShapeDtypeStruct((n_tok, LANES), F32)],
        compiler_params=_params(("parallel",)),
        name="outproj",
    )(o_att, o_rw, o_ssd, x, mod_l, w_out_bf, n2, router_w, router_b)


def _moe_kernel(h2_ref, comb_ref, x1_ref, mod_ref, wg_ref, wu_ref, wd_ref, o_ref, acc_ref):
    s = pl.program_id(1)

    @pl.when(s == 0)
    def _():
        acc_ref[...] = jnp.zeros_like(acc_ref)

    h2 = h2_ref[...]
    comb = comb_ref[...]
    lane = _iota(comb.shape, 1)
    weighted = []
    for j in range(MOE_EXPERTS_PER_STEP):
        hid = _silu(jnp.dot(h2, wg_ref[j], preferred_element_type=F32)) * jnp.dot(
            h2, wu_ref[j], preferred_element_type=F32)
        wcol = jnp.sum(jnp.where(lane == s * MOE_EXPERTS_PER_STEP + j, comb, 0.0), axis=1, keepdims=True)
        weighted.append((hid * wcol).astype(BF16))
    acc_ref[...] += jnp.dot(jnp.concatenate(weighted, axis=1), wd_ref[...].reshape(-1, D_MODEL),
                            preferred_element_type=F32)

    @pl.when(s == N_EXPERTS // MOE_EXPERTS_PER_STEP - 1)
    def _():
        o_ref[...] = x1_ref[...] + mod_ref[5] * acc_ref[...]


def _moe(h2, comb, x1, mod_l, row_fn, wg_bf, wu_bf, wd_bf):
    n_tok = x1.shape[0]
    tile = lambda w: pl.BlockSpec((MOE_TILE, w), lambda i, e: (i, 0))
    return pl.pallas_call(
        _moe_kernel,
        grid=(n_tok // MOE_TILE, N_EXPERTS // MOE_EXPERTS_PER_STEP),
        in_specs=[tile(D_MODEL), tile(LANES), tile(D_MODEL), _mod_spec(row_fn),
                  pl.BlockSpec((MOE_EXPERTS_PER_STEP, D_MODEL, D_FF), lambda i, e: (e, 0, 0)),
                  pl.BlockSpec((MOE_EXPERTS_PER_STEP, D_MODEL, D_FF), lambda i, e: (e, 0, 0)),
                  pl.BlockSpec((MOE_EXPERTS_PER_STEP, D_FF, D_MODEL), lambda i, e: (e, 0, 0))],
        out_specs=tile(D_MODEL),
        out_shape=jax.ShapeDtypeStruct((n_tok, D_MODEL), F32),
        scratch_shapes=[pltpu.VMEM((MOE_TILE, D_MODEL), F32)],
        compiler_params=_params(("parallel", "arbitrary")),
        name="moe",
    )(h2, comb, x1, mod_l, wg_bf, wu_bf, wd_bf)


def _rope_tables(seq_len):
    t = np.arange(seq_len)
    pos = np.stack([t // GRID_W, t % GRID_W], 0).astype(np.float32)
    n_freq = HEAD_DIM // 4
    inv = jnp.asarray(ROPE_BASE, F32) ** (-jnp.arange(n_freq, dtype=F32) / n_freq)
    ang = jnp.asarray(pos)[:, :, None] * inv
    cos, sin = jnp.cos(ang), jnp.sin(ang)
    cos_h = jnp.concatenate([cos[0], cos[0], cos[1], cos[1]], -1)
    sin_h = jnp.concatenate([-sin[0], sin[0], -sin[1], sin[1]], -1)
    return jnp.tile(cos_h, (1, 2)), jnp.tile(sin_h, (1, 2))


def _pad_rows(w, start, total):
    return jnp.pad(w, ((0, 0),) * (w.ndim - 2) + ((start, total - start - w.shape[-2]), (0, 0)))


def _value_order(x, axis, start, grp):
    if grp is None:
        return x
    axis %= x.ndim
    sl = lambda a, b: lax.slice_in_dim(x, a, b, axis=axis)
    mid = sl(start, start + RW_W)
    shape = mid.shape
    mid = mid.reshape(shape[:axis] + (RW_HEADS, grp, RW_HD // grp) + shape[axis + 1:])
    order = tuple(range(axis)) + (axis + 2, axis, axis + 1) + tuple(range(axis + 3, mid.ndim))
    mid = jnp.transpose(mid, order).reshape(shape)
    return jnp.concatenate([sl(0, start), mid, sl(start + RW_W, x.shape[axis])], axis=axis)


def _rw_layer_params(l, P, grp):
    row = lambda a: a.reshape(1, -1)
    v0 = 2 * RW_W
    w_in = jnp.pad(P["w_in"][l], ((0, 0), (0, SSD_PAD - SSD_COLS))).astype(BF16)
    head = np.arange(RW_W) // RW_HD if grp is None else (np.arange(RW_W) // grp) % RW_HEADS
    rp = {}
    rp["w_in"] = _value_order(w_in, 1, ATT_COLS + v0, grp)
    rp["w_out"] = _value_order(P["w_out"][l].astype(BF16), 0, ATT_W, grp)
    rp["prep"] = (
        _value_order(P["rw_mu"][l], 1, v0, grp),
        _pad_rows(_value_order(P["rw_g_up"][l], 1, 0, grp), 64, LANES),
        _pad_rows(P["rw_w_up"][l], 0, LANES),
        _pad_rows(P["rw_a_up"][l], 32, LANES),
        P["rw_w0"][l], P["rw_a0"][l],
        row(P["rw_k_k"][l]), row(P["rw_k_a"][l]), row(P["rw_r_k"][l]),
        _head_selector(np.arange(RW_W) // RW_HD), _head_selector(np.arange(RW_W) // RW_HD, head),
    )
    rp["lnw"] = _value_order(row(P["rw_ln_w"][l]), 1, 0, grp)
    rp["lnb"] = _value_order(row(P["rw_ln_b"][l]), 1, 0, grp)
    rp["hs"] = _head_selector(head)
    return rp


def _layer_params(l, P):
    row = lambda a: a.reshape(1, -1)
    lp = {}
    lp["n1"] = row(P["norm1_w"][l])
    lp["n2"] = row(P["norm2_w"][l])
    lp["qw"] = row(jnp.tile(P["q_norm_w"][l], ATT_HEADS))
    lp["kw"] = row(jnp.tile(P["k_norm_w"][l], ATT_KV_HEADS))
    lp["sink"] = P["attn_sink"][l]
    pad_heads = lambda a: jnp.pad(a, ((0, 0), (0, LANES - a.shape[1]))).reshape(2, 1, LANES)
    lp["ssd"] = (
        P["ssd_conv_w"][l], row(P["ssd_conv_b"][l]),
        pad_heads(P["ssd_dt_bias"][l]), pad_heads(-jnp.exp(P["ssd_a_log"][l])),
        row(jnp.repeat(P["ssd_d"][l], 64)), row(P["ssd_norm_w"][l]),
    )
    lp["wg"] = P["exp_gate"][l].astype(BF16)
    lp["wu"] = P["exp_up"][l].astype(BF16)
    lp["wd"] = P["exp_down"][l].astype(BF16)
    return lp


def _trunk(x, nb, seq_len, is_ctx, mod, layers, rw_layers, router, cache_k, cache_v, state_rwkv, state_ssd):
    tiles_per_seq = seq_len // TOKEN_TILE
    moe_per_seq = max(seq_len // MOE_TILE, 1)
    if is_ctx:
        row_tok = lambda i: 0
        row_moe = lambda i: 0
        rope = None
    else:
        row_tok = lambda i: 1 + i // tiles_per_seq
        row_moe = lambda i: 1 + i // moe_per_seq
        rope = _rope_tables(seq_len)
    new_k, new_v, new_rw, new_ssd = [], [], [], []
    for l, (lp, rp) in enumerate(zip(layers, rw_layers)):
        q, k, v, u_rw, u_ssd = _inproj(x, mod[l], row_tok, lp["n1"], rp["w_in"], lp["qw"], lp["kw"], rope, seq_len)
        if is_ctx:
            o_att = _attn_ctx(lp["sink"], q, k, v, nb, seq_len)
            rw0 = None
            ssd0 = jnp.zeros((nb, 2, 2, LANES, 64), F32)
        else:
            kc = cache_k[:, l].reshape(nb, -1, KV_W)
            vc = cache_v[:, l].reshape(nb, -1, KV_W)
            o_att = _attn_lat(lp["sink"], q, k, v, kc, vc, nb, seq_len)
            rw0 = state_rwkv[:, l]
            ssd0 = state_ssd[:, l].reshape(nb, 2, 2, LANES, 64)
        o_rw, s_rw = _rw_mixer(not is_ctx, u_rw.reshape(nb, seq_len, RW_COLS), rw0, rp["prep"],
                               rp["lnw"], rp["lnb"], rp["hs"])
        o_ssd, s_ssd = _ssd_mixer(u_ssd, nb, seq_len, ssd0, *lp["ssd"])
        if is_ctx:
            new_k.append(k.reshape(nb, seq_len, ATT_KV_HEADS, HEAD_DIM))
            new_v.append(v.reshape(nb, seq_len, ATT_KV_HEADS, HEAD_DIM))
            new_rw.append(s_rw)
            new_ssd.append(s_ssd)
        x1, h2, comb = _outproj(o_att, o_rw.reshape(nb * seq_len, RW_W), o_ssd, x, mod[l], row_tok, rp["w_out"],
                                lp["n2"], *router)
        x = _moe(h2, comb, x1, mod[l], row_moe, lp["wg"], lp["wu"], lp["wd"])
    if is_ctx:
        return x, tuple(jnp.stack(t, 1) for t in (new_k, new_v, new_rw, new_ssd))
    return x, None


def kernel(x_prompt, x_sample, cache_k, cache_v, state_rwkv, state_ssd, c, c_ctx, w_mod, b_mod, norm1_w, norm2_w, w_in, w_out, q_norm_w, k_norm_w, attn_sink, rw_mu, rw_w0, rw_w_up, rw_a0, rw_a_up, rw_g_up, rw_k_k, rw_k_a, rw_r_k, rw_ln_w, rw_ln_b, ssd_conv_w, ssd_conv_b, ssd_dt_bias, ssd_a_log, ssd_d, ssd_norm_w, router_w, router_bias, exp_gate, exp_up, exp_down):
    P = dict(norm1_w=norm1_w, norm2_w=norm2_w, w_in=w_in, w_out=w_out, q_norm_w=q_norm_w, k_norm_w=k_norm_w,
             attn_sink=attn_sink, rw_mu=rw_mu, rw_w0=rw_w0, rw_w_up=rw_w_up, rw_a0=rw_a0, rw_a_up=rw_a_up,
             rw_g_up=rw_g_up, rw_k_k=rw_k_k, rw_k_a=rw_k_a, rw_r_k=rw_r_k, rw_ln_w=rw_ln_w, rw_ln_b=rw_ln_b,
             ssd_conv_w=ssd_conv_w, ssd_conv_b=ssd_conv_b, ssd_dt_bias=ssd_dt_bias, ssd_a_log=ssd_a_log,
             ssd_d=ssd_d, ssd_norm_w=ssd_norm_w, exp_gate=exp_gate, exp_up=exp_up, exp_down=exp_down)
    nb_ctx, seq_ctx, _ = x_prompt.shape
    nb_lat, seq_lat, _ = x_sample.shape
    assert nb_lat + 1 <= SUBLANES and (nb_ctx * seq_ctx) % MOE_TILE == 0 and seq_lat % MOE_TILE == 0
    assert nb_ctx * RW_HEADS == LANES and LANES % (nb_lat * RW_HEADS) == 0
    cond8 = jnp.zeros((SUBLANES, D_MODEL), F32).at[0].set(c_ctx).at[1:1 + nb_lat].set(c)
    mod = _modulation(cond8, w_mod, b_mod)
    layers = [_layer_params(l, P) for l in range(DEPTH)]
    rw_ctx = [_rw_layer_params(l, P, None) for l in range(DEPTH)]
    rw_lat = [_rw_layer_params(l, P, LANES // (nb_lat * RW_HEADS)) for l in range(DEPTH)]
    router = (router_w.T.astype(BF16), router_bias.reshape(N_EXPERTS, 1))
    y_prompt, ctx_state = _trunk(x_prompt.reshape(-1, D_MODEL), nb_ctx, seq_ctx, True, mod, layers, rw_ctx, router,
                                 None, None, None, None)
    y_sample, _ = _trunk(x_sample.reshape(-1, D_MODEL), nb_lat, seq_lat, False, mod, layers, rw_lat, router,
                         cache_k, cache_v, state_rwkv, state_ssd)
    return (y_prompt.reshape(x_prompt.shape), y_sample.reshape(x_sample.shape)) + ctx_state
```
